```python
import math
import jax, jax.numpy as jnp
from jax import lax
import numpy as np

D_MODEL = 2048
BATCH = 4
SEQ = 2048
DEPTH = 4
DEC_BATCH = 128
DEC_SEQ = 8
PAST_LEN = 16384
PAGE_SIZE = 128

N_MIXERS = 3
LAYER_KIND = tuple(i % N_MIXERS for i in range(DEPTH))
N_A = sum(1 for kd in LAYER_KIND if kd == 0)
N_B = sum(1 for kd in LAYER_KIND if kd == 1)
N_C = sum(1 for kd in LAYER_KIND if kd == 2)

D_FF = 5632
EPS = 1e-6
N_ADA = 9

RW_HEAD = 64
RW_HEADS = D_MODEL // RW_HEAD
RW_DECAY_LORA = 96
RW_ICLR_LORA = 96
RW_GATE_LORA = 256
RW_GN_EPS = 64e-5

GLA_HEADS = 4
GLA_DK = D_MODEL // 2
GLA_DV = D_MODEL
GLA_HK = GLA_DK // GLA_HEADS
GLA_HV = GLA_DV // GLA_HEADS
GLA_GATE_LORA = 16
GLA_GATE_NORM = 16.0
GLA_CHUNK = 64
GLA_SPLITS = (GLA_DK, 2 * GLA_DK, 2 * GLA_DK + GLA_DV, 2 * GLA_DK + 2 * GLA_DV)

S5_GROUP = 16
S5_GROUPS = D_MODEL // S5_GROUP
S5_STATE = 64
S5_DT_MIN = 1e-3
S5_DT_MAX = 1e-1

kernel_name = "hybrid_rwkv7_gla_s5_macaron_adaln_step"


def rms_norm(x, g):
    xf = x.astype(jnp.float32)
    y = xf * lax.rsqrt(jnp.mean(xf * xf, axis=-1, keepdims=True) + EPS)
    return (y * g.astype(jnp.float32)).astype(x.dtype)


def modulate(h, shift, scale):
    return h * (1.0 + scale[:, None, :]) + shift[:, None, :]


def swiglu(h, w_in, w_out):
    gate, up = jnp.split(h @ w_in, 2, axis=-1)
    return (jax.nn.silu(gate) * up) @ w_out


def time_major(a):
    return jnp.moveaxis(a, 1, 0)


def rwkv7_mix(h, shift_prev, wkv0, mu, w_rkv, w0, w1, w2, a0, a1, a2, g1, g2, k_k, k_a, r_k, gn_w, gn_b, w_o):
    bsz, t_len, _ = h.shape
    f32 = jnp.float32
    h_prev = jnp.concatenate([shift_prev[:, None, :].astype(h.dtype), h[:, :-1]], axis=1)
    xm = h[None] + (h_prev - h)[None] * mu[:, None, None, :]
    r, k, v = jnp.einsum('pbtd,pde->pbte', xm[:3], w_rkv)
    x_w, x_a, x_g = xm[3], xm[4], xm[5]
    w_log = -jax.nn.softplus(-(w0 + jnp.tanh(x_w @ w1) @ w2).astype(f32)) - 0.5
    decay = jnp.exp(-jnp.exp(w_log))
    a = jax.nn.sigmoid((a0 + (x_a @ a1) @ a2).astype(f32))
    g = jax.nn.sigmoid(x_g @ g1) @ g2
    heads = lambda z: z.astype(f32).reshape(bsz, t_len, RW_HEADS, RW_HEAD)
    r, k, v, decay, a = map(heads, (r, k, v, decay, a))
    kk = k * k_k.astype(f32).reshape(RW_HEADS, RW_HEAD)
    kk = kk / jnp.maximum(jnp.linalg.norm(kk, axis=-1, keepdims=True), 1e-12)
    k = k * (1.0 + (a - 1.0) * k_a.astype(f32).reshape(RW_HEADS, RW_HEAD))

    def step(S, inp):
        r_t, w_t, k_t, v_t, a_t, b_t = inp
        sa = jnp.einsum('bhvk,bhk->bhv', S, a_t)
        S = S * w_t[:, :, None, :] + sa[..., None] * b_t[:, :, None, :] + v_t[..., None] * k_t[:, :, None, :]
        return S, jnp.einsum('bhvk,bhk->bhv', S, r_t)

    xs = tuple(map(time_major, (r, decay, k, v, -kk, kk * a)))
    wkv, y = lax.scan(step, wkv0.astype(f32), xs)
    y = time_major(y)
    mean = jnp.mean(y, axis=-1, keepdims=True)
    var = jnp.mean(jnp.square(y - mean), axis=-1, keepdims=True)
    y = ((y - mean) * lax.rsqrt(var + RW_GN_EPS)).reshape(bsz, t_len, D_MODEL) * gn_w + gn_b
    bonus = jnp.sum(r * k * r_k.astype(f32), axis=-1, keepdims=True) * v
    y = y + bonus.reshape(bsz, t_len, D_MODEL)
    out = (y.astype(h.dtype) * g) @ w_o
    return out, h[:, -1].astype(shift_prev.dtype), wkv.astype(wkv0.dtype)


def to_chunks(z, n_chunks):
    bsz, t_len, nh, e = z.shape
    return z.reshape(bsz, n_chunks, t_len // n_chunks, nh, e).transpose(1, 0, 3, 2, 4)


def gla_mix(h, S0, w_in, a_w2, a_b, o_norm, w_o):
    bsz, t_len, _ = h.shape
    f32 = jnp.float32
    q, k, v, r, a_low = jnp.split(h @ w_in, GLA_SPLITS, axis=-1)
    log_a = jax.nn.log_sigmoid((a_low @ a_w2 + a_b).astype(f32)) / GLA_GATE_NORM
    chunk = math.gcd(t_len, GLA_CHUNK)
    nc = t_len // chunk
    hd = lambda z: z.astype(f32).reshape(bsz, t_len, GLA_HEADS, -1)
    qc = to_chunks(hd(q) * GLA_HK ** -0.5, nc)
    kc, vc, lac = to_chunks(hd(k), nc), to_chunks(hd(v), nc), to_chunks(hd(log_a), nc)
    causal = jnp.tril(jnp.ones((chunk, chunk), dtype=bool))

    def chunk_step(S, inp):
        q_c, k_c, v_c, la_c = inp
        b = jnp.cumsum(la_c, axis=2)
        b_last = b[:, :, -1:, :]
        q_in = q_c * jnp.exp(b)
        k_in = k_c * jnp.exp(-b)
        att = jnp.where(causal, jnp.einsum('bhik,bhjk->bhij', q_in, k_in), 0.0)
        o = jnp.einsum('bhij,bhjv->bhiv', att, v_c) + jnp.einsum('bhik,bhkv->bhiv', q_in, S)
        S = S * jnp.exp(b_last)[:, :, 0, :, None] + jnp.einsum('bhjk,bhjv->bhkv', k_c * jnp.exp(b_last - b), v_c)
        return S, o

    S, o = lax.scan(chunk_step, S0.astype(f32), (qc, kc, vc, lac))
    o = o.transpose(1, 0, 3, 2, 4).reshape(bsz, t_len, GLA_HEADS, GLA_HV)
    o = o * lax.rsqrt(jnp.mean(o * o, axis=-1, keepdims=True) + EPS) * o_norm.astype(f32)
    o = o.reshape(bsz, t_len, GLA_DV).astype(h.dtype) * jax.nn.silu(r)
    return o @ w_o, S.astype(S0.dtype)


def complex_affine_combine(e1, e2):
    a1r, a1i, b1r, b1i = e1
    a2r, a2i, b2r, b2i = e2
    return (a1r * a2r - a1i * a2i, a1r * a2i + a1i * a2r,
            a2r * b1r - a2i * b1i + b2r, a2r * b1i + a2i * b1r + b2i)


def s5_mix(h, x0_re, x0_im, A_re, A_im, log_dt, B_re, B_im, C_re, C_im, D_skip, glu_w, glu_b):
    bsz, t_len, _ = h.shape
    f32 = jnp.float32
    A_re, A_im = A_re.astype(f32), A_im.astype(f32)
    dt = jnp.exp(log_dt.astype(f32))[:, None]
    mag = jnp.exp(A_re * dt)
    ab_re, ab_im = mag * jnp.cos(A_im * dt), mag * jnp.sin(A_im * dt)
    den = A_re * A_re + A_im * A_im
    nr = ab_re - 1.0
    cf_re = (nr * A_re + ab_im * A_im) / den
    cf_im = (ab_im * A_re - nr * A_im) / den
    B_re, B_im = B_re.astype(f32), B_im.astype(f32)
    bb_re = cf_re[..., None] * B_re - cf_im[..., None] * B_im
    bb_im = cf_re[..., None] * B_im + cf_im[..., None] * B_re
    u = h.astype(f32).reshape(bsz, t_len, S5_GROUPS, S5_GROUP)
    bu_re = jnp.einsum('btgc,gnc->btgn', u, bb_re)
    bu_im = jnp.einsum('btgc,gnc->btgn', u, bb_im)
    xr0, xi0 = x0_re.astype(f32), x0_im.astype(f32)
    bu_re = bu_re.at[:, 0].add(ab_re * xr0 - ab_im * xi0)
    bu_im = bu_im.at[:, 0].add(ab_re * xi0 + ab_im * xr0)
    a_re = jnp.broadcast_to(ab_re, (1, t_len) + ab_re.shape)
    a_im = jnp.broadcast_to(ab_im, (1, t_len) + ab_im.shape)
    _, _, xs_re, xs_im = lax.associative_scan(complex_affine_combine, (a_re, a_im, bu_re, bu_im), axis=1)
    y = (jnp.einsum('btgn,gcn->btgc', xs_re, C_re.astype(f32))
         - jnp.einsum('btgn,gcn->btgc', xs_im, C_im.astype(f32)))
    y = y.reshape(bsz, t_len, D_MODEL) + D_skip.astype(f32) * h.astype(f32)
    z = jax.nn.gelu(y).astype(h.dtype) @ glu_w + glu_b
    val, gate = jnp.split(z, 2, axis=-1)
    return val * jax.nn.sigmoid(gate), xs_re[:, -1].astype(x0_re.dtype), xs_im[:, -1].astype(x0_im.dtype)


def trunk(x, c, wkv, shift, gla, s5_re, s5_im, shared, rw, gl, s5, final_g):
    norm_g, ada_w, ada_b, ffn_w_in, ffn_w_out = shared
    ia = ib = ic = 0
    new_wkv, new_shift, new_gla, new_re, new_im = [], [], [], [], []
    for l in range(DEPTH):
        ada = (jax.nn.silu(c) @ ada_w[l] + ada_b[l]).reshape(c.shape[0], N_ADA, D_MODEL)
        h = modulate(rms_norm(x, norm_g[l, 0]), ada[:, 0], ada[:, 1])
        x = x + 0.5 * ada[:, 2, None, :] * swiglu(h, ffn_w_in[l, 0], ffn_w_out[l, 0])
        h = modulate(rms_norm(x, norm_g[l, 1]), ada[:, 3], ada[:, 4])
        kind = LAYER_KIND[l]
        if kind == 0:
            m, sh, st = rwkv7_mix(h, shift[ia], wkv[ia], *(p[ia] for p in rw))
            new_shift.append(sh)
            new_wkv.append(st)
            ia += 1
        elif kind == 1:
            m, st = gla_mix(h, gla[ib], *(p[ib] for p in gl))
            new_gla.append(st)
            ib += 1
        else:
            m, sr, si = s5_mix(h, s5_re[ic], s5_im[ic], *(p[ic] for p in s5))
            new_re.append(sr)
            new_im.append(si)
            ic += 1
        x = x + ada[:, 5, None, :] * m
        h = modulate(rms_norm(x, norm_g[l, 2]), ada[:, 6], ada[:, 7])
        x = x + 0.5 * ada[:, 8, None, :] * swiglu(h, ffn_w_in[l, 1], ffn_w_out[l, 1])
    y = rms_norm(x, final_g)
    return (y, jnp.stack(new_wkv), jnp.stack(new_shift), jnp.stack(new_gla), jnp.stack(new_re), jnp.stack(new_im))


def setup_inputs(seed: int = 0) -> dict:
    key = jax.random.key(seed)
    ks = iter(jax.random.split(key, 64))
    nrm = lambda shape, scale=1.0: scale * jax.random.normal(next(ks), shape, jnp.float32)
    uni = lambda shape, lo, hi: jax.random.uniform(next(ks), shape, jnp.float32, lo, hi)
    D, F = D_MODEL, D_FF
    return {
        "x_prompt": nrm((BATCH, SEQ, D)),
        "x_sample": nrm((DEC_BATCH, DEC_SEQ, D)),
        "state_rwkv_wkv": nrm((N_A, DEC_BATCH, RW_HEADS, RW_HEAD, RW_HEAD), 0.5),
        "state_rwkv_shift": nrm((N_A, DEC_BATCH, D)),
        "state_gla": nrm((N_B, DEC_BATCH, GLA_HEADS, GLA_HK, GLA_HV), 0.5),
        "state_s5_re": nrm((N_C, DEC_BATCH, S5_GROUPS, S5_STATE), 0.1),
        "state_s5_im": nrm((N_C, DEC_BATCH, S5_GROUPS, S5_STATE), 0.1),
        "c_prompt": nrm((BATCH, D)),
        "c_sample": nrm((DEC_BATCH, D)),
        "norm_g": 1.0 + nrm((DEPTH, 3, D), 0.02),
        "ada_w": nrm((DEPTH, D, N_ADA * D), 0.5 * D ** -0.5),
        "ada_b": nrm((DEPTH, N_ADA * D), 0.02),
        "ffn_w_in": nrm((DEPTH, 2, D, 2 * F), D ** -0.5),
        "ffn_w_out": nrm((DEPTH, 2, F, D), F ** -0.5),
        "rw_mu": uni((N_A, 6, D), 0.0, 1.0),
        "rw_w_rkv": nrm((N_A, 3, D, D), D ** -0.5),
        "rw_w0": uni((N_A, D), -4.0, 1.0),
        "rw_w1": nrm((N_A, D, RW_DECAY_LORA), D ** -0.5),
        "rw_w2": nrm((N_A, RW_DECAY_LORA, D), 0.5 * RW_DECAY_LORA ** -0.5),
        "rw_a0": nrm((N_A, D), 0.1),
        "rw_a1": nrm((N_A, D, RW_ICLR_LORA), D ** -0.5),
        "rw_a2": nrm((N_A, RW_ICLR_LORA, D), 0.5 * RW_ICLR_LORA ** -0.5),
        "rw_g1": nrm((N_A, D, RW_GATE_LORA), D ** -0.5),
        "rw_g2": nrm((N_A, RW_GATE_LORA, D), RW_GATE_LORA ** -0.5),
        "rw_k_k": 0.85 + nrm((N_A, D), 0.02),
        "rw_k_a": 1.0 + nrm((N_A, D), 0.02),
        "rw_r_k": nrm((N_A, RW_HEADS, RW_HEAD), 0.1),
        "rw_gn_w": 1.0 + nrm((N_A, D), 0.02),
        "rw_gn_b": nrm((N_A, D), 0.02),
        "rw_w_o": nrm((N_A, D, D), D ** -0.5),
        "gla_w_in": nrm((N_B, D, GLA_SPLITS[-1] + GLA_GATE_LORA), D ** -0.5),
        "gla_a_w2": nrm((N_B, GLA_GATE_LORA, GLA_DK), GLA_GATE_LORA ** -0.5),
        "gla_a_b": nrm((N_B, GLA_DK), 0.1),
        "gla_o_norm": 1.0 + nrm((N_B, GLA_HV), 0.02),
        "gla_w_o": nrm((N_B, GLA_DV, D), GLA_DV ** -0.5),
        "s5_A_re": -0.5 + nrm((N_C, S5_GROUPS, S5_STATE), 0.01),
        "s5_A_im": jnp.pi * jnp.arange(S5_STATE, dtype=jnp.float32) + nrm((N_C, S5_GROUPS, S5_STATE), 0.01),
        "s5_log_dt": uni((N_C, S5_GROUPS), math.log(S5_DT_MIN), math.log(S5_DT_MAX)),
        "s5_B_re": nrm((N_C, S5_GROUPS, S5_STATE, S5_GROUP), (2 * S5_GROUP) ** -0.5),
        "s5_B_im": nrm((N_C, S5_GROUPS, S5_STATE, S5_GROUP), (2 * S5_GROUP) ** -0.5),
        "s5_C_re": nrm((N_C, S5_GROUPS, S5_GROUP, S5_STATE), 0.5),
        "s5_C_im": nrm((N_C, S5_GROUPS, S5_GROUP, S5_STATE), 0.5),
        "s5_D": nrm((N_C, D)),
        "s5_glu_w": nrm((N_C, D, 2 * D), D ** -0.5),
        "s5_glu_b": nrm((N_C, 2 * D), 0.02),
        "final_g": 1.0 + nrm((D,), 0.02),
    }


def reference(x_prompt, x_sample, state_rwkv_wkv, state_rwkv_shift, state_gla, state_s5_re, state_s5_im,
              c_prompt, c_sample, norm_g, ada_w, ada_b, ffn_w_in, ffn_w_out,
              rw_mu, rw_w_rkv, rw_w0, rw_w1, rw_w2, rw_a0, rw_a1, rw_a2, rw_g1, rw_g2, rw_k_k, rw_k_a, rw_r_k,
              rw_gn_w, rw_gn_b, rw_w_o,
              gla_w_in, gla_a_w2, gla_a_b, gla_o_norm, gla_w_o,
              s5_A_re, s5_A_im, s5_log_dt, s5_B_re, s5_B_im, s5_C_re, s5_C_im, s5_D, s5_glu_w, s5_glu_b,
              final_g):
    shared = (norm_g, ada_w, ada_b, ffn_w_in, ffn_w_out)
    rw = (rw_mu, rw_w_rkv, rw_w0, rw_w1, rw_w2, rw_a0, rw_a1, rw_a2, rw_g1, rw_g2, rw_k_k, rw_k_a, rw_r_k,
          rw_gn_w, rw_gn_b, rw_w_o)
    gl = (gla_w_in, gla_a_w2, gla_a_b, gla_o_norm, gla_w_o)
    s5 = (s5_A_re, s5_A_im, s5_log_dt, s5_B_re, s5_B_im, s5_C_re, s5_C_im, s5_D, s5_glu_w, s5_glu_b)
    bp = x_prompt.shape[0]
    dt = x_prompt.dtype
    z_wkv = jnp.zeros((N_A, bp) + state_rwkv_wkv.shape[2:], dt)
    z_shift = jnp.zeros((N_A, bp) + state_rwkv_shift.shape[2:], dt)
    z_gla = jnp.zeros((N_B, bp) + state_gla.shape[2:], dt)
    z_re = jnp.zeros((N_C, bp) + state_s5_re.shape[2:], dt)
    z_im = jnp.zeros((N_C, bp) + state_s5_im.shape[2:], dt)
    y_prompt, p_wkv, p_shift, p_gla, p_re, p_im = trunk(
        x_prompt, c_prompt, z_wkv, z_shift, z_gla, z_re, z_im, shared, rw, gl, s5, final_g)
    y_sample, s_wkv, s_shift, s_gla, s_re, s_im = trunk(
        x_sample, c_sample, state_rwkv_wkv, state_rwkv_shift, state_gla, state_s5_re, state_s5_im,
        shared, rw, gl, s5, final_g)
    return (y_prompt, y_sample, p_wkv, p_shift, p_gla, p_re, p_im, s_wkv, s_shift, s_gla, s_re, s_im)
```

```python
import functools
import math

import jax
import jax.numpy as jnp
from jax import lax
from jax.experimental import pallas as pl
from jax.experimental.pallas import tpu as pltpu

f32 = jnp.float32
bf16 = jnp.bfloat16
HI = lax.Precision.HIGHEST

EPS = 1e-6
N_ADA = 9
RW_HEAD = 64
RW_GN_EPS = 64e-5
RW_LORA_PAD = 128
GLA_HEADS = 4
GLA_GATE_NORM = 16.0
GLA_CHUNK = 64
GLA_LORA_PAD = 128
S5_GROUP = 16
S5_STATE = 64
S5_SLAB_GROUPS = 8
LANE = 128
SUBLANE = 8
VMEM_LIMIT_BYTES = 56 * 1024 * 1024

ROW_TILE = 512
FFN_TILE_F = 512
MM_TILE_N = 512
ADA_TILE_N = 1024
RW_PROJ_ROWS = 256
RW_SCAN_TBLK = 256
GLA_SCAN_TBLK = 256
S5_TBLK = 128
S5_LANE_CHUNK = 1024


def _cparams(*sem):
    return pltpu.CompilerParams(dimension_semantics=sem, vmem_limit_bytes=VMEM_LIMIT_BYTES)


def _dot(a, b):
    return jnp.dot(a, b, preferred_element_type=f32)


def _dot_hi(a, b):
    return jnp.dot(a, b, precision=HI, preferred_element_type=f32)


def _dot_nt_hi(a, b):
    return lax.dot_general(a, b, (((1,), (1,)), ((), ())), precision=HI, preferred_element_type=f32)


def _dot_tn_hi(a, b):
    return lax.dot_general(a, b, (((0,), (0,)), ((), ())), precision=HI, preferred_element_type=f32)


def _normmod(x, g, shift, scale):
    ms = jnp.mean(x * x, axis=-1, keepdims=True)
    y = x * lax.rsqrt(ms + EPS) * g
    return y * (1.0 + scale) + shift


def _ada_kernel(c_ref, w_ref, b_ref, o_ref):
    c = c_ref[...]
    s = (c * jax.nn.sigmoid(c)).astype(bf16)
    o_ref[0] = _dot(s, w_ref[0].astype(bf16)) + b_ref[0]


def _ada_call(c_all, ada_w, ada_b):
    depth, d, n = ada_w.shape
    m = c_all.shape[0]
    tn = min(ADA_TILE_N, n)
    return pl.pallas_call(
        _ada_kernel,
        grid=(depth, n // tn),
        in_specs=[pl.BlockSpec((m, d), lambda l, j: (0, 0)),
                  pl.BlockSpec((1, d, tn), lambda l, j: (l, 0, j)),
                  pl.BlockSpec((1, 1, tn), lambda l, j: (l, 0, j))],
        out_specs=pl.BlockSpec((1, m, tn), lambda l, j: (l, 0, j)),
        out_shape=jax.ShapeDtypeStruct((depth, m, n), f32),
        compiler_params=_cparams("arbitrary", "arbitrary"),
        name="ada",
    )(c_all, ada_w, ada_b.reshape(depth, 1, n))


def _ffn_kernel(x_ref, g_ref, sh_ref, sc_ref, gt_ref, wg_ref, wu_ref, wo_ref, o_ref, h_scr, acc_scr):
    j = pl.program_id(1)
    nb, tt, d = x_ref.shape

    @pl.when(j == 0)
    def _():
        h = _normmod(x_ref[...], g_ref[...], sh_ref[...], sc_ref[...])
        h_scr[...] = h.reshape(nb * tt, d).astype(bf16)
        acc_scr[...] = jnp.zeros_like(acc_scr)

    hb = h_scr[...]
    gate = _dot(hb, wg_ref[...])
    up = _dot(hb, wu_ref[...])
    act = (gate * jax.nn.sigmoid(gate) * up).astype(bf16)
    acc_scr[...] += _dot(act, wo_ref[...])

    @pl.when(j == pl.num_programs(1) - 1)
    def _():
        o_ref[...] = x_ref[...] + 0.5 * gt_ref[...] * acc_scr[...].reshape(nb, tt, d)


def _ffn_call(x, g, shift, scale, gate, w_in, w_out, l, s, nb, tt):
    nseq, t, d = x.shape
    f = w_out.shape[2]
    tf = min(FFN_TILE_F, f)
    nf = f // tf
    rows = pl.BlockSpec((nb, tt, d), lambda i, j: (i, 0, 0)) if nb > 1 else pl.BlockSpec((1, tt, d), lambda i, j: (i // (t // tt), i % (t // tt), 0))
    mod = _mod_spec(nb, t, tt, d)
    grid_rows = nseq // nb if nb > 1 else nseq * (t // tt)
    return pl.pallas_call(
        _ffn_kernel,
        grid=(grid_rows, nf),
        in_specs=[rows, pl.BlockSpec((1, 1, d), lambda i, j: (0, 0, 0)), mod, mod, mod,
                  pl.BlockSpec((None, None, d, tf), lambda i, j: (l, s, 0, j)),
                  pl.BlockSpec((None, None, d, tf), lambda i, j: (l, s, 0, nf + j)),
                  pl.BlockSpec((None, None, tf, d), lambda i, j: (l, s, j, 0))],
        out_specs=rows,
        out_shape=jax.ShapeDtypeStruct(x.shape, f32),
        scratch_shapes=[pltpu.VMEM((nb * tt, d), bf16), pltpu.VMEM((nb * tt, d), f32)],
        compiler_params=_cparams("parallel", "arbitrary"),
        name="ffn",
    )(x, g, shift, scale, gate, w_in, w_in, w_out)


def _mod_spec(nb, t, tt, width):
    if nb > 1:
        return pl.BlockSpec((nb, 1, width), lambda i, j: (i, 0, 0))
    return pl.BlockSpec((1, 1, width), lambda i, j: (i // (t // tt), 0, 0))


def _mm_kernel(*refs, n_row, n_mod, n_vec, n_w, n_erow, n_emod, n_evec, n_out, prologue, epilogue):
    pos = 0

    def take(n):
        nonlocal pos
        out = refs[pos:pos + n]
        pos += n
        return out

    rows, mods, vecs, ws = take(n_row), take(n_mod), take(n_vec), take(n_w)
    erows, emods, evecs, outs = take(n_erow), take(n_emod), take(n_evec), take(n_out)
    a_scr = refs[pos]
    nb, tt, k = rows[0].shape
    tn = ws[0].shape[-1]

    @pl.when(pl.program_id(1) == 0)
    def _():
        a = prologue([r[...] for r in rows], [m[...] for m in mods], [v[...] for v in vecs])
        a_scr[...] = a.reshape(nb * tt, k).astype(bf16)

    ab = a_scr[...]
    accs = [_dot(ab, w[...]).reshape(nb, tt, tn) for w in ws]
    res = epilogue(accs, [r[...] for r in erows], [m[...] for m in emods], [v[...] for v in evecs])
    for o_ref, o in zip(outs, res):
        o_ref[...] = o


def _mm_call(name, rows, mods, vecs, ws, erows, emods, evecs, n_total, n_out, prologue, epilogue, nb, tt, tn=MM_TILE_N):
    nseq, t, k = rows[0].shape
    tn = min(tn, n_total)
    nj = n_total // tn
    if nb > 1:
        grid_rows = nseq // nb
        rmap = lambda i, j: (i, 0, 0)
        ermap = lambda i, j: (i, 0, j)
        mmap = lambda i, j: (i, 0, 0)
        emmap = lambda i, j: (i, 0, j)
    else:
        per = t // tt
        grid_rows = nseq * per
        rmap = lambda i, j: (i // per, i % per, 0)
        ermap = lambda i, j: (i // per, i % per, j)
        mmap = lambda i, j: (i // per, 0, 0)
        emmap = lambda i, j: (i // per, 0, j)
    in_specs = ([pl.BlockSpec((nb, tt, k), rmap)] * len(rows)
                + [pl.BlockSpec((nb, 1, k), mmap)] * len(mods)
                + [pl.BlockSpec((1, 1, k), lambda i, j: (0, 0, 0))] * len(vecs)
                + [pl.BlockSpec((k, tn), functools.partial(lambda i, j, off: (0, off + j), off=off)) for _, off in ws]
                + [pl.BlockSpec((nb, tt, tn), ermap)] * len(erows)
                + [pl.BlockSpec((nb, 1, tn), emmap)] * len(emods)
                + [pl.BlockSpec((1, 1, tn), lambda i, j: (0, 0, j))] * len(evecs))
    kern = functools.partial(_mm_kernel, n_row=len(rows), n_mod=len(mods), n_vec=len(vecs), n_w=len(ws),
                             n_erow=len(erows), n_emod=len(emods), n_evec=len(evecs), n_out=n_out,
                             prologue=prologue, epilogue=epilogue)
    out = pl.pallas_call(
        kern,
        grid=(grid_rows, nj),
        in_specs=in_specs,
        out_specs=[pl.BlockSpec((nb, tt, tn), ermap)] * n_out,
        out_shape=[jax.ShapeDtypeStruct((nseq, t, n_total), f32)] * n_out,
        scratch_shapes=[pltpu.VMEM((nb * tt, k), bf16)],
        compiler_params=_cparams("parallel", "arbitrary"),
        name=name,
    )(*rows, *mods, *vecs, *[w for w, _ in ws], *erows, *emods, *evecs)
    return out


def _pro_normmod(rows, mods, vecs):
    return _normmod(rows[0], vecs[0], mods[0], mods[1])


def _pro_mul(rows, mods, vecs):
    return rows[0] * rows[1]


def _pro_id(rows, mods, vecs):
    return rows[0]


def _epi_id(accs, erows, emods, evecs):
    return accs


def _epi_residual(accs, erows, emods, evecs):
    return [erows[0] + emods[0] * accs[0]]


def _normmod_kernel(x_ref, g_ref, sh_ref, sc_ref, o_ref):
    o_ref[...] = _normmod(x_ref[...], g_ref[...], sh_ref[...], sc_ref[...])


def _rms_kernel(x_ref, g_ref, o_ref):
    x = x_ref[...]
    o_ref[...] = x * lax.rsqrt(jnp.mean(x * x, axis=-1, keepdims=True) + EPS) * g_ref[...]


def _rowwise_call(kern, name, x, vec, mods, nb, tt):
    nseq, t, d = x.shape
    if nb > 1:
        grid_rows = nseq // nb
        rmap = lambda i: (i, 0, 0)
        mmap = lambda i: (i, 0, 0)
    else:
        per = t // tt
        grid_rows = nseq * per
        rmap = lambda i: (i // per, i % per, 0)
        mmap = lambda i: (i // per, 0, 0)
    return pl.pallas_call(
        kern,
        grid=(grid_rows,),
        in_specs=[pl.BlockSpec((nb, tt, d), rmap), pl.BlockSpec((1, 1, d), lambda i: (0, 0, 0))]
                 + [pl.BlockSpec((nb, 1, d), mmap)] * len(mods),
        out_specs=pl.BlockSpec((nb, tt, d), rmap),
        out_shape=jax.ShapeDtypeStruct(x.shape, f32),
        compiler_params=_cparams("parallel"),
        name=name,
    )(x, vec, *mods)


def _rwproj_kernel(h_ref, hp_ref, mu_ref, w1_ref, a1_ref, g1_ref, wr_ref, wk_ref, wv_ref, w2_ref, a2_ref, g2_ref,
                   w0_ref, a0_ref, r_ref, k_ref, v_ref, lw_ref, a_ref, g_ref, xm_scr, tw_scr, ta_scr, tg_scr):
    nb, tt, d = h_ref.shape
    rows = nb * tt
    tn = wr_ref.shape[-1]

    @pl.when(pl.program_id(1) == 0)
    def _():
        h = h_ref[...].reshape(rows, d)
        dlt = hp_ref[...].reshape(rows, d) - h
        mu = mu_ref[...]
        for p in range(3):
            xm_scr[p] = (h + dlt * mu[p:p + 1]).astype(bf16)
        xw = (h + dlt * mu[3:4]).astype(bf16)
        tw_scr[...] = jnp.tanh(_dot(xw, w1_ref[...])).astype(bf16)
        xa = (h + dlt * mu[4:5]).astype(bf16)
        ta_scr[...] = _dot(xa, a1_ref[...]).astype(bf16)
        xg = (h + dlt * mu[5:6]).astype(bf16)
        tg_scr[...] = jax.nn.sigmoid(_dot(xg, g1_ref[...])).astype(bf16)

    shp = (nb, tt, tn)
    r_ref[...] = _dot(xm_scr[0], wr_ref[...]).reshape(shp)
    k_ref[...] = _dot(xm_scr[1], wk_ref[...]).reshape(shp)
    v_ref[...] = _dot(xm_scr[2], wv_ref[...]).reshape(shp)
    w_log = -jax.nn.softplus(-(w0_ref[0] + _dot(tw_scr[...], w2_ref[...]))) - 0.5
    lw_ref[...] = (-jnp.exp(w_log)).reshape(shp)
    a_ref[...] = jax.nn.sigmoid(a0_ref[0] + _dot(ta_scr[...], a2_ref[...])).reshape(shp)
    g_ref[...] = _dot(tg_scr[...], g2_ref[...]).reshape(shp)


def _rwproj_call(h, h_prev, mu, w1, a1, g1, wr, wk, wv, w2, a2, g2, w0, a0, nb, tt):
    nseq, t, d = h.shape
    tn = min(MM_TILE_N, d)
    if nb > 1:
        grid_rows = nseq // nb
        rmap = lambda i, j: (i, 0, 0)
        omap = lambda i, j: (i, 0, j)
    else:
        per = t // tt
        grid_rows = nseq * per
        rmap = lambda i, j: (i // per, i % per, 0)
        omap = lambda i, j: (i // per, i % per, j)
    full = lambda a: pl.BlockSpec(a.shape, lambda i, j: (0,) * a.ndim)
    col = lambda a: pl.BlockSpec((a.shape[0], tn), lambda i, j: (0, j))
    rows = nb * tt
    return pl.pallas_call(
        _rwproj_kernel,
        grid=(grid_rows, d // tn),
        in_specs=[pl.BlockSpec((nb, tt, d), rmap), pl.BlockSpec((nb, tt, d), rmap), full(mu), full(w1), full(a1), full(g1),
                  col(wr), col(wk), col(wv), col(w2), col(a2), col(g2),
                  pl.BlockSpec((1, 1, tn), lambda i, j: (0, 0, j)), pl.BlockSpec((1, 1, tn), lambda i, j: (0, 0, j))],
        out_specs=[pl.BlockSpec((nb, tt, tn), omap)] * 6,
        out_shape=[jax.ShapeDtypeStruct((nseq, t, d), f32)] * 6,
        scratch_shapes=[pltpu.VMEM((3, rows, d), bf16), pltpu.VMEM((rows, w1.shape[1]), bf16),
                        pltpu.VMEM((rows, a1.shape[1]), bf16), pltpu.VMEM((rows, g1.shape[1]), bf16)],
        compiler_params=_cparams("parallel", "arbitrary"),
        name="rwkv_proj",
    )(h, h_prev, mu, w1, a1, g1, wr, wk, wv, w2, a2, g2, w0, a0)


def _rwscan_kernel(r_ref, k_ref, v_ref, lw_ref, a_ref, kk_ref, ka_ref, rk_ref, gw_ref, gb_ref, s0_ref,
                   y_ref, sout_ref, s_scr, *, nb, tt, n_inner):
    c = pl.program_id(2)
    n = RW_HEAD
    big = nb * tt
    n_dbl = max(1, math.ceil(math.log2(tt)))

    @pl.when(c == 0)
    def _():
        s_scr[...] = s0_ref[...]

    ti = lax.broadcasted_iota(jnp.int32, (big, big), 0)
    si = lax.broadcasted_iota(jnp.int32, (big, big), 1)
    same = (ti // tt) == (si // tt)
    incl = jnp.logical_and(same, si <= ti)
    strict = jnp.logical_and(same, si < ti)
    same_f = same.astype(f32)
    incl_f = incl.astype(f32)
    eye_f = (ti == si).astype(f32)

    kkp, kap, rkp, gwp, gbp = kk_ref[0], ka_ref[0], rk_ref[0], gw_ref[0], gb_ref[0]

    def load(ref, ci):
        if nb > 1:
            return ref[...].reshape(big, 2 * n)
        return ref[0, pl.ds(pl.multiple_of(ci * big, big), big), :]

    def chunk(ci, carry):
        r, k, v, lw, a = (load(ref, ci) for ref in (r_ref, k_ref, v_ref, lw_ref, a_ref))
        cum = _dot_hi(incl_f, lw)
        tot = _dot_hi(same_f, lw)
        g_in = jnp.exp(cum)
        g_prev = jnp.exp(cum - lw)
        g_inv = jnp.exp(-cum)
        g_rest = jnp.exp(tot - cum)
        g_tot = jnp.exp(tot)
        kk = k * kkp
        k2 = k * (1.0 + (a - 1.0) * kap)
        ys = []
        for hh in range(2):
            sl = slice(hh * n, (hh + 1) * n)
            kk_h = kk[:, sl]
            kk_h = kk_h / jnp.maximum(jnp.sqrt(jnp.sum(kk_h * kk_h, axis=1, keepdims=True)), 1e-12)
            b_h = kk_h * a[:, sl]
            r_h, k_h, v_h = r[:, sl], k2[:, sl], v[:, sl]
            rt = r_h * g_in[:, sl]
            at = -kk_h * g_prev[:, sl]
            bt = b_h * g_inv[:, sl]
            kt = k_h * g_inv[:, sl]
            bhat = b_h * g_rest[:, sl]
            khat = k_h * g_rest[:, sl]
            ra = jnp.concatenate([rt, at], axis=0)
            amat = _dot_nt_hi(ra, jnp.concatenate([bt, kt], axis=0))
            a_rb = jnp.where(incl, amat[:big, :big], 0.0)
            a_rk = jnp.where(incl, amat[:big, big:], 0.0)
            a_ab = jnp.where(strict, amat[big:, :big], 0.0)
            a_ak = jnp.where(strict, amat[big:, big:], 0.0)
            r_s, a_s = [], []
            for i in range(nb):
                rows = slice(i * tt, (i + 1) * tt)
                x1 = _dot_nt_hi(jnp.concatenate([rt[rows], at[rows]], axis=0), s_scr[i, hh])
                r_s.append(x1[:tt])
                a_s.append(x1[tt:])
            r_s = r_s[0] if nb == 1 else jnp.concatenate(r_s, axis=0)
            a_s = a_s[0] if nb == 1 else jnp.concatenate(a_s, axis=0)
            tm = eye_f + a_ab
            pw = a_ab
            for _ in range(n_dbl - 1):
                pw = _dot_hi(pw, pw)
                tm = tm + _dot_hi(tm, pw)
            u = _dot_hi(tm, a_s + _dot_hi(a_ak, v_h))
            y = r_s + _dot_hi(a_rb, u) + _dot_hi(a_rk, v_h)
            for i in range(nb):
                rows = slice(i * tt, (i + 1) * tt)
                uv = jnp.concatenate([u[rows], v_h[rows]], axis=0)
                bk = jnp.concatenate([bhat[rows], khat[rows]], axis=0)
                s_scr[i, hh] = s_scr[i, hh] * g_tot[i * tt:i * tt + 1, sl] + _dot_tn_hi(uv, bk)
            mean = jnp.mean(y, axis=1, keepdims=True)
            var = jnp.mean(jnp.square(y - mean), axis=1, keepdims=True)
            yn = (y - mean) * lax.rsqrt(var + RW_GN_EPS) * gwp[:, sl] + gbp[:, sl]
            bonus = jnp.sum(r_h * k_h * rkp[:, sl], axis=1, keepdims=True) * v_h
            ys.append(yn + bonus)
        yo = jnp.concatenate(ys, axis=1)
        if nb > 1:
            y_ref[...] = yo.reshape(nb, tt, 2 * n)
        else:
            y_ref[0, pl.ds(pl.multiple_of(ci * big, big), big), :] = yo
        return carry

    lax.fori_loop(0, n_inner, chunk, 0)

    @pl.when(c == pl.num_programs(2) - 1)
    def _():
        sout_ref[...] = s_scr[...]


def _rwscan_call(r, k, v, lw, a, k_k, k_a, r_k, gn_w, gn_b, s0, nb, tt, tblk):
    nseq, t, d = r.shape
    nh = d // RW_HEAD
    lanes = 2 * RW_HEAD
    nc = t // tblk
    xspec = pl.BlockSpec((nb, tblk, lanes), lambda i, p, c: (i, c, p))
    pspec = pl.BlockSpec((1, 1, lanes), lambda i, p, c: (0, 0, p))
    sspec = pl.BlockSpec((nb, 2, RW_HEAD, RW_HEAD), lambda i, p, c: (i, p, 0, 0))
    kern = functools.partial(_rwscan_kernel, nb=nb, tt=tt, n_inner=tblk // tt if nb == 1 else 1)
    return pl.pallas_call(
        kern,
        grid=(nseq // nb, nh // 2, nc),
        in_specs=[xspec] * 5 + [pspec] * 5 + [sspec],
        out_specs=[xspec, sspec],
        out_shape=[jax.ShapeDtypeStruct((nseq, t, d), f32), jax.ShapeDtypeStruct(s0.shape, f32)],
        scratch_shapes=[pltpu.VMEM((nb, 2, RW_HEAD, RW_HEAD), f32)],
        compiler_params=_cparams("parallel", "parallel", "arbitrary"),
        name="rwkv_scan",
    )(r, k, v, lw, a, k_k, k_a, r_k, gn_w, gn_b, s0)


def _glascan_kernel(q_ref, k_ref, v_ref, r_ref, la_ref, on_ref, s0_ref, o_ref, sout_ref, s_scr, *, chunk, n_inner):
    c = pl.program_id(2)
    hk = q_ref.shape[-1]
    hv = v_ref.shape[-1]

    @pl.when(c == 0)
    def _():
        s_scr[...] = s0_ref[0, 0]

    ti = lax.broadcasted_iota(jnp.int32, (chunk, chunk), 0)
    si = lax.broadcasted_iota(jnp.int32, (chunk, chunk), 1)
    causal = si <= ti
    causal_f = causal.astype(f32)
    ones_cv = jnp.ones((chunk, hv), f32)

    def body(ci, carry):
        rows = pl.ds(pl.multiple_of(ci * chunk, chunk), chunk)
        q, k, v, la = q_ref[0, rows, :], k_ref[0, rows, :], v_ref[0, rows, :], la_ref[0, rows, :]
        b = _dot_hi(causal_f, la)
        b_last = b[chunk - 1:chunk, :]
        q_in = (q * (hk ** -0.5) * jnp.exp(b)).astype(bf16)
        k_in = (k * jnp.exp(-b)).astype(bf16)
        att = lax.dot_general(q_in, k_in, (((1,), (1,)), ((), ())), preferred_element_type=f32)
        att = jnp.where(causal, att, 0.0).astype(bf16)
        vb = v.astype(bf16)
        s = s_scr[...]
        o = _dot(att, vb) + _dot(q_in, s.astype(bf16))
        kd = (k * jnp.exp(b_last - b)).astype(bf16)
        dec = jnp.exp(_dot_tn_hi(la, ones_cv))
        s_scr[...] = s * dec + lax.dot_general(kd, vb, (((0,), (0,)), ((), ())), preferred_element_type=f32)
        o = o * lax.rsqrt(jnp.mean(o * o, axis=-1, keepdims=True) + EPS) * on_ref[...]
        rr = r_ref[0, rows, :]
        o_ref[0, rows, :] = o * (rr * jax.nn.sigmoid(rr))
        return carry

    lax.fori_loop(0, n_inner, body, 0)

    @pl.when(c == pl.num_programs(2) - 1)
    def _():
        sout_ref[0, 0] = s_scr[...]


def _glascan_call(qkvr, la, o_norm, s0, tblk):
    nseq, t, _ = qkvr.shape
    _, gh, hk, hv = s0.shape
    chunk = math.gcd(t, GLA_CHUNK)
    tblk = min(tblk, t)
    k_off = (gh * hk) // hk
    v_off = (2 * gh * hk) // hv
    r_off = (2 * gh * hk + gh * hv) // hv
    kern = functools.partial(_glascan_kernel, chunk=chunk, n_inner=tblk // chunk)
    return pl.pallas_call(
        kern,
        grid=(nseq, gh, t // tblk),
        in_specs=[pl.BlockSpec((1, tblk, hk), lambda b, h, c: (b, c, h)),
                  pl.BlockSpec((1, tblk, hk), lambda b, h, c: (b, c, k_off + h)),
                  pl.BlockSpec((1, tblk, hv), lambda b, h, c: (b, c, v_off + h)),
                  pl.BlockSpec((1, tblk, hv), lambda b, h, c: (b, c, r_off + h)),
                  pl.BlockSpec((1, tblk, hk), lambda b, h, c: (b, c, h)),
                  pl.BlockSpec((1, hv), lambda b, h, c: (0, 0)),
                  pl.BlockSpec((1, 1, hk, hv), lambda b, h, c: (b, h, 0, 0))],
        out_specs=[pl.BlockSpec((1, tblk, hv), lambda b, h, c: (b, c, h)),
                   pl.BlockSpec((1, 1, hk, hv), lambda b, h, c: (b, h, 0, 0))],
        out_shape=[jax.ShapeDtypeStruct((nseq, t, gh * hv), f32), jax.ShapeDtypeStruct(s0.shape, f32)],
        scratch_shapes=[pltpu.VMEM((hk, hv), f32)],
        compiler_params=_cparams("parallel", "parallel", "arbitrary"),
        name="gla_scan",
    )(qkvr, qkvr, qkvr, qkvr, la, o_norm, s0)


def _s5prep_kernel(are_ref, aim_ref, ldt_ref, bre_ref, bim_ref, abre_ref, abim_ref, bbre_ref, bbim_ref):
    a_re, a_im = are_ref[...], aim_ref[...]
    dt = jnp.exp(ldt_ref[...])
    mag = jnp.exp(a_re * dt)
    ab_re, ab_im = mag * jnp.cos(a_im * dt), mag * jnp.sin(a_im * dt)
    den = a_re * a_re + a_im * a_im
    nr = ab_re - 1.0
    cf_re = (nr * a_re + ab_im * a_im) / den
    cf_im = (ab_im * a_re - nr * a_im) / den
    abre_ref[...] = ab_re
    abim_ref[...] = ab_im
    b_re, b_im = bre_ref[...], bim_ref[...]
    bbre_ref[...] = cf_re * b_re - cf_im * b_im
    bbim_ref[...] = cf_re * b_im + cf_im * b_re


def _s5prep_call(a_re, a_im, log_dt, b_re_t, b_im_t):
    g, n = a_re.shape
    return pl.pallas_call(
        _s5prep_kernel,
        out_shape=[jax.ShapeDtypeStruct((g, 1, n), f32)] * 2 + [jax.ShapeDtypeStruct(b_re_t.shape, f32)] * 2,
        name="s5_prep",
    )(a_re.reshape(g, 1, n), a_im.reshape(g, 1, n), log_dt.reshape(g, 1, 1), b_re_t, b_im_t)


def _s5scan_kernel(x_ref, g_ref, sh_ref, sc_ref, wbu_ref, wc_ref, abre_ref, abim_ref, dsk_ref, x0re_ref, x0im_ref,
                   y_ref, xre_ref, xim_ref, bre_scr, bim_scr, sre_scr, sim_scr, *, lane_chunk):
    c = pl.program_id(1)
    _, tblk, d = x_ref.shape
    n_slab = wbu_ref.shape[0]
    half = wbu_ref.shape[2] // 2
    n_state = n_slab * half

    @pl.when(c == 0)
    def _():
        sre_scr[...] = x0re_ref[0]
        sim_scr[...] = x0im_ref[0]

    h = _normmod(x_ref[...], g_ref[...], sh_ref[...], sc_ref[...])[0]
    hb = h.astype(bf16)
    for s in range(n_slab):
        bu = _dot(hb[:, s * LANE:(s + 1) * LANE], wbu_ref[s])
        bre_scr[:, s * half:(s + 1) * half] = bu[:, :half]
        bim_scr[:, s * half:(s + 1) * half] = bu[:, half:]

    for q in range(n_state // lane_chunk):
        ls = slice(q * lane_chunk, (q + 1) * lane_chunk)
        ar, ai = abre_ref[:, ls], abim_ref[:, ls]

        def step(t, carry):
            xr, xi = carry
            row = pl.ds(t, 1)
            nxr = ar * xr - ai * xi + bre_scr[row, ls]
            nxi = ar * xi + ai * xr + bim_scr[row, ls]
            bre_scr[row, ls] = nxr
            bim_scr[row, ls] = nxi
            return nxr, nxi

        xr, xi = lax.fori_loop(0, tblk, step, (sre_scr[:, ls], sim_scr[:, ls]))
        sre_scr[:, ls] = xr
        sim_scr[:, ls] = xi

    for s in range(n_slab):
        xs = jnp.concatenate([bre_scr[:, s * half:(s + 1) * half], bim_scr[:, s * half:(s + 1) * half]], axis=1)
        ys = _dot(xs.astype(bf16), wc_ref[s])
        sl = slice(s * LANE, (s + 1) * LANE)
        y_ref[0, :, sl] = jax.nn.gelu(ys + dsk_ref[:, sl] * h[:, sl])

    @pl.when(c == pl.num_programs(1) - 1)
    def _():
        xre_ref[0] = sre_scr[...]
        xim_ref[0] = sim_scr[...]


def _s5scan_call(x, g, shift, scale, wbu, wc, ab_re, ab_im, d_skip, x0_re, x0_im, tblk):
    nseq, t, d = x.shape
    tblk = min(tblk, t)
    n_state = ab_re.shape[1]
    full = lambda a: pl.BlockSpec(a.shape, lambda b, c: (0,) * a.ndim)
    mod = pl.BlockSpec((1, 1, d), lambda b, c: (b, 0, 0))
    st = pl.BlockSpec((1, 1, n_state), lambda b, c: (b, 0, 0))
    kern = functools.partial(_s5scan_kernel, lane_chunk=min(S5_LANE_CHUNK, n_state))
    return pl.pallas_call(
        kern,
        grid=(nseq, t // tblk),
        in_specs=[pl.BlockSpec((1, tblk, d), lambda b, c: (b, c, 0)), pl.BlockSpec((1, 1, d), lambda b, c: (0, 0, 0)),
                  mod, mod, full(wbu), full(wc), full(ab_re), full(ab_im), full(d_skip), st, st],
        out_specs=[pl.BlockSpec((1, tblk, d), lambda b, c: (b, c, 0)), st, st],
        out_shape=[jax.ShapeDtypeStruct((nseq, t, d), f32), jax.ShapeDtypeStruct((nseq, 1, n_state), f32),
                   jax.ShapeDtypeStruct((nseq, 1, n_state), f32)],
        scratch_shapes=[pltpu.VMEM((tblk, n_state), f32), pltpu.VMEM((tblk, n_state), f32),
                        pltpu.VMEM((1, n_state), f32), pltpu.VMEM((1, n_state), f32)],
        compiler_params=_cparams("parallel", "arbitrary"),
        name="s5_scan",
    )(x, g, shift, scale, wbu, wc, ab_re, ab_im, d_skip, x0_re, x0_im)


def _epi_glu(accs, erows, emods, evecs):
    val = accs[0] + evecs[0]
    gate = accs[1] + evecs[1]
    return [erows[0] + emods[0] * (val * jax.nn.sigmoid(gate))]


def _rwkv_mixer(x, g, shift, scale, gm, shift_prev, wkv0, p, nb, tt, scan_cfg):
    nseq, t, d = x.shape
    h = _rowwise_call(_normmod_kernel, "normmod", x, g, [shift, scale], nb, tt)
    h_prev = jnp.concatenate([shift_prev[:, None, :], h[:, :-1]], axis=1)
    pnb, ptt = (RW_PROJ_ROWS // t, t) if nb > 1 else (1, min(RW_PROJ_ROWS, t))
    r, k, v, lw, a, gg = _rwproj_call(h, h_prev, p["mu"], p["w1"], p["a1"], p["g1"], p["wr"], p["wk"], p["wv"],
                                       p["w2"], p["a2"], p["g2"], p["w0"], p["a0"], pnb, ptt)
    y, wkv = _rwscan_call(r, k, v, lw, a, p["k_k"], p["k_a"], p["r_k"], p["gn_w"], p["gn_b"], wkv0, *scan_cfg)
    (x_new,) = _mm_call("rwkv_out", [y, gg], [], [], [(p["w_o"], 0)], [x], [gm], [], d, 1, _pro_mul, _epi_residual, nb, tt)
    return x_new, h[:, -1], wkv


def _gla_mixer(x, g, shift, scale, gm, s0, p, nb, tt):
    nseq, t, d = x.shape
    n_main = p["w_main"].shape[1]
    (qkvr,) = _mm_call("gla_in", [x], [shift, scale], [g], [(p["w_main"], 0)], [], [], [], n_main, 1,
                       _pro_normmod, _epi_id, nb, tt)
    (a_low,) = _mm_call("gla_gate_in", [x], [shift, scale], [g], [(p["w_gate"], 0)], [], [], [], GLA_LORA_PAD, 1,
                        _pro_normmod, _epi_id, nb, tt)

    def epi_la(accs, erows, emods, evecs):
        return [jax.nn.log_sigmoid(accs[0] + evecs[0]) / GLA_GATE_NORM]

    dk = p["a_w2"].shape[1]
    (la,) = _mm_call("gla_gate", [a_low], [], [], [(p["a_w2"], 0)], [], [], [p["a_b"]], dk, 1, _pro_id, epi_la, nb, tt)
    o, s_new = _glascan_call(qkvr, la, p["o_norm"], s0, GLA_SCAN_TBLK)
    (x_new,) = _mm_call("gla_out", [o], [], [], [(p["w_o"], 0)], [x], [gm], [], d, 1, _pro_id, _epi_residual, nb, tt)
    return x_new, s_new


def _s5_mixer(x, g, shift, scale, gm, x0_re, x0_im, p, nb, tt):
    nseq, t, d = x.shape
    n_state = p["ab_re"].shape[1]
    yg, xre, xim = _s5scan_call(x, g, shift, scale, p["wbu"], p["wc"], p["ab_re"], p["ab_im"], p["d_skip"],
                                x0_re.reshape(nseq, 1, n_state), x0_im.reshape(nseq, 1, n_state), S5_TBLK)
    (x_new,) = _mm_call("s5_glu", [yg], [], [], [(p["glu_w"], 0), (p["glu_w"], d // min(MM_TILE_N, d))], [x], [gm],
                        [p["glu_b_val"], p["glu_b_gate"]], d, 1, _pro_id, _epi_glu, nb, tt)
    return x_new, xre.reshape(x0_re.shape), xim.reshape(x0_im.shape)


def _s5_params(a_re, a_im, log_dt, b_re, b_im, c_re, c_im, d_skip, glu_w, glu_b):
    g, n, cg = b_re.shape
    d = g * cg
    ab_re, ab_im, bb_re, bb_im = _s5prep_call(a_re, a_im, log_dt, b_re.transpose(0, 2, 1), b_im.transpose(0, 2, 1))
    sg = S5_SLAB_GROUPS
    eye = jnp.eye(sg, dtype=f32)

    def bdiag(m):
        gq, rr, cc = m.shape
        m = m.reshape(gq // sg, sg, rr, cc)
        return jnp.einsum("sjrc,jk->sjrkc", m, eye).reshape(gq // sg, sg * rr, sg * cc)

    wbu = jnp.concatenate([bdiag(bb_re), bdiag(bb_im)], axis=2).astype(bf16)
    wc = jnp.concatenate([bdiag(c_re.transpose(0, 2, 1)), bdiag(-c_im.transpose(0, 2, 1))], axis=1).astype(bf16)
    return dict(wbu=wbu, wc=wc, ab_re=ab_re.reshape(1, g * n), ab_im=ab_im.reshape(1, g * n),
                d_skip=d_skip.reshape(1, d), glu_w=glu_w.astype(bf16),
                glu_b_val=glu_b[:d].reshape(1, 1, d), glu_b_gate=glu_b[d:].reshape(1, 1, d))


def _trunk(x, ada, states, weights, nb, tt, rw_scan_cfg):
    nseq, t, d = x.shape
    depth = ada.shape[0]
    wkv, shift, gla, s5_re, s5_im = states
    new = dict(wkv=[], shift=[], gla=[], re=[], im=[])
    ia = ib = ic = 0
    vec = lambda l, j: ada[l, :, j, :].reshape(nseq, 1, d)
    for l in range(depth):
        ng = lambda s: weights["norm_g"][l, s].reshape(1, 1, d)
        x = _ffn_call(x, ng(0), vec(l, 0), vec(l, 1), vec(l, 2), weights["ffn_w_in"], weights["ffn_w_out"], l, 0, nb, tt)
        kind = l % 3
        if kind == 0:
            x, sh, st = _rwkv_mixer(x, ng(1), vec(l, 3), vec(l, 4), vec(l, 5), shift[ia], wkv[ia], weights["rw"][ia],
                                    nb, tt, rw_scan_cfg)
            new["shift"].append(sh)
            new["wkv"].append(st)
            ia += 1
        elif kind == 1:
            x, st = _gla_mixer(x, ng(1), vec(l, 3), vec(l, 4), vec(l, 5), gla[ib], weights["gla"][ib], nb, tt)
            new["gla"].append(st)
            ib += 1
        else:
            x, sr, si = _s5_mixer(x, ng(1), vec(l, 3), vec(l, 4), vec(l, 5), s5_re[ic], s5_im[ic], weights["s5"][ic], nb, tt)
            new["re"].append(sr)
            new["im"].append(si)
            ic += 1
        x = _ffn_call(x, ng(2), vec(l, 6), vec(l, 7), vec(l, 8), weights["ffn_w_in"], weights["ffn_w_out"], l, 1, nb, tt)
    y = _rowwise_call(_rms_kernel, "final_norm", x, weights["final_g"].reshape(1, 1, d), [], nb, tt)
    return (y, jnp.stack(new["wkv"]), jnp.stack(new["shift"]), jnp.stack(new["gla"]),
            jnp.stack(new["re"]), jnp.stack(new["im"]))


def _pad_to(a, axis, size):
    pad = [(0, 0)] * a.ndim
    pad[axis] = (0, size - a.shape[axis])
    return jnp.pad(a, pad)


def kernel(x_prompt, x_sample, state_rwkv_wkv, state_rwkv_shift, state_gla, state_s5_re, state_s5_im, c_prompt, c_sample, norm_g, ada_w, ada_b, ffn_w_in, ffn_w_out, rw_mu, rw_w_rkv, rw_w0, rw_w1, rw_w2, rw_a0, rw_a1, rw_a2, rw_g1, rw_g2, rw_k_k, rw_k_a, rw_r_k, rw_gn_w, rw_gn_b, rw_w_o, gla_w_in, gla_a_w2, gla_a_b, gla_o_norm, gla_w_o, s5_A_re, s5_A_im, s5_log_dt, s5_B_re, s5_B_im, s5_C_re, s5_C_im, s5_D, s5_glu_w, s5_glu_b, final_g):
    bp, t_p, d = x_prompt.shape
    bs, t_s, _ = x_sample.shape
    depth = ada_w.shape[0]
    n_a, n_b, n_c = rw_mu.shape[0], gla_w_in.shape[0], s5_A_re.shape[0]

    n_c_rows = bp + bs
    c_all = _pad_to(jnp.concatenate([c_prompt, c_sample], axis=0), 0, -(-n_c_rows // SUBLANE) * SUBLANE)
    ada = _ada_call(c_all, ada_w, ada_b)
    ada_p = ada[:, :bp].reshape(depth, bp, N_ADA, d)
    ada_s = ada[:, bp:n_c_rows].reshape(depth, bs, N_ADA, d)

    rw = []
    for i in range(n_a):
        vec = lambda a: a[i].reshape(1, 1, d)
        rw.append(dict(
            mu=rw_mu[i], wr=rw_w_rkv[i, 0].astype(bf16), wk=rw_w_rkv[i, 1].astype(bf16), wv=rw_w_rkv[i, 2].astype(bf16),
            w1=_pad_to(rw_w1[i], 1, RW_LORA_PAD).astype(bf16), w2=_pad_to(rw_w2[i], 0, RW_LORA_PAD).astype(bf16),
            a1=_pad_to(rw_a1[i], 1, RW_LORA_PAD).astype(bf16), a2=_pad_to(rw_a2[i], 0, RW_LORA_PAD).astype(bf16),
            g1=rw_g1[i].astype(bf16), g2=rw_g2[i].astype(bf16), w0=vec(rw_w0), a0=vec(rw_a0),
            k_k=vec(rw_k_k), k_a=vec(rw_k_a), r_k=rw_r_k[i].reshape(1, 1, d), gn_w=vec(rw_gn_w), gn_b=vec(rw_gn_b),
            w_o=rw_w_o[i].astype(bf16)))
    gl = []
    for i in range(n_b):
        n_main = gla_w_in.shape[2] - gla_a_w2.shape[1]
        gl.append(dict(
            w_main=gla_w_in[i, :, :n_main].astype(bf16),
            w_gate=_pad_to(gla_w_in[i, :, n_main:], 1, GLA_LORA_PAD).astype(bf16),
            a_w2=_pad_to(gla_a_w2[i], 0, GLA_LORA_PAD).astype(bf16), a_b=gla_a_b[i].reshape(1, 1, -1),
            o_norm=gla_o_norm[i].reshape(1, -1), w_o=gla_w_o[i].astype(bf16)))
    s5 = [_s5_params(s5_A_re[i], s5_A_im[i], s5_log_dt[i], s5_B_re[i], s5_B_im[i], s5_C_re[i], s5_C_im[i],
                     s5_D[i], s5_glu_w[i], s5_glu_b[i]) for i in range(n_c)]
    weights = dict(norm_g=norm_g, ffn_w_in=ffn_w_in.astype(bf16), ffn_w_out=ffn_w_out.astype(bf16),
                   rw=rw, gla=gl, s5=s5, final_g=final_g)

    zeros = lambda s: jnp.zeros((s.shape[0], bp) + s.shape[2:], f32)
    p_states = tuple(zeros(s) for s in (state_rwkv_wkv, state_rwkv_shift, state_gla, state_s5_re, state_s5_im))
    s_states = (state_rwkv_wkv, state_rwkv_shift, state_gla, state_s5_re, state_s5_im)

    tt_p = min(ROW_TILE, t_p)
    nb_s = min(bs, ROW_TILE // t_s)
    rw_chunk = min(RW_HEAD, t_p)
    y_p, p_wkv, p_shift, p_gla, p_re, p_im = _trunk(x_prompt, ada_p, p_states, weights, 1, tt_p,
                                                     (1, rw_chunk, min(RW_SCAN_TBLK, t_p)))
    nb_scan = max(1, min(bs, RW_HEAD // t_s))
    y_s, s_wkv, s_shift, s_gla, s_re, s_im = _trunk(x_sample, ada_s, s_states, weights, nb_s, t_s,
                                                     (nb_scan, t_s, t_s))
    return (y_p, y_s, p_wkv, p_shift, p_gla, p_re, p_im, s_wkv, s_shift, s_gla, s_re, s_im)
```

```python
import functools
import math

import jax
import jax.numpy as jnp
from jax import lax
from jax.experimental import pallas as pl
from jax.experimental.pallas import tpu as pltpu

f32 = jnp.float32
bf16 = jnp.bfloat16
HI = lax.Precision.HIGHEST

EPS = 1e-6
N_ADA = 9
RW_HEAD = 64
RW_GN_EPS = 64e-5
RW_LORA_PAD = 128
GLA_HEADS = 4
GLA_GATE_NORM = 16.0
GLA_CHUNK = 64
GLA_LORA_PAD = 128
S5_GROUP = 16
S5_STATE = 64
S5_SLAB_GROUPS = 8
LANE = 128
SUBLANE = 8
VMEM_LIMIT_BYTES = 56 * 1024 * 1024

ROW_TILE = 512
FFN_TILE_F = 512
MM_TILE_N = 512
ADA_TILE_N = 1024
RW_PROJ_ROWS = 512
RW_PROJ_TILE_N = 256
RW_SCAN_TBLK = 256
RW_SCAN_HEADS_PROMPT = 4
RW_SCAN_HEADS_SAMPLE = 8
GLA_SCAN_TBLK = 256
S5_TBLK = 128
S5_LANE_CHUNK = 1024


def _cparams(*sem):
    return pltpu.CompilerParams(dimension_semantics=sem, vmem_limit_bytes=VMEM_LIMIT_BYTES)


def _dot(a, b):
    return jnp.dot(a, b, preferred_element_type=f32)


def _dot_hi(a, b):
    return jnp.dot(a, b, precision=HI, preferred_element_type=f32)


def _dot_nt_hi(a, b):
    return lax.dot_general(a, b, (((1,), (1,)), ((), ())), precision=HI, preferred_element_type=f32)


def _dot_tn_hi(a, b):
    return lax.dot_general(a, b, (((0,), (0,)), ((), ())), precision=HI, preferred_element_type=f32)


def _normmod(x, g, shift, scale):
    ms = jnp.mean(x * x, axis=-1, keepdims=True)
    y = x * lax.rsqrt(ms + EPS) * g
    return y * (1.0 + scale) + shift


def _ada_kernel(c_ref, w_ref, b_ref, o_ref):
    c = c_ref[...]
    s = (c * jax.nn.sigmoid(c)).astype(bf16)
    o_ref[0] = _dot(s, w_ref[0].astype(bf16)) + b_ref[0]


def _ada_call(c_all, ada_w, ada_b):
    depth, d, n = ada_w.shape
    m = c_all.shape[0]
    tn = min(ADA_TILE_N, n)
    return pl.pallas_call(
        _ada_kernel,
        grid=(depth, n // tn),
        in_specs=[pl.BlockSpec((m, d), lambda l, j: (0, 0)),
                  pl.BlockSpec((1, d, tn), lambda l, j: (l, 0, j)),
                  pl.BlockSpec((1, 1, tn), lambda l, j: (l, 0, j))],
        out_specs=pl.BlockSpec((1, m, tn), lambda l, j: (l, 0, j)),
        out_shape=jax.ShapeDtypeStruct((depth, m, n), f32),
        compiler_params=_cparams("arbitrary", "arbitrary"),
        name="ada",
    )(c_all, ada_w, ada_b.reshape(depth, 1, n))


def _ffn_kernel(x_ref, g_ref, sh_ref, sc_ref, gt_ref, wg_ref, wu_ref, wo_ref, o_ref, h_scr, acc_scr):
    j = pl.program_id(1)
    nb, tt, d = x_ref.shape

    @pl.when(j == 0)
    def _():
        h = _normmod(x_ref[...], g_ref[...], sh_ref[...], sc_ref[...])
        h_scr[...] = h.reshape(nb * tt, d).astype(bf16)
        acc_scr[...] = jnp.zeros_like(acc_scr)

    hb = h_scr[...]
    gate = _dot(hb, wg_ref[...])
    up = _dot(hb, wu_ref[...])
    act = (gate * jax.nn.sigmoid(gate) * up).astype(bf16)
    acc_scr[...] += _dot(act, wo_ref[...])

    @pl.when(j == pl.num_programs(1) - 1)
    def _():
        o_ref[...] = x_ref[...] + 0.5 * gt_ref[...] * acc_scr[...].reshape(nb, tt, d)


def _ffn_call(x, g, shift, scale, gate, w_in, w_out, l, s, nb, tt):
    nseq, t, d = x.shape
    f = w_out.shape[2]
    tf = min(FFN_TILE_F, f)
    nf = f // tf
    rows = pl.BlockSpec((nb, tt, d), lambda i, j: (i, 0, 0)) if nb > 1 else pl.BlockSpec((1, tt, d), lambda i, j: (i // (t // tt), i % (t // tt), 0))
    mod = _mod_spec(nb, t, tt, d)
    grid_rows = nseq // nb if nb > 1 else nseq * (t // tt)
    return pl.pallas_call(
        _ffn_kernel,
        grid=(grid_rows, nf),
        in_specs=[rows, pl.BlockSpec((1, 1, d), lambda i, j: (0, 0, 0)), mod, mod, mod,
                  pl.BlockSpec((None, None, d, tf), lambda i, j: (l, s, 0, j)),
                  pl.BlockSpec((None, None, d, tf), lambda i, j: (l, s, 0, nf + j)),
                  pl.BlockSpec((None, None, tf, d), lambda i, j: (l, s, j, 0))],
        out_specs=rows,
        out_shape=jax.ShapeDtypeStruct(x.shape, f32),
        scratch_shapes=[pltpu.VMEM((nb * tt, d), bf16), pltpu.VMEM((nb * tt, d), f32)],
        compiler_params=_cparams("parallel", "arbitrary"),
        name="ffn",
    )(x, g, shift, scale, gate, w_in, w_in, w_out)


def _mod_spec(nb, t, tt, width):
    if nb > 1:
        return pl.BlockSpec((nb, 1, width), lambda i, j: (i, 0, 0))
    return pl.BlockSpec((1, 1, width), lambda i, j: (i // (t // tt), 0, 0))


def _mm_kernel(*refs, n_row, n_mod, n_vec, n_w, n_erow, n_emod, n_evec, n_out, prologue, epilogue):
    pos = 0

    def take(n):
        nonlocal pos
        out = refs[pos:pos + n]
        pos += n
        return out

    rows, mods, vecs, ws = take(n_row), take(n_mod), take(n_vec), take(n_w)
    erows, emods, evecs, outs = take(n_erow), take(n_emod), take(n_evec), take(n_out)
    a_scr = refs[pos]
    nb, tt, k = rows[0].shape
    tn = ws[0].shape[-1]

    @pl.when(pl.program_id(1) == 0)
    def _():
        a = prologue([r[...] for r in rows], [m[...] for m in mods], [v[...] for v in vecs])
        a_scr[...] = a.reshape(nb * tt, k).astype(bf16)

    ab = a_scr[...]
    accs = [_dot(ab, w[...]).reshape(nb, tt, tn) for w in ws]
    res = epilogue(accs, [r[...] for r in erows], [m[...] for m in emods], [v[...] for v in evecs])
    for o_ref, o in zip(outs, res):
        o_ref[...] = o


def _mm_call(name, rows, mods, vecs, ws, erows, emods, evecs, n_total, n_out, prologue, epilogue, nb, tt, tn=MM_TILE_N):
    nseq, t, k = rows[0].shape
    tn = min(tn, n_total)
    nj = n_total // tn
    if nb > 1:
        grid_rows = nseq // nb
        rmap = lambda i, j: (i, 0, 0)
        ermap = lambda i, j: (i, 0, j)
        mmap = lambda i, j: (i, 0, 0)
        emmap = lambda i, j: (i, 0, j)
    else:
        per = t // tt
        grid_rows = nseq * per
        rmap = lambda i, j: (i // per, i % per, 0)
        ermap = lambda i, j: (i // per, i % per, j)
        mmap = lambda i, j: (i // per, 0, 0)
        emmap = lambda i, j: (i // per, 0, j)
    in_specs = ([pl.BlockSpec((nb, tt, k), rmap)] * len(rows)
                + [pl.BlockSpec((nb, 1, k), mmap)] * len(mods)
                + [pl.BlockSpec((1, 1, k), lambda i, j: (0, 0, 0))] * len(vecs)
                + [pl.BlockSpec((k, tn), functools.partial(lambda i, j, off: (0, off + j), off=off)) for _, off in ws]
                + [pl.BlockSpec((nb, tt, tn), ermap)] * len(erows)
                + [pl.BlockSpec((nb, 1, tn), emmap)] * len(emods)
                + [pl.BlockSpec((1, 1, tn), lambda i, j: (0, 0, j))] * len(evecs))
    kern = functools.partial(_mm_kernel, n_row=len(rows), n_mod=len(mods), n_vec=len(vecs), n_w=len(ws),
                             n_erow=len(erows), n_emod=len(emods), n_evec=len(evecs), n_out=n_out,
                             prologue=prologue, epilogue=epilogue)
    out = pl.pallas_call(
        kern,
        grid=(grid_rows, nj),
        in_specs=in_specs,
        out_specs=[pl.BlockSpec((nb, tt, tn), ermap)] * n_out,
        out_shape=[jax.ShapeDtypeStruct((nseq, t, n_total), f32)] * n_out,
        scratch_shapes=[pltpu.VMEM((nb * tt, k), bf16)],
        compiler_params=_cparams("parallel", "arbitrary"),
        name=name,
    )(*rows, *mods, *vecs, *[w for w, _ in ws], *erows, *emods, *evecs)
    return out


def _pro_normmod(rows, mods, vecs):
    return _normmod(rows[0], vecs[0], mods[0], mods[1])


def _pro_mul(rows, mods, vecs):
    return rows[0] * rows[1]


def _pro_id(rows, mods, vecs):
    return rows[0]


def _epi_id(accs, erows, emods, evecs):
    return accs


def _epi_residual(accs, erows, emods, evecs):
    return [erows[0] + emods[0] * accs[0]]


def _normmod_kernel(x_ref, g_ref, sh_ref, sc_ref, o_ref):
    o_ref[...] = _normmod(x_ref[...], g_ref[...], sh_ref[...], sc_ref[...])


def _rms_kernel(x_ref, g_ref, o_ref):
    x = x_ref[...]
    o_ref[...] = x * lax.rsqrt(jnp.mean(x * x, axis=-1, keepdims=True) + EPS) * g_ref[...]


def _rowwise_call(kern, name, x, vec, mods, nb, tt):
    nseq, t, d = x.shape
    if nb > 1:
        grid_rows = nseq // nb
        rmap = lambda i: (i, 0, 0)
        mmap = lambda i: (i, 0, 0)
    else:
        per = t // tt
        grid_rows = nseq * per
        rmap = lambda i: (i // per, i % per, 0)
        mmap = lambda i: (i // per, 0, 0)
    return pl.pallas_call(
        kern,
        grid=(grid_rows,),
        in_specs=[pl.BlockSpec((nb, tt, d), rmap), pl.BlockSpec((1, 1, d), lambda i: (0, 0, 0))]
                 + [pl.BlockSpec((nb, 1, d), mmap)] * len(mods),
        out_specs=pl.BlockSpec((nb, tt, d), rmap),
        out_shape=jax.ShapeDtypeStruct(x.shape, f32),
        compiler_params=_cparams("parallel"),
        name=name,
    )(x, vec, *mods)


def _rwproj_kernel(h_ref, hp_ref, mu_ref, w1_ref, a1_ref, g1_ref, wr_ref, wk_ref, wv_ref, w2_ref, a2_ref, g2_ref,
                   w0_ref, a0_ref, r_ref, k_ref, v_ref, lw_ref, a_ref, g_ref, xm_scr, tw_scr, ta_scr, tg_scr):
    nb, tt, d = h_ref.shape
    rows = nb * tt
    tn = wr_ref.shape[-1]

    @pl.when(pl.program_id(1) == 0)
    def _():
        h = h_ref[...].reshape(rows, d)
        dlt = hp_ref[...].reshape(rows, d) - h
        mu = mu_ref[...]
        for p in range(3):
            xm_scr[p] = (h + dlt * mu[p:p + 1]).astype(bf16)
        xw = (h + dlt * mu[3:4]).astype(bf16)
        tw_scr[...] = jnp.tanh(_dot(xw, w1_ref[...])).astype(bf16)
        xa = (h + dlt * mu[4:5]).astype(bf16)
        ta_scr[...] = _dot(xa, a1_ref[...]).astype(bf16)
        xg = (h + dlt * mu[5:6]).astype(bf16)
        tg_scr[...] = jax.nn.sigmoid(_dot(xg, g1_ref[...])).astype(bf16)

    shp = (nb, tt, tn)
    r_ref[...] = _dot(xm_scr[0], wr_ref[...]).reshape(shp)
    k_ref[...] = _dot(xm_scr[1], wk_ref[...]).reshape(shp)
    v_ref[...] = _dot(xm_scr[2], wv_ref[...]).reshape(shp)
    w_log = -jax.nn.softplus(-(w0_ref[0] + _dot(tw_scr[...], w2_ref[...]))) - 0.5
    lw_ref[...] = (-jnp.exp(w_log)).reshape(shp)
    a_ref[...] = jax.nn.sigmoid(a0_ref[0] + _dot(ta_scr[...], a2_ref[...])).reshape(shp)
    g_ref[...] = _dot(tg_scr[...], g2_ref[...]).reshape(shp)


def _rwproj_call(h, h_prev, mu, w1, a1, g1, wr, wk, wv, w2, a2, g2, w0, a0, nb, tt):
    nseq, t, d = h.shape
    tn = min(RW_PROJ_TILE_N, d)
    if nb > 1:
        grid_rows = nseq // nb
        rmap = lambda i, j: (i, 0, 0)
        omap = lambda i, j: (i, 0, j)
    else:
        per = t // tt
        grid_rows = nseq * per
        rmap = lambda i, j: (i // per, i % per, 0)
        omap = lambda i, j: (i // per, i % per, j)
    full = lambda a: pl.BlockSpec(a.shape, lambda i, j: (0,) * a.ndim)
    col = lambda a: pl.BlockSpec((a.shape[0], tn), lambda i, j: (0, j))
    rows = nb * tt
    return pl.pallas_call(
        _rwproj_kernel,
        grid=(grid_rows, d // tn),
        in_specs=[pl.BlockSpec((nb, tt, d), rmap), pl.BlockSpec((nb, tt, d), rmap), full(mu), full(w1), full(a1), full(g1),
                  col(wr), col(wk), col(wv), col(w2), col(a2), col(g2),
                  pl.BlockSpec((1, 1, tn), lambda i, j: (0, 0, j)), pl.BlockSpec((1, 1, tn), lambda i, j: (0, 0, j))],
        out_specs=[pl.BlockSpec((nb, tt, tn), omap)] * 6,
        out_shape=[jax.ShapeDtypeStruct((nseq, t, d), f32)] * 6,
        scratch_shapes=[pltpu.VMEM((3, rows, d), bf16), pltpu.VMEM((rows, w1.shape[1]), bf16),
                        pltpu.VMEM((rows, a1.shape[1]), bf16), pltpu.VMEM((rows, g1.shape[1]), bf16)],
        compiler_params=_cparams("parallel", "arbitrary"),
        name="rwkv_proj",
    )(h, h_prev, mu, w1, a1, g1, wr, wk, wv, w2, a2, g2, w0, a0)


def _dot_nt(a, b):
    return lax.dot_general(a, b, (((1,), (1,)), ((), ())), preferred_element_type=f32)


def _dot_tn(a, b):
    return lax.dot_general(a, b, (((0,), (0,)), ((), ())), preferred_element_type=f32)


def _rwkv_chunk_kernel(r_ref, k_ref, v_ref, lw_ref, a_ref, kk_ref, ka_ref, rk_ref, gw_ref, gb_ref, s0_ref,
                       y_ref, sout_ref, s_scr, rp_scr, y0_scr, bon_scr, m_scr, n_scr, *, nb, tt, n_inner):
    c = pl.program_id(2)
    n = RW_HEAD
    big = nb * tt
    n_dbl = max(1, math.ceil(math.log2(tt)))

    @pl.when(c == 0)
    def _():
        s_scr[...] = s0_ref[...]

    ti = lax.broadcasted_iota(jnp.int32, (big, big), 0)
    si = lax.broadcasted_iota(jnp.int32, (big, big), 1)
    same = (ti // tt) == (si // tt)
    incl = jnp.logical_and(same, si <= ti)
    strict = jnp.logical_and(same, si < ti)
    incl_f = incl.astype(f32)
    eye_f = (ti == si).astype(f32)
    eye_n = (lax.broadcasted_iota(jnp.int32, (n, n), 0) == lax.broadcasted_iota(jnp.int32, (n, n), 1)).astype(f32)

    kkp, kap, rkp, gwp, gbp = kk_ref[0], ka_ref[0], rk_ref[0], gw_ref[0], gb_ref[0]

    hp = r_ref.shape[-1] // n

    def load(ref, ci):
        if nb > 1:
            return ref[...].reshape(big, hp * n)
        return ref[0, ci * big:(ci + 1) * big, :]

    ch = []
    for ci in range(n_inner):
        r, k, v, lw, a = (load(ref, ci) for ref in (r_ref, k_ref, v_ref, lw_ref, a_ref))
        cum = _dot_hi(incl_f, lw)
        tot = cum[big - 1:big, :] if nb == 1 else _dot_hi(same.astype(f32), lw)
        g_in = jnp.exp(cum)
        g_prev = jnp.exp(cum - lw)
        g_inv = jnp.exp(-cum)
        g_rest = jnp.exp(tot - cum)
        g_tot = jnp.exp(tot)
        kk = k * kkp
        k2 = k * (1.0 + (a - 1.0) * kap)
        for hh in range(hp):
            sl = slice(hh * n, (hh + 1) * n)
            kk_h = kk[:, sl]
            kk_h = kk_h / jnp.maximum(jnp.sqrt(jnp.sum(kk_h * kk_h, axis=1, keepdims=True)), 1e-12)
            b_h = kk_h * a[:, sl]
            r_h, k_h, v_h = r[:, sl], k2[:, sl], v[:, sl]
            rt = r_h * g_in[:, sl]
            at = -kk_h * g_prev[:, sl]
            bon_scr[ci, hh] = jnp.sum(r_h * k_h * rkp[:, sl], axis=1, keepdims=True) * v_h
            ch.append(dict(
                ci=ci, hh=hh, rt=rt, at=at, v=v_h, vb=v_h.astype(bf16), gt=g_tot[:, sl],
                bhat=b_h * g_rest[:, sl], khat=k_h * g_rest[:, sl],
                ra=jnp.concatenate([rt, at], axis=0).astype(bf16),
                bk=jnp.concatenate([b_h * g_inv[:, sl], k_h * g_inv[:, sl]], axis=0).astype(bf16)))
    for q in ch:
        q["amat"] = _dot_nt(q["ra"], q["bk"])
    for q in ch:
        amat = q["amat"]
        q["a_rb"] = jnp.where(incl, amat[:big, :big], 0.0).astype(bf16)
        q["a_rk"] = jnp.where(incl, amat[:big, big:], 0.0).astype(bf16)
        q["pw"] = jnp.where(strict, amat[big:, :big], 0.0)
        q["a_ak"] = jnp.where(strict, amat[big:, big:], 0.0).astype(bf16)
        q["tm"] = eye_f + q["pw"]
    for _ in range(n_dbl - 1):
        for q in ch:
            pwb = q["pw"].astype(bf16)
            q["pw"] = _dot(pwb, pwb)
        for q in ch:
            q["tm"] = q["tm"] + _dot(q["tm"].astype(bf16), q["pw"].astype(bf16))
    for q in ch:
        q["akv"] = _dot(q["a_ak"], q["vb"])
    for q in ch:
        q["wa"] = _dot(q["tm"].astype(bf16), jnp.concatenate([q["akv"], q["at"]], axis=1).astype(bf16))
    for q in ch:
        q["aw"] = _dot(q["a_rb"], q["wa"].astype(bf16))
        q["arkv"] = _dot(q["a_rk"], q["vb"])
    for q in ch:
        ci, hh, wa = q["ci"], q["hh"], q["wa"]
        rp_scr[ci, hh] = q["rt"] + q["aw"][:, n:]
        y0_scr[ci, hh] = q["aw"][:, :n] + q["arkv"]
        for i in range(nb):
            rows = slice(i * tt, (i + 1) * tt)
            bh_i = q["bhat"][rows].astype(bf16)
            m_scr[ci, hh, i] = eye_n * q["gt"][i * tt:i * tt + 1] + _dot_tn(wa[rows, n:].astype(bf16), bh_i)
            n_scr[ci, hh, i] = _dot_tn(jnp.concatenate([wa[rows, :n], q["v"][rows]], axis=0).astype(bf16),
                                       jnp.concatenate([q["bhat"][rows], q["khat"][rows]], axis=0).astype(bf16))

    for ci in range(n_inner):
        ys = []
        for hh in range(hp):
            sl = slice(hh * n, (hh + 1) * n)
            rp = rp_scr[ci, hh]
            parts = []
            for i in range(nb):
                sb = s_scr[i, hh].astype(bf16)
                parts.append(_dot_nt(rp[i * tt:(i + 1) * tt].astype(bf16), sb))
                s_scr[i, hh] = _dot(sb, m_scr[ci, hh, i].astype(bf16)) + n_scr[ci, hh, i]
            y = (parts[0] if nb == 1 else jnp.concatenate(parts, axis=0)) + y0_scr[ci, hh]
            mean = jnp.mean(y, axis=1, keepdims=True)
            var = jnp.mean(jnp.square(y - mean), axis=1, keepdims=True)
            ys.append((y - mean) * lax.rsqrt(var + RW_GN_EPS) * gwp[:, sl] + gbp[:, sl] + bon_scr[ci, hh])
        yo = jnp.concatenate(ys, axis=1)
        if nb > 1:
            y_ref[...] = yo.reshape(nb, tt, hp * n)
        else:
            y_ref[0, ci * big:(ci + 1) * big, :] = yo

    @pl.when(c == pl.num_programs(2) - 1)
    def _():
        sout_ref[...] = s_scr[...]


def _rwscan_call(r, k, v, lw, a, k_k, k_a, r_k, gn_w, gn_b, s0, nb, tt, tblk, hp):
    nseq, t, d = r.shape
    nh = d // RW_HEAD
    lanes = hp * RW_HEAD
    nc = t // tblk
    n_inner = tblk // tt if nb == 1 else 1
    xspec = pl.BlockSpec((nb, tblk, lanes), lambda i, p, c: (i, c, p))
    pspec = pl.BlockSpec((1, 1, lanes), lambda i, p, c: (0, 0, p))
    sspec = pl.BlockSpec((nb, hp, RW_HEAD, RW_HEAD), lambda i, p, c: (i, p, 0, 0))
    kern = functools.partial(_rwkv_chunk_kernel, nb=nb, tt=tt, n_inner=n_inner)
    chunk_rows = pltpu.VMEM((n_inner, hp, nb * tt, RW_HEAD), f32)
    chunk_mats = pltpu.VMEM((n_inner, hp, nb, RW_HEAD, RW_HEAD), f32)
    return pl.pallas_call(
        kern,
        grid=(nseq // nb, nh // hp, nc),
        in_specs=[xspec] * 5 + [pspec] * 5 + [sspec],
        out_specs=[xspec, sspec],
        out_shape=[jax.ShapeDtypeStruct((nseq, t, d), f32), jax.ShapeDtypeStruct(s0.shape, f32)],
        scratch_shapes=[pltpu.VMEM((nb, hp, RW_HEAD, RW_HEAD), f32), chunk_rows, chunk_rows, chunk_rows,
                        chunk_mats, chunk_mats],
        compiler_params=_cparams("parallel", "parallel", "arbitrary"),
        name="rwkv_scan",
    )(r, k, v, lw, a, k_k, k_a, r_k, gn_w, gn_b, s0)


def _glascan_kernel(q_ref, k_ref, v_ref, r_ref, la_ref, on_ref, s0_ref, o_ref, sout_ref, s_scr, *, chunk, n_inner):
    c = pl.program_id(2)
    hk = q_ref.shape[-1]
    hv = v_ref.shape[-1]

    @pl.when(c == 0)
    def _():
        s_scr[...] = s0_ref[0, 0]

    ti = lax.broadcasted_iota(jnp.int32, (chunk, chunk), 0)
    si = lax.broadcasted_iota(jnp.int32, (chunk, chunk), 1)
    causal = si <= ti
    causal_f = causal.astype(f32)
    ones_cv = jnp.ones((chunk, hv), f32)

    def body(ci, carry):
        rows = pl.ds(pl.multiple_of(ci * chunk, chunk), chunk)
        q, k, v, la = q_ref[0, rows, :], k_ref[0, rows, :], v_ref[0, rows, :], la_ref[0, rows, :]
        b = _dot_hi(causal_f, la)
        b_last = b[chunk - 1:chunk, :]
        q_in = (q * (hk ** -0.5) * jnp.exp(b)).astype(bf16)
        k_in = (k * jnp.exp(-b)).astype(bf16)
        att = lax.dot_general(q_in, k_in, (((1,), (1,)), ((), ())), preferred_element_type=f32)
        att = jnp.where(causal, att, 0.0).astype(bf16)
        vb = v.astype(bf16)
        s = s_scr[...]
        o = _dot(att, vb) + _dot(q_in, s.astype(bf16))
        kd = (k * jnp.exp(b_last - b)).astype(bf16)
        dec = jnp.exp(_dot_tn_hi(la, ones_cv))
        s_scr[...] = s * dec + lax.dot_general(kd, vb, (((0,), (0,)), ((), ())), preferred_element_type=f32)
        o = o * lax.rsqrt(jnp.mean(o * o, axis=-1, keepdims=True) + EPS) * on_ref[...]
        rr = r_ref[0, rows, :]
        o_ref[0, rows, :] = o * (rr * jax.nn.sigmoid(rr))
        return carry

    lax.fori_loop(0, n_inner, body, 0)

    @pl.when(c == pl.num_programs(2) - 1)
    def _():
        sout_ref[0, 0] = s_scr[...]


def _glascan_call(qkvr, la, o_norm, s0, tblk):
    nseq, t, _ = qkvr.shape
    _, gh, hk, hv = s0.shape
    chunk = math.gcd(t, GLA_CHUNK)
    tblk = min(tblk, t)
    k_off = (gh * hk) // hk
    v_off = (2 * gh * hk) // hv
    r_off = (2 * gh * hk + gh * hv) // hv
    kern = functools.partial(_glascan_kernel, chunk=chunk, n_inner=tblk // chunk)
    return pl.pallas_call(
        kern,
        grid=(nseq, gh, t // tblk),
        in_specs=[pl.BlockSpec((1, tblk, hk), lambda b, h, c: (b, c, h)),
                  pl.BlockSpec((1, tblk, hk), lambda b, h, c: (b, c, k_off + h)),
                  pl.BlockSpec((1, tblk, hv), lambda b, h, c: (b, c, v_off + h)),
                  pl.BlockSpec((1, tblk, hv), lambda b, h, c: (b, c, r_off + h)),
                  pl.BlockSpec((1, tblk, hk), lambda b, h, c: (b, c, h)),
                  pl.BlockSpec((1, hv), lambda b, h, c: (0, 0)),
                  pl.BlockSpec((1, 1, hk, hv), lambda b, h, c: (b, h, 0, 0))],
        out_specs=[pl.BlockSpec((1, tblk, hv), lambda b, h, c: (b, c, h)),
                   pl.BlockSpec((1, 1, hk, hv), lambda b, h, c: (b, h, 0, 0))],
        out_shape=[jax.ShapeDtypeStruct((nseq, t, gh * hv), f32), jax.ShapeDtypeStruct(s0.shape, f32)],
        scratch_shapes=[pltpu.VMEM((hk, hv), f32)],
        compiler_params=_cparams("parallel", "parallel", "arbitrary"),
        name="gla_scan",
    )(qkvr, qkvr, qkvr, qkvr, la, o_norm, s0)


def _s5prep_kernel(are_ref, aim_ref, ldt_ref, bre_ref, bim_ref, abre_ref, abim_ref, bbre_ref, bbim_ref):
    a_re, a_im = are_ref[...], aim_ref[...]
    dt = jnp.exp(ldt_ref[...])
    mag = jnp.exp(a_re * dt)
    ab_re, ab_im = mag * jnp.cos(a_im * dt), mag * jnp.sin(a_im * dt)
    den = a_re * a_re + a_im * a_im
    nr = ab_re - 1.0
    cf_re = (nr * a_re + ab_im * a_im) / den
    cf_im = (ab_im * a_re - nr * a_im) / den
    abre_ref[...] = ab_re
    abim_ref[...] = ab_im
    b_re, b_im = bre_ref[...], bim_ref[...]
    bbre_ref[...] = cf_re * b_re - cf_im * b_im
    bbim_ref[...] = cf_re * b_im + cf_im * b_re


def _s5prep_call(a_re, a_im, log_dt, b_re_t, b_im_t):
    g, n = a_re.shape
    return pl.pallas_call(
        _s5prep_kernel,
        out_shape=[jax.ShapeDtypeStruct((g, 1, n), f32)] * 2 + [jax.ShapeDtypeStruct(b_re_t.shape, f32)] * 2,
        name="s5_prep",
    )(a_re.reshape(g, 1, n), a_im.reshape(g, 1, n), log_dt.reshape(g, 1, 1), b_re_t, b_im_t)


def _s5scan_kernel(x_ref, g_ref, sh_ref, sc_ref, wbu_ref, wc_ref, abre_ref, abim_ref, dsk_ref, x0re_ref, x0im_ref,
                   y_ref, xre_ref, xim_ref, bre_scr, bim_scr, sre_scr, sim_scr, *, lane_chunk):
    c = pl.program_id(1)
    _, tblk, d = x_ref.shape
    n_slab = wbu_ref.shape[0]
    half = wbu_ref.shape[2] // 2
    n_state = n_slab * half

    @pl.when(c == 0)
    def _():
        sre_scr[...] = x0re_ref[0]
        sim_scr[...] = x0im_ref[0]

    h = _normmod(x_ref[...], g_ref[...], sh_ref[...], sc_ref[...])[0]
    hb = h.astype(bf16)
    for s in range(n_slab):
        bu = _dot(hb[:, s * LANE:(s + 1) * LANE], wbu_ref[s])
        bre_scr[:, s * half:(s + 1) * half] = bu[:, :half]
        bim_scr[:, s * half:(s + 1) * half] = bu[:, half:]

    for q in range(n_state // lane_chunk):
        ls = slice(q * lane_chunk, (q + 1) * lane_chunk)
        ar, ai = abre_ref[:, ls], abim_ref[:, ls]

        def step(t, carry):
            xr, xi = carry
            row = pl.ds(t, 1)
            nxr = ar * xr - ai * xi + bre_scr[row, ls]
            nxi = ar * xi + ai * xr + bim_scr[row, ls]
            bre_scr[row, ls] = nxr
            bim_scr[row, ls] = nxi
            return nxr, nxi

        xr, xi = lax.fori_loop(0, tblk, step, (sre_scr[:, ls], sim_scr[:, ls]))
        sre_scr[:, ls] = xr
        sim_scr[:, ls] = xi

    for s in range(n_slab):
        xs = jnp.concatenate([bre_scr[:, s * half:(s + 1) * half], bim_scr[:, s * half:(s + 1) * half]], axis=1)
        ys = _dot(xs.astype(bf16), wc_ref[s])
        sl = slice(s * LANE, (s + 1) * LANE)
        y_ref[0, :, sl] = jax.nn.gelu(ys + dsk_ref[:, sl] * h[:, sl])

    @pl.when(c == pl.num_programs(1) - 1)
    def _():
        xre_ref[0] = sre_scr[...]
        xim_ref[0] = sim_scr[...]


def _s5scan_call(x, g, shift, scale, wbu, wc, ab_re, ab_im, d_skip, x0_re, x0_im, tblk):
    nseq, t, d = x.shape
    tblk = min(tblk, t)
    n_state = ab_re.shape[1]
    full = lambda a: pl.BlockSpec(a.shape, lambda b, c: (0,) * a.ndim)
    mod = pl.BlockSpec((1, 1, d), lambda b, c: (b, 0, 0))
    st = pl.BlockSpec((1, 1, n_state), lambda b, c: (b, 0, 0))
    kern = functools.partial(_s5scan_kernel, lane_chunk=min(S5_LANE_CHUNK, n_state))
    return pl.pallas_call(
        kern,
        grid=(nseq, t // tblk),
        in_specs=[pl.BlockSpec((1, tblk, d), lambda b, c: (b, c, 0)), pl.BlockSpec((1, 1, d), lambda b, c: (0, 0, 0)),
                  mod, mod, full(wbu), full(wc), full(ab_re), full(ab_im), full(d_skip), st, st],
        out_specs=[pl.BlockSpec((1, tblk, d), lambda b, c: (b, c, 0)), st, st],
        out_shape=[jax.ShapeDtypeStruct((nseq, t, d), f32), jax.ShapeDtypeStruct((nseq, 1, n_state), f32),
                   jax.ShapeDtypeStruct((nseq, 1, n_state), f32)],
        scratch_shapes=[pltpu.VMEM((tblk, n_state), f32), pltpu.VMEM((tblk, n_state), f32),
                        pltpu.VMEM((1, n_state), f32), pltpu.VMEM((1, n_state), f32)],
        compiler_params=_cparams("parallel", "arbitrary"),
        name="s5_scan",
    )(x, g, shift, scale, wbu, wc, ab_re, ab_im, d_skip, x0_re, x0_im)


def _epi_glu(accs, erows, emods, evecs):
    val = accs[0] + evecs[0]
    gate = accs[1] + evecs[1]
    return [erows[0] + emods[0] * (val * jax.nn.sigmoid(gate))]


def _rwkv_mixer(x, g, shift, scale, gm, shift_prev, wkv0, p, nb, tt, scan_cfg):
    nseq, t, d = x.shape
    h = _rowwise_call(_normmod_kernel, "normmod", x, g, [shift, scale], nb, tt)
    h_prev = jnp.concatenate([shift_prev[:, None, :], h[:, :-1]], axis=1)
    pnb, ptt = (RW_PROJ_ROWS // t, t) if nb > 1 else (1, min(RW_PROJ_ROWS, t))
    r, k, v, lw, a, gg = _rwproj_call(h, h_prev, p["mu"], p["w1"], p["a1"], p["g1"], p["wr"], p["wk"], p["wv"],
                                       p["w2"], p["a2"], p["g2"], p["w0"], p["a0"], pnb, ptt)
    y, wkv = _rwscan_call(r, k, v, lw, a, p["k_k"], p["k_a"], p["r_k"], p["gn_w"], p["gn_b"], wkv0, *scan_cfg)
    (x_new,) = _mm_call("rwkv_out", [y, gg], [], [], [(p["w_o"], 0)], [x], [gm], [], d, 1, _pro_mul, _epi_residual, nb, tt)
    return x_new, h[:, -1], wkv


def _gla_mixer(x, g, shift, scale, gm, s0, p, nb, tt):
    nseq, t, d = x.shape
    n_main = p["w_main"].shape[1]
    (qkvr,) = _mm_call("gla_in", [x], [shift, scale], [g], [(p["w_main"], 0)], [], [], [], n_main, 1,
                       _pro_normmod, _epi_id, nb, tt)
    (a_low,) = _mm_call("gla_gate_in", [x], [shift, scale], [g], [(p["w_gate"], 0)], [], [], [], GLA_LORA_PAD, 1,
                        _pro_normmod, _epi_id, nb, tt)

    def epi_la(accs, erows, emods, evecs):
        return [jax.nn.log_sigmoid(accs[0] + evecs[0]) / GLA_GATE_NORM]

    dk = p["a_w2"].shape[1]
    (la,) = _mm_call("gla_gate", [a_low], [], [], [(p["a_w2"], 0)], [], [], [p["a_b"]], dk, 1, _pro_id, epi_la, nb, tt)
    o, s_new = _glascan_call(qkvr, la, p["o_norm"], s0, GLA_SCAN_TBLK)
    (x_new,) = _mm_call("gla_out", [o], [], [], [(p["w_o"], 0)], [x], [gm], [], d, 1, _pro_id, _epi_residual, nb, tt)
    return x_new, s_new


def _s5_mixer(x, g, shift, scale, gm, x0_re, x0_im, p, nb, tt):
    nseq, t, d = x.shape
    n_state = p["ab_re"].shape[1]
    yg, xre, xim = _s5scan_call(x, g, shift, scale, p["wbu"], p["wc"], p["ab_re"], p["ab_im"], p["d_skip"],
                                x0_re.reshape(nseq, 1, n_state), x0_im.reshape(nseq, 1, n_state), S5_TBLK)
    (x_new,) = _mm_call("s5_glu", [yg], [], [], [(p["glu_w"], 0), (p["glu_w"], d // min(MM_TILE_N, d))], [x], [gm],
                        [p["glu_b_val"], p["glu_b_gate"]], d, 1, _pro_id, _epi_glu, nb, tt)
    return x_new, xre.reshape(x0_re.shape), xim.reshape(x0_im.shape)


def _s5_params(a_re, a_im, log_dt, b_re, b_im, c_re, c_im, d_skip, glu_w, glu_b):
    g, n, cg = b_re.shape
    d = g * cg
    ab_re, ab_im, bb_re, bb_im = _s5prep_call(a_re, a_im, log_dt, b_re.transpose(0, 2, 1), b_im.transpose(0, 2, 1))
    sg = S5_SLAB_GROUPS
    eye = jnp.eye(sg, dtype=f32)

    def bdiag(m):
        gq, rr, cc = m.shape
        m = m.reshape(gq // sg, sg, rr, cc)
        return jnp.einsum("sjrc,jk->sjrkc", m, eye).reshape(gq // sg, sg * rr, sg * cc)

    wbu = jnp.concatenate([bdiag(bb_re), bdiag(bb_im)], axis=2).astype(bf16)
    wc = jnp.concatenate([bdiag(c_re.transpose(0, 2, 1)), bdiag(-c_im.transpose(0, 2, 1))], axis=1).astype(bf16)
    return dict(wbu=wbu, wc=wc, ab_re=ab_re.reshape(1, g * n), ab_im=ab_im.reshape(1, g * n),
                d_skip=d_skip.reshape(1, d), glu_w=glu_w.astype(bf16),
                glu_b_val=glu_b[:d].reshape(1, 1, d), glu_b_gate=glu_b[d:].reshape(1, 1, d))


def _trunk(x, ada, states, weights, nb, tt, rw_scan_cfg):
    nseq, t, d = x.shape
    depth = ada.shape[0]
    wkv, shift, gla, s5_re, s5_im = states
    new = dict(wkv=[], shift=[], gla=[], re=[], im=[])
    ia = ib = ic = 0
    vec = lambda l, j: ada[l, :, j, :].reshape(nseq, 1, d)
    for l in range(depth):
        ng = lambda s: weights["norm_g"][l, s].reshape(1, 1, d)
        x = _ffn_call(x, ng(0), vec(l, 0), vec(l, 1), vec(l, 2), weights["ffn_w_in"], weights["ffn_w_out"], l, 0, nb, tt)
        kind = l % 3
        if kind == 0:
            x, sh, st = _rwkv_mixer(x, ng(1), vec(l, 3), vec(l, 4), vec(l, 5), shift[ia], wkv[ia], weights["rw"][ia],
                                    nb, tt, rw_scan_cfg)
            new["shift"].append(sh)
            new["wkv"].append(st)
            ia += 1
        elif kind == 1:
            x, st = _gla_mixer(x, ng(1), vec(l, 3), vec(l, 4), vec(l, 5), gla[ib], weights["gla"][ib], nb, tt)
            new["gla"].append(st)
            ib += 1
        else:
            x, sr, si = _s5_mixer(x, ng(1), vec(l, 3), vec(l, 4), vec(l, 5), s5_re[ic], s5_im[ic], weights["s5"][ic], nb, tt)
            new["re"].append(sr)
            new["im"].append(si)
            ic += 1
        x = _ffn_call(x, ng(2), vec(l, 6), vec(l, 7), vec(l, 8), weights["ffn_w_in"], weights["ffn_w_out"], l, 1, nb, tt)
    y = _rowwise_call(_rms_kernel, "final_norm", x, weights["final_g"].reshape(1, 1, d), [], nb, tt)
    return (y, jnp.stack(new["wkv"]), jnp.stack(new["shift"]), jnp.stack(new["gla"]),
            jnp.stack(new["re"]), jnp.stack(new["im"]))


def _pad_to(a, axis, size):
    pad = [(0, 0)] * a.ndim
    pad[axis] = (0, size - a.shape[axis])
    return jnp.pad(a, pad)


def kernel(x_prompt, x_sample, state_rwkv_wkv, state_rwkv_shift, state_gla, state_s5_re, state_s5_im, c_prompt, c_sample, norm_g, ada_w, ada_b, ffn_w_in, ffn_w_out, rw_mu, rw_w_rkv, rw_w0, rw_w1, rw_w2, rw_a0, rw_a1, rw_a2, rw_g1, rw_g2, rw_k_k, rw_k_a, rw_r_k, rw_gn_w, rw_gn_b, rw_w_o, gla_w_in, gla_a_w2, gla_a_b, gla_o_norm, gla_w_o, s5_A_re, s5_A_im, s5_log_dt, s5_B_re, s5_B_im, s5_C_re, s5_C_im, s5_D, s5_glu_w, s5_glu_b, final_g):
    bp, t_p, d = x_prompt.shape
    bs, t_s, _ = x_sample.shape
    depth = ada_w.shape[0]
    n_a, n_b, n_c = rw_mu.shape[0], gla_w_in.shape[0], s5_A_re.shape[0]

    n_c_rows = bp + bs
    c_all = _pad_to(jnp.concatenate([c_prompt, c_sample], axis=0), 0, -(-n_c_rows // SUBLANE) * SUBLANE)
    ada = _ada_call(c_all, ada_w, ada_b)
    ada_p = ada[:, :bp].reshape(depth, bp, N_ADA, d)
    ada_s = ada[:, bp:n_c_rows].reshape(depth, bs, N_ADA, d)

    rw = []
    for i in range(n_a):
        vec = lambda a: a[i].reshape(1, 1, d)
        rw.append(dict(
            mu=rw_mu[i], wr=rw_w_rkv[i, 0].astype(bf16), wk=rw_w_rkv[i, 1].astype(bf16), wv=rw_w_rkv[i, 2].astype(bf16),
            w1=_pad_to(rw_w1[i], 1, RW_LORA_PAD).astype(bf16), w2=_pad_to(rw_w2[i], 0, RW_LORA_PAD).astype(bf16),
            a1=_pad_to(rw_a1[i], 1, RW_LORA_PAD).astype(bf16), a2=_pad_to(rw_a2[i], 0, RW_LORA_PAD).astype(bf16),
            g1=rw_g1[i].astype(bf16), g2=rw_g2[i].astype(bf16), w0=vec(rw_w0), a0=vec(rw_a0),
            k_k=vec(rw_k_k), k_a=vec(rw_k_a), r_k=rw_r_k[i].reshape(1, 1, d), gn_w=vec(rw_gn_w), gn_b=vec(rw_gn_b),
            w_o=rw_w_o[i].astype(bf16)))
    gl = []
    for i in range(n_b):
        n_main = gla_w_in.shape[2] - gla_a_w2.shape[1]
        gl.append(dict(
            w_main=gla_w_in[i, :, :n_main].astype(bf16),
            w_gate=_pad_to(gla_w_in[i, :, n_main:], 1, GLA_LORA_PAD).astype(bf16),
            a_w2=_pad_to(gla_a_w2[i], 0, GLA_LORA_PAD).astype(bf16), a_b=gla_a_b[i].reshape(1, 1, -1),
            o_norm=gla_o_norm[i].reshape(1, -1), w_o=gla_w_o[i].astype(bf16)))
    s5 = [_s5_params(s5_A_re[i], s5_A_im[i], s5_log_dt[i], s5_B_re[i], s5_B_im[i], s5_C_re[i], s5_C_im[i],
                     s5_D[i], s5_glu_w[i], s5_glu_b[i]) for i in range(n_c)]
    weights = dict(norm_g=norm_g, ffn_w_in=ffn_w_in.astype(bf16), ffn_w_out=ffn_w_out.astype(bf16),
                   rw=rw, gla=gl, s5=s5, final_g=final_g)

    zeros = lambda s: jnp.zeros((s.shape[0], bp) + s.shape[2:], f32)
    p_states = tuple(zeros(s) for s in (state_rwkv_wkv, state_rwkv_shift, state_gla, state_s5_re, state_s5_im))
    s_states = (state_rwkv_wkv, state_rwkv_shift, state_gla, state_s5_re, state_s5_im)

    tt_p = min(ROW_TILE, t_p)
    nb_s = min(bs, ROW_TILE // t_s)
    rw_chunk = min(RW_HEAD, t_p)
    n_heads = d // RW_HEAD
    y_p, p_wkv, p_shift, p_gla, p_re, p_im = _trunk(x_prompt, ada_p, p_states, weights, 1, tt_p,
                                                     (1, rw_chunk, min(RW_SCAN_TBLK, t_p), min(RW_SCAN_HEADS_PROMPT, n_heads)))
    nb_scan = max(1, min(bs, RW_HEAD // t_s))
    y_s, s_wkv, s_shift, s_gla, s_re, s_im = _trunk(x_sample, ada_s, s_states, weights, nb_s, t_s,
                                                     (nb_scan, t_s, t_s, min(RW_SCAN_HEADS_SAMPLE, n_heads)))
    return (y_p, y_s, p_wkv, p_shift, p_gla, p_re, p_im, s_wkv, s_shift, s_gla, s_re, s_im)
```

```python
import functools
import math

import jax
import jax.numpy as jnp
from jax import lax
from jax.experimental import pallas as pl
from jax.experimental.pallas import tpu as pltpu

f32 = jnp.float32
bf16 = jnp.bfloat16
HI = lax.Precision.HIGHEST

EPS = 1e-6
N_ADA = 9
RW_HEAD = 64
RW_GN_EPS = 64e-5
RW_LORA_PAD = 128
GLA_HEADS = 4
GLA_GATE_NORM = 16.0
GLA_CHUNK = 64
GLA_LORA_PAD = 128
S5_GROUP = 16
S5_STATE = 64
S5_SLAB_GROUPS = 8
LANE = 128
SUBLANE = 8
VMEM_LIMIT_BYTES = 56 * 1024 * 1024

ROW_TILE = 512
FFN_TILE_F = 512
MM_TILE_N = 512
ADA_TILE_N = 1024
RW_PROJ_ROWS = 512
RW_PROJ_TILE_N = 256
RW_SCAN_TBLK = 256
RW_SCAN_HEADS_PROMPT = 4
RW_SCAN_HEADS_SAMPLE = 8
GLA_SCAN_TBLK = 256
GLA_SCAN_HEADS_PROMPT = 2
GLA_SCAN_HEADS_SAMPLE = 4
S5_TBLK = 128
S5_SEQS_SAMPLE = 8
S5_CARRY_ELEMS = 8192


def _cparams(*sem):
    return pltpu.CompilerParams(dimension_semantics=sem, vmem_limit_bytes=VMEM_LIMIT_BYTES)


def _dot(a, b):
    return jnp.dot(a, b, preferred_element_type=f32)


def _dot_hi(a, b):
    return jnp.dot(a, b, precision=HI, preferred_element_type=f32)


def _dot_nt_hi(a, b):
    return lax.dot_general(a, b, (((1,), (1,)), ((), ())), precision=HI, preferred_element_type=f32)


def _dot_tn_hi(a, b):
    return lax.dot_general(a, b, (((0,), (0,)), ((), ())), precision=HI, preferred_element_type=f32)


def _normmod(x, g, shift, scale):
    ms = jnp.mean(x * x, axis=-1, keepdims=True)
    y = x * lax.rsqrt(ms + EPS) * g
    return y * (1.0 + scale) + shift


def _ada_kernel(c_ref, w_ref, b_ref, o_ref):
    c = c_ref[...]
    s = (c * jax.nn.sigmoid(c)).astype(bf16)
    o_ref[0] = _dot(s, w_ref[0].astype(bf16)) + b_ref[0]


def _ada_call(c_all, ada_w, ada_b):
    depth, d, n = ada_w.shape
    m = c_all.shape[0]
    tn = min(ADA_TILE_N, n)
    return pl.pallas_call(
        _ada_kernel,
        grid=(depth, n // tn),
        in_specs=[pl.BlockSpec((m, d), lambda l, j: (0, 0)),
                  pl.BlockSpec((1, d, tn), lambda l, j: (l, 0, j)),
                  pl.BlockSpec((1, 1, tn), lambda l, j: (l, 0, j))],
        out_specs=pl.BlockSpec((1, m, tn), lambda l, j: (l, 0, j)),
        out_shape=jax.ShapeDtypeStruct((depth, m, n), f32),
        compiler_params=_cparams("arbitrary", "arbitrary"),
        name="ada",
    )(c_all, ada_w, ada_b.reshape(depth, 1, n))


def _ffn_kernel(x_ref, g_ref, sh_ref, sc_ref, gt_ref, wg_ref, wu_ref, wo_ref, o_ref, h_scr, acc_scr):
    j = pl.program_id(1)
    nb, tt, d = x_ref.shape

    @pl.when(j == 0)
    def _():
        h = _normmod(x_ref[...], g_ref[...], sh_ref[...], sc_ref[...])
        h_scr[...] = h.reshape(nb * tt, d).astype(bf16)
        acc_scr[...] = jnp.zeros_like(acc_scr)

    hb = h_scr[...]
    gate = _dot(hb, wg_ref[...])
    up = _dot(hb, wu_ref[...])
    act = (gate * jax.nn.sigmoid(gate) * up).astype(bf16)
    acc_scr[...] += _dot(act, wo_ref[...])

    @pl.when(j == pl.num_programs(1) - 1)
    def _():
        o_ref[...] = x_ref[...] + 0.5 * gt_ref[...] * acc_scr[...].reshape(nb, tt, d)


def _ffn_call(x, g, shift, scale, gate, w_in, w_out, l, s, nb, tt):
    nseq, t, d = x.shape
    f = w_out.shape[2]
    tf = min(FFN_TILE_F, f)
    nf = f // tf
    rows = pl.BlockSpec((nb, tt, d), lambda i, j: (i, 0, 0)) if nb > 1 else pl.BlockSpec((1, tt, d), lambda i, j: (i // (t // tt), i % (t // tt), 0))
    mod = _mod_spec(nb, t, tt, d)
    grid_rows = nseq // nb if nb > 1 else nseq * (t // tt)
    return pl.pallas_call(
        _ffn_kernel,
        grid=(grid_rows, nf),
        in_specs=[rows, pl.BlockSpec((1, 1, d), lambda i, j: (0, 0, 0)), mod, mod, mod,
                  pl.BlockSpec((None, None, d, tf), lambda i, j: (l, s, 0, j)),
                  pl.BlockSpec((None, None, d, tf), lambda i, j: (l, s, 0, nf + j)),
                  pl.BlockSpec((None, None, tf, d), lambda i, j: (l, s, j, 0))],
        out_specs=rows,
        out_shape=jax.ShapeDtypeStruct(x.shape, f32),
        scratch_shapes=[pltpu.VMEM((nb * tt, d), bf16), pltpu.VMEM((nb * tt, d), f32)],
        compiler_params=_cparams("parallel", "arbitrary"),
        name="ffn",
    )(x, g, shift, scale, gate, w_in, w_in, w_out)


def _mod_spec(nb, t, tt, width):
    if nb > 1:
        return pl.BlockSpec((nb, 1, width), lambda i, j: (i, 0, 0))
    return pl.BlockSpec((1, 1, width), lambda i, j: (i // (t // tt), 0, 0))


def _mm_kernel(*refs, n_row, n_mod, n_vec, n_w, n_erow, n_emod, n_evec, n_out, prologue, epilogue):
    pos = 0

    def take(n):
        nonlocal pos
        out = refs[pos:pos + n]
        pos += n
        return out

    rows, mods, vecs, ws = take(n_row), take(n_mod), take(n_vec), take(n_w)
    erows, emods, evecs, outs = take(n_erow), take(n_emod), take(n_evec), take(n_out)
    a_scr = refs[pos]
    nb, tt, k = rows[0].shape
    tn = ws[0].shape[-1]

    @pl.when(pl.program_id(1) == 0)
    def _():
        a = prologue([r[...] for r in rows], [m[...] for m in mods], [v[...] for v in vecs])
        a_scr[...] = a.reshape(nb * tt, k).astype(bf16)

    ab = a_scr[...]
    accs = [_dot(ab, w[...]).reshape(nb, tt, tn) for w in ws]
    res = epilogue(accs, [r[...] for r in erows], [m[...] for m in emods], [v[...] for v in evecs])
    for o_ref, o in zip(outs, res):
        o_ref[...] = o


def _mm_call(name, rows, mods, vecs, ws, erows, emods, evecs, n_total, n_out, prologue, epilogue, nb, tt, tn=MM_TILE_N):
    nseq, t, k = rows[0].shape
    tn = min(tn, n_total)
    assert n_total % tn == 0, (name, n_total, tn)
    nj = n_total // tn
    if nb > 1:
        grid_rows = nseq // nb
        rmap = lambda i, j: (i, 0, 0)
        ermap = lambda i, j: (i, 0, j)
        mmap = lambda i, j: (i, 0, 0)
        emmap = lambda i, j: (i, 0, j)
    else:
        per = t // tt
        grid_rows = nseq * per
        rmap = lambda i, j: (i // per, i % per, 0)
        ermap = lambda i, j: (i // per, i % per, j)
        mmap = lambda i, j: (i // per, 0, 0)
        emmap = lambda i, j: (i // per, 0, j)
    in_specs = ([pl.BlockSpec((nb, tt, k), rmap)] * len(rows)
                + [pl.BlockSpec((nb, 1, k), mmap)] * len(mods)
                + [pl.BlockSpec((1, 1, k), lambda i, j: (0, 0, 0))] * len(vecs)
                + [pl.BlockSpec((k, tn), functools.partial(lambda i, j, off: (0, off + j), off=off)) for _, off in ws]
                + [pl.BlockSpec((nb, tt, tn), ermap)] * len(erows)
                + [pl.BlockSpec((nb, 1, tn), emmap)] * len(emods)
                + [pl.BlockSpec((1, 1, tn), lambda i, j: (0, 0, j))] * len(evecs))
    kern = functools.partial(_mm_kernel, n_row=len(rows), n_mod=len(mods), n_vec=len(vecs), n_w=len(ws),
                             n_erow=len(erows), n_emod=len(emods), n_evec=len(evecs), n_out=n_out,
                             prologue=prologue, epilogue=epilogue)
    out = pl.pallas_call(
        kern,
        grid=(grid_rows, nj),
        in_specs=in_specs,
        out_specs=[pl.BlockSpec((nb, tt, tn), ermap)] * n_out,
        out_shape=[jax.ShapeDtypeStruct((nseq, t, n_total), f32)] * n_out,
        scratch_shapes=[pltpu.VMEM((nb * tt, k), bf16)],
        compiler_params=_cparams("parallel", "arbitrary"),
        name=name,
    )(*rows, *mods, *vecs, *[w for w, _ in ws], *erows, *emods, *evecs)
    return out


def _pro_normmod(rows, mods, vecs):
    return _normmod(rows[0], vecs[0], mods[0], mods[1])


def _pro_mul(rows, mods, vecs):
    return rows[0] * rows[1]


def _pro_id(rows, mods, vecs):
    return rows[0]


def _epi_id(accs, erows, emods, evecs):
    return accs


def _epi_residual(accs, erows, emods, evecs):
    return [erows[0] + emods[0] * accs[0]]


def _normmod_kernel(x_ref, g_ref, sh_ref, sc_ref, o_ref):
    o_ref[...] = _normmod(x_ref[...], g_ref[...], sh_ref[...], sc_ref[...])


def _rms_kernel(x_ref, g_ref, o_ref):
    x = x_ref[...]
    o_ref[...] = x * lax.rsqrt(jnp.mean(x * x, axis=-1, keepdims=True) + EPS) * g_ref[...]


def _rowwise_call(kern, name, x, vec, mods, nb, tt):
    nseq, t, d = x.shape
    if nb > 1:
        grid_rows = nseq // nb
        rmap = lambda i: (i, 0, 0)
        mmap = lambda i: (i, 0, 0)
    else:
        per = t // tt
        grid_rows = nseq * per
        rmap = lambda i: (i // per, i % per, 0)
        mmap = lambda i: (i // per, 0, 0)
    return pl.pallas_call(
        kern,
        grid=(grid_rows,),
        in_specs=[pl.BlockSpec((nb, tt, d), rmap), pl.BlockSpec((1, 1, d), lambda i: (0, 0, 0))]
                 + [pl.BlockSpec((nb, 1, d), mmap)] * len(mods),
        out_specs=pl.BlockSpec((nb, tt, d), rmap),
        out_shape=jax.ShapeDtypeStruct(x.shape, f32),
        compiler_params=_cparams("parallel"),
        name=name,
    )(x, vec, *mods)


def _rwproj_kernel(x_ref, xprev_ref, sp_ref, ng_ref, sh_ref, sc_ref, mu_ref, w1_ref, a1_ref, g1_ref,
                   wr_ref, wk_ref, wv_ref, w2_ref, a2_ref, g2_ref, w0_ref, a0_ref,
                   r_ref, k_ref, v_ref, lw_ref, a_ref, g_ref, hl_ref, xm_scr, tw_scr, ta_scr, tg_scr, *, tiles_per_seq):
    nb, tt, d = x_ref.shape
    rows = nb * tt
    tn = wr_ref.shape[-1]

    @pl.when(pl.program_id(1) == 0)
    def _():
        ng, sh, sc = ng_ref[...], sh_ref[...], sc_ref[...]
        h3 = _normmod(x_ref[...], ng, sh, sc)
        first = sp_ref[...]
        if tiles_per_seq > 1:
            h_before = _normmod(xprev_ref[...], ng, sh, sc)[:, SUBLANE - 1:SUBLANE, :]
            first = jnp.where(pl.program_id(0) % tiles_per_seq == 0, first, h_before)
        tok = lax.broadcasted_iota(jnp.int32, h3.shape, 1)
        hp3 = jnp.where(tok == 0, first, pltpu.roll(h3, 1, axis=1))
        hl_ref[...] = h3[:, tt - SUBLANE:, :]
        h = h3.reshape(rows, d)
        dlt = hp3.reshape(rows, d) - h
        mu = mu_ref[...]
        for p in range(3):
            xm_scr[p] = (h + dlt * mu[p:p + 1]).astype(bf16)
        xw = (h + dlt * mu[3:4]).astype(bf16)
        tw_scr[...] = jnp.tanh(_dot(xw, w1_ref[...])).astype(bf16)
        xa = (h + dlt * mu[4:5]).astype(bf16)
        ta_scr[...] = _dot(xa, a1_ref[...]).astype(bf16)
        xg = (h + dlt * mu[5:6]).astype(bf16)
        tg_scr[...] = jax.nn.sigmoid(_dot(xg, g1_ref[...])).astype(bf16)

    shp = (nb, tt, tn)
    r_ref[...] = _dot(xm_scr[0], wr_ref[...]).reshape(shp)
    k_ref[...] = _dot(xm_scr[1], wk_ref[...]).reshape(shp)
    v_ref[...] = _dot(xm_scr[2], wv_ref[...]).reshape(shp)
    w_log = -jax.nn.softplus(-(w0_ref[0] + _dot(tw_scr[...], w2_ref[...]))) - 0.5
    lw_ref[...] = (-jnp.exp(w_log)).reshape(shp)
    a_ref[...] = jax.nn.sigmoid(a0_ref[0] + _dot(ta_scr[...], a2_ref[...])).reshape(shp)
    g_ref[...] = _dot(tg_scr[...], g2_ref[...]).reshape(shp)


def _rwproj_call(x, shift_prev, ng, shift, scale, mu, w1, a1, g1, wr, wk, wv, w2, a2, g2, w0, a0, nb, tt):
    nseq, t, d = x.shape
    tn = min(RW_PROJ_TILE_N, d)
    if nb > 1:
        per = 1
        grid_rows = nseq // nb
        rmap = lambda i, j: (i, 0, 0)
        omap = lambda i, j: (i, 0, j)
        pmap = rmap
        mmap = lambda i, j: (i, 0, 0)
    else:
        per = t // tt
        grid_rows = nseq * per
        rmap = lambda i, j: (i // per, i % per, 0)
        omap = lambda i, j: (i // per, i % per, j)
        pmap = lambda i, j: (i // per, jnp.maximum((i % per) * (tt // SUBLANE) - 1, 0), 0)
        mmap = lambda i, j: (i // per, 0, 0)
    full = lambda a: pl.BlockSpec(a.shape, lambda i, j: (0,) * a.ndim)
    col = lambda a: pl.BlockSpec((a.shape[0], tn), lambda i, j: (0, j))
    mod = pl.BlockSpec((nb, 1, d), mmap)
    rows = nb * tt
    kern = functools.partial(_rwproj_kernel, tiles_per_seq=per)
    return pl.pallas_call(
        kern,
        grid=(grid_rows, d // tn),
        in_specs=[pl.BlockSpec((nb, tt, d), rmap), pl.BlockSpec((nb, SUBLANE, d), pmap), mod,
                  pl.BlockSpec((1, 1, d), lambda i, j: (0, 0, 0)), mod, mod, full(mu), full(w1), full(a1), full(g1),
                  col(wr), col(wk), col(wv), col(w2), col(a2), col(g2),
                  pl.BlockSpec((1, 1, tn), lambda i, j: (0, 0, j)), pl.BlockSpec((1, 1, tn), lambda i, j: (0, 0, j))],
        out_specs=[pl.BlockSpec((nb, tt, tn), omap)] * 6 + [pl.BlockSpec((nb, SUBLANE, d), rmap)],
        out_shape=[jax.ShapeDtypeStruct((nseq, t, d), f32)] * 6 + [jax.ShapeDtypeStruct((nseq, per * SUBLANE, d), f32)],
        scratch_shapes=[pltpu.VMEM((3, rows, d), bf16), pltpu.VMEM((rows, w1.shape[1]), bf16),
                        pltpu.VMEM((rows, a1.shape[1]), bf16), pltpu.VMEM((rows, g1.shape[1]), bf16)],
        compiler_params=_cparams("parallel", "arbitrary"),
        name="rwkv_proj",
    )(x, x, shift_prev, ng, shift, scale, mu, w1, a1, g1, wr, wk, wv, w2, a2, g2, w0, a0)


def _dot_nt(a, b):
    return lax.dot_general(a, b, (((1,), (1,)), ((), ())), preferred_element_type=f32)


def _dot_tn(a, b):
    return lax.dot_general(a, b, (((0,), (0,)), ((), ())), preferred_element_type=f32)


def _rwkv_chunk_kernel(r_ref, k_ref, v_ref, lw_ref, a_ref, kk_ref, ka_ref, rk_ref, gw_ref, gb_ref, s0_ref,
                       y_ref, sout_ref, s_scr, rp_scr, y0_scr, bon_scr, m_scr, n_scr, *, nb, tt, n_inner):
    c = pl.program_id(2)
    n = RW_HEAD
    big = nb * tt
    n_dbl = max(1, math.ceil(math.log2(tt)))

    @pl.when(c == 0)
    def _():
        s_scr[...] = s0_ref[...]

    ti = lax.broadcasted_iota(jnp.int32, (big, big), 0)
    si = lax.broadcasted_iota(jnp.int32, (big, big), 1)
    same = (ti // tt) == (si // tt)
    incl = jnp.logical_and(same, si <= ti)
    strict = jnp.logical_and(same, si < ti)
    cumsum = _make_masked_sum(incl)
    seqsum = _make_masked_sum(same)
    eye_f = (ti == si).astype(f32)
    eye_n = (lax.broadcasted_iota(jnp.int32, (n, n), 0) == lax.broadcasted_iota(jnp.int32, (n, n), 1)).astype(f32)

    kkp, kap, rkp, gwp, gbp = kk_ref[0], ka_ref[0], rk_ref[0], gw_ref[0], gb_ref[0]

    hp = r_ref.shape[-1] // n

    def load(ref, ci):
        if nb > 1:
            return ref[...].reshape(big, hp * n)
        return ref[0, ci * big:(ci + 1) * big, :]

    ch = []
    for ci in range(n_inner):
        r, k, v, lw, a = (load(ref, ci) for ref in (r_ref, k_ref, v_ref, lw_ref, a_ref))
        cum = cumsum(lw)
        tot = cum[big - 1:big, :] if nb == 1 else seqsum(lw)
        g_in = jnp.exp(cum)
        g_prev = jnp.exp(cum - lw)
        g_inv = jnp.exp(-cum)
        g_rest = jnp.exp(tot - cum)
        g_tot = jnp.exp(tot)
        kk = k * kkp
        k2 = k * (1.0 + (a - 1.0) * kap)
        for hh in range(hp):
            sl = slice(hh * n, (hh + 1) * n)
            kk_h = kk[:, sl]
            kk_h = kk_h / jnp.maximum(jnp.sqrt(jnp.sum(kk_h * kk_h, axis=1, keepdims=True)), 1e-12)
            b_h = kk_h * a[:, sl]
            r_h, k_h, v_h = r[:, sl], k2[:, sl], v[:, sl]
            rt = r_h * g_in[:, sl]
            at = -kk_h * g_prev[:, sl]
            bon_scr[ci, hh] = jnp.sum(r_h * k_h * rkp[:, sl], axis=1, keepdims=True) * v_h
            ch.append(dict(
                ci=ci, hh=hh, rt=rt, at=at, v=v_h, vb=v_h.astype(bf16), gt=g_tot[:, sl],
                bhat=b_h * g_rest[:, sl], khat=k_h * g_rest[:, sl],
                ra=jnp.concatenate([rt, at], axis=0).astype(bf16),
                bk=jnp.concatenate([b_h * g_inv[:, sl], k_h * g_inv[:, sl]], axis=0).astype(bf16)))
    for q in ch:
        q["amat"] = _dot_nt(q["ra"], q["bk"])
    for q in ch:
        amat = q["amat"]
        q["a_rb"] = jnp.where(incl, amat[:big, :big], 0.0).astype(bf16)
        q["a_rk"] = jnp.where(incl, amat[:big, big:], 0.0).astype(bf16)
        q["pw"] = jnp.where(strict, amat[big:, :big], 0.0)
        q["a_ak"] = jnp.where(strict, amat[big:, big:], 0.0).astype(bf16)
        q["tm"] = eye_f + q["pw"]
    for _ in range(n_dbl - 1):
        for q in ch:
            pwb = q["pw"].astype(bf16)
            q["pw"] = _dot(pwb, pwb)
        for q in ch:
            q["tm"] = q["tm"] + _dot(q["tm"].astype(bf16), q["pw"].astype(bf16))
    for q in ch:
        q["akv"] = _dot(q["a_ak"], q["vb"])
    for q in ch:
        q["wa"] = _dot(q["tm"].astype(bf16), jnp.concatenate([q["akv"], q["at"]], axis=1).astype(bf16))
    for q in ch:
        q["aw"] = _dot(q["a_rb"], q["wa"].astype(bf16))
        q["arkv"] = _dot(q["a_rk"], q["vb"])
    for q in ch:
        ci, hh, wa = q["ci"], q["hh"], q["wa"]
        rp_scr[ci, hh] = q["rt"] + q["aw"][:, n:]
        y0_scr[ci, hh] = q["aw"][:, :n] + q["arkv"]
        for i in range(nb):
            rows = slice(i * tt, (i + 1) * tt)
            bh_i = q["bhat"][rows].astype(bf16)
            m_scr[ci, hh, i] = eye_n * q["gt"][i * tt:i * tt + 1] + _dot_tn(wa[rows, n:].astype(bf16), bh_i)
            n_scr[ci, hh, i] = _dot_tn(jnp.concatenate([wa[rows, :n], q["v"][rows]], axis=0).astype(bf16),
                                       jnp.concatenate([q["bhat"][rows], q["khat"][rows]], axis=0).astype(bf16))

    for ci in range(n_inner):
        ys = []
        for hh in range(hp):
            sl = slice(hh * n, (hh + 1) * n)
            rp = rp_scr[ci, hh]
            parts = []
            for i in range(nb):
                sb = s_scr[i, hh].astype(bf16)
                parts.append(_dot_nt(rp[i * tt:(i + 1) * tt].astype(bf16), sb))
                s_scr[i, hh] = _dot(sb, m_scr[ci, hh, i].astype(bf16)) + n_scr[ci, hh, i]
            y = (parts[0] if nb == 1 else jnp.concatenate(parts, axis=0)) + y0_scr[ci, hh]
            mean = jnp.mean(y, axis=1, keepdims=True)
            var = jnp.mean(jnp.square(y - mean), axis=1, keepdims=True)
            ys.append((y - mean) * lax.rsqrt(var + RW_GN_EPS) * gwp[:, sl] + gbp[:, sl] + bon_scr[ci, hh])
        yo = jnp.concatenate(ys, axis=1)
        if nb > 1:
            y_ref[...] = yo.reshape(nb, tt, hp * n)
        else:
            y_ref[0, ci * big:(ci + 1) * big, :] = yo

    @pl.when(c == pl.num_programs(2) - 1)
    def _():
        sout_ref[...] = s_scr[...]


def _rwscan_call(r, k, v, lw, a, k_k, k_a, r_k, gn_w, gn_b, s0_all, layer, nb, tt, tblk, hp):
    nseq, t, d = r.shape
    nh = d // RW_HEAD
    lanes = hp * RW_HEAD
    nc = t // tblk
    n_inner = tblk // tt if nb == 1 else 1
    xspec = pl.BlockSpec((nb, tblk, lanes), lambda i, p, c: (i, c, p))
    pspec = pl.BlockSpec((1, 1, lanes), lambda i, p, c: (0, 0, p))
    sspec = pl.BlockSpec((nb, hp, RW_HEAD, RW_HEAD), lambda i, p, c: (i, p, 0, 0))
    kern = functools.partial(_rwkv_chunk_kernel, nb=nb, tt=tt, n_inner=n_inner)
    chunk_rows = pltpu.VMEM((n_inner, hp, nb * tt, RW_HEAD), f32)
    chunk_mats = pltpu.VMEM((n_inner, hp, nb, RW_HEAD, RW_HEAD), f32)
    return pl.pallas_call(
        kern,
        grid=(nseq // nb, nh // hp, nc),
        in_specs=[xspec] * 5 + [pspec] * 5
                 + [pl.BlockSpec((None, nb, hp, RW_HEAD, RW_HEAD), lambda i, p, c: (layer, i, p, 0, 0))],
        out_specs=[xspec, sspec],
        out_shape=[jax.ShapeDtypeStruct((nseq, t, d), f32), jax.ShapeDtypeStruct(s0_all.shape[1:], f32)],
        scratch_shapes=[pltpu.VMEM((nb, hp, RW_HEAD, RW_HEAD), f32), chunk_rows, chunk_rows, chunk_rows,
                        chunk_mats, chunk_mats],
        compiler_params=_cparams("parallel", "parallel", "arbitrary"),
        name="rwkv_scan",
    )(r, k, v, lw, a, k_k, k_a, r_k, gn_w, gn_b, s0_all)


def _split3(x):
    hi = x.astype(bf16)
    r1 = x - hi.astype(f32)
    mid = r1.astype(bf16)
    return hi, mid, (r1 - mid.astype(f32)).astype(bf16)


def _make_masked_sum(mask):
    rows = mask.shape[0]
    if (3 * rows) % 16:
        mask_f = mask.astype(f32)
        return lambda x: _dot_hi(mask_f, x)
    mask3 = jnp.concatenate([mask.astype(bf16)] * 3, axis=1)
    return lambda x: _dot(mask3, jnp.concatenate(_split3(x), axis=0))


def _glascan_kernel(q_ref, k_ref, v_ref, r_ref, la_ref, on_ref, s0_ref, o_ref, sout_ref, s_scr, *, chunk, n_inner, hb):
    c = pl.program_id(2)
    hk = q_ref.shape[-1] // hb
    hv = v_ref.shape[-1] // hb

    @pl.when(c == 0)
    def _():
        s_scr[...] = s0_ref[0]

    ti = lax.broadcasted_iota(jnp.int32, (chunk, chunk), 0)
    si = lax.broadcasted_iota(jnp.int32, (chunk, chunk), 1)
    causal = si <= ti
    cumsum = _make_masked_sum(causal)

    ch = []
    for ci in range(n_inner):
        rows = slice(ci * chunk, (ci + 1) * chunk)
        for h in range(hb):
            ks, vs = slice(h * hk, (h + 1) * hk), slice(h * hv, (h + 1) * hv)
            ch.append(dict(rows=rows, h=h, vs=vs, q=q_ref[0, rows, ks], k=k_ref[0, rows, ks],
                           vb=v_ref[0, rows, vs].astype(bf16), la=la_ref[0, rows, ks]))
    for z in ch:
        z["b"] = cumsum(z["la"])
    for z in ch:
        b = z["b"]
        b_last = b[chunk - 1:chunk, :]
        z["q_in"] = (z["q"] * (hk ** -0.5) * jnp.exp(b)).astype(bf16)
        z["k_in"] = (z["k"] * jnp.exp(-b)).astype(bf16)
        z["kd"] = (z["k"] * jnp.exp(b_last - b)).astype(bf16)
        z["dec"] = jnp.transpose(jnp.broadcast_to(jnp.exp(b_last), (SUBLANE, hk)))[:, 0:1]
    for z in ch:
        z["att"] = _dot_nt(z["q_in"], z["k_in"])
    for z in ch:
        z["ov"] = _dot(jnp.where(causal, z["att"], 0.0).astype(bf16), z["vb"])
        z["inc"] = _dot_tn(z["kd"], z["vb"])

    for z in ch:
        h, rows, vs = z["h"], z["rows"], z["vs"]
        s = s_scr[h]
        o = z["ov"] + _dot(z["q_in"], s.astype(bf16))
        s_scr[h] = s * z["dec"] + z["inc"]
        o = o * lax.rsqrt(jnp.mean(o * o, axis=-1, keepdims=True) + EPS) * on_ref[...]
        rr = r_ref[0, rows, vs]
        o_ref[0, rows, vs] = o * (rr * jax.nn.sigmoid(rr))

    @pl.when(c == pl.num_programs(2) - 1)
    def _():
        sout_ref[0] = s_scr[...]


def _glascan_call(qkvr, la, o_norm, s0, tblk, hb):
    nseq, t, _ = qkvr.shape
    _, gh, hk, hv = s0.shape
    chunk = math.gcd(t, GLA_CHUNK)
    tblk = min(tblk, t)
    k_off = (gh * hk) // (hb * hk)
    v_off = (2 * gh * hk) // (hb * hv)
    r_off = (2 * gh * hk + gh * hv) // (hb * hv)
    kern = functools.partial(_glascan_kernel, chunk=chunk, n_inner=tblk // chunk, hb=hb)
    return pl.pallas_call(
        kern,
        grid=(nseq, gh // hb, t // tblk),
        in_specs=[pl.BlockSpec((1, tblk, hb * hk), lambda b, h, c: (b, c, h)),
                  pl.BlockSpec((1, tblk, hb * hk), lambda b, h, c: (b, c, k_off + h)),
                  pl.BlockSpec((1, tblk, hb * hv), lambda b, h, c: (b, c, v_off + h)),
                  pl.BlockSpec((1, tblk, hb * hv), lambda b, h, c: (b, c, r_off + h)),
                  pl.BlockSpec((1, tblk, hb * hk), lambda b, h, c: (b, c, h)),
                  pl.BlockSpec((1, hv), lambda b, h, c: (0, 0)),
                  pl.BlockSpec((1, hb, hk, hv), lambda b, h, c: (b, h, 0, 0))],
        out_specs=[pl.BlockSpec((1, tblk, hb * hv), lambda b, h, c: (b, c, h)),
                   pl.BlockSpec((1, hb, hk, hv), lambda b, h, c: (b, h, 0, 0))],
        out_shape=[jax.ShapeDtypeStruct((nseq, t, gh * hv), f32), jax.ShapeDtypeStruct(s0.shape, f32)],
        scratch_shapes=[pltpu.VMEM((hb, hk, hv), f32)],
        compiler_params=_cparams("parallel", "parallel", "arbitrary"),
        name="gla_scan",
    )(qkvr, qkvr, qkvr, qkvr, la, o_norm, s0)


def _s5prep_kernel(are_ref, aim_ref, ldt_ref, bre_ref, bim_ref, abre_ref, abim_ref, bbre_ref, bbim_ref):
    a_re, a_im = are_ref[...], aim_ref[...]
    dt = jnp.exp(ldt_ref[...])
    mag = jnp.exp(a_re * dt)
    ab_re, ab_im = mag * jnp.cos(a_im * dt), mag * jnp.sin(a_im * dt)
    den = a_re * a_re + a_im * a_im
    nr = ab_re - 1.0
    cf_re = (nr * a_re + ab_im * a_im) / den
    cf_im = (ab_im * a_re - nr * a_im) / den
    abre_ref[...] = ab_re
    abim_ref[...] = ab_im
    b_re, b_im = bre_ref[...], bim_ref[...]
    bbre_ref[...] = cf_re * b_re - cf_im * b_im
    bbim_ref[...] = cf_re * b_im + cf_im * b_re


def _s5prep_call(a_re, a_im, log_dt, b_re_t, b_im_t):
    g, n = a_re.shape
    return pl.pallas_call(
        _s5prep_kernel,
        out_shape=[jax.ShapeDtypeStruct((g, 1, n), f32)] * 2 + [jax.ShapeDtypeStruct(b_re_t.shape, f32)] * 2,
        name="s5_prep",
    )(a_re.reshape(g, 1, n), a_im.reshape(g, 1, n), log_dt.reshape(g, 1, 1), b_re_t, b_im_t)


def _s5scan_kernel(x_ref, g_ref, sh_ref, sc_ref, wbu_ref, wc_ref, abre_ref, abim_ref, dsk_ref, x0re_ref, x0im_ref,
                   y_ref, xre_ref, xim_ref, bre_scr, bim_scr, sre_scr, sim_scr, *, lane_chunk, nb):
    c = pl.program_id(1)
    d = x_ref.shape[-1]
    rows = x_ref.shape[0] * x_ref.shape[1]
    tblk = rows // nb
    n_slab = wbu_ref.shape[0]
    half = wbu_ref.shape[2] // 2
    n_state = n_slab * half

    @pl.when(c == 0)
    def _():
        sre_scr[...] = x0re_ref[0]
        sim_scr[...] = x0im_ref[0]

    h = _normmod(x_ref[...], g_ref[...], sh_ref[...], sc_ref[...]).reshape(rows, d)
    hb = h.astype(bf16)
    for s in range(n_slab):
        bu = _dot(hb[:, s * LANE:(s + 1) * LANE], wbu_ref[s])
        bre_scr[:, s * half:(s + 1) * half] = bu[:, :half]
        bim_scr[:, s * half:(s + 1) * half] = bu[:, half:]

    for q in range(n_state // lane_chunk):
        ls = slice(q * lane_chunk, (q + 1) * lane_chunk)
        ar, ai = abre_ref[:, ls], abim_ref[:, ls]

        def step(t, carry):
            xr, xi = carry
            row = pl.ds(pl.multiple_of(t * nb, nb), nb)
            nxr = ar * xr - ai * xi + bre_scr[row, ls]
            nxi = ar * xi + ai * xr + bim_scr[row, ls]
            bre_scr[row, ls] = nxr
            bim_scr[row, ls] = nxi
            return nxr, nxi

        xr, xi = lax.fori_loop(0, tblk, step, (sre_scr[:, ls], sim_scr[:, ls]))
        sre_scr[:, ls] = xr
        sim_scr[:, ls] = xi

    for s in range(n_slab):
        xs = jnp.concatenate([bre_scr[:, s * half:(s + 1) * half], bim_scr[:, s * half:(s + 1) * half]], axis=1)
        ys = _dot(xs.astype(bf16), wc_ref[s])
        sl = slice(s * LANE, (s + 1) * LANE)
        y_ref[:, :, sl] = jax.nn.gelu(ys + dsk_ref[:, sl] * h[:, sl]).reshape(y_ref.shape[0], y_ref.shape[1], LANE)

    @pl.when(c == pl.num_programs(1) - 1)
    def _():
        xre_ref[0] = sre_scr[...]
        xim_ref[0] = sim_scr[...]


def _s5scan_call(x, g, shift, scale, wbu, wc, ab_re, ab_im, d_skip, x0_re, x0_im, tblk, nb):
    nseq, t, d = x.shape
    tblk = min(tblk, t)
    n_state = ab_re.shape[1]
    full = lambda a: pl.BlockSpec(a.shape, lambda b, c: (0,) * a.ndim)
    st = pl.BlockSpec((1, nb, n_state), lambda b, c: (b, 0, 0))
    st_shape = jax.ShapeDtypeStruct((nseq // nb, nb, n_state), f32)
    if nb > 1:
        assert tblk == t
        x = x.transpose(1, 0, 2)
        shift, scale = shift.reshape(1, nseq, d), scale.reshape(1, nseq, d)
        xspec = pl.BlockSpec((t, nb, d), lambda b, c: (0, b, 0))
        mod = pl.BlockSpec((1, nb, d), lambda b, c: (0, b, 0))
    else:
        xspec = pl.BlockSpec((1, tblk, d), lambda b, c: (b, c, 0))
        mod = pl.BlockSpec((1, 1, d), lambda b, c: (b, 0, 0))
    kern = functools.partial(_s5scan_kernel, lane_chunk=min(n_state, S5_CARRY_ELEMS // nb), nb=nb)
    rows = nb * tblk
    y, xre, xim = pl.pallas_call(
        kern,
        grid=(nseq // nb, t // tblk),
        in_specs=[xspec, pl.BlockSpec((1, 1, d), lambda b, c: (0, 0, 0)),
                  mod, mod, full(wbu), full(wc), full(ab_re), full(ab_im), full(d_skip), st, st],
        out_specs=[xspec, st, st],
        out_shape=[jax.ShapeDtypeStruct(x.shape, f32), st_shape, st_shape],
        scratch_shapes=[pltpu.VMEM((rows, n_state), f32), pltpu.VMEM((rows, n_state), f32),
                        pltpu.VMEM((nb, n_state), f32), pltpu.VMEM((nb, n_state), f32)],
        compiler_params=_cparams("parallel", "arbitrary"),
        name="s5_scan",
    )(x, g, shift, scale, wbu, wc, ab_re, ab_im, d_skip,
      x0_re.reshape(nseq // nb, nb, n_state), x0_im.reshape(nseq // nb, nb, n_state))
    return (y.transpose(1, 0, 2) if nb > 1 else y), xre, xim


def _epi_glu(accs, erows, emods, evecs):
    val = accs[0] + evecs[0]
    gate = accs[1] + evecs[1]
    return [erows[0] + emods[0] * (val * jax.nn.sigmoid(gate))]


def _rwkv_mixer(x, g, shift, scale, gm, shift_prev, wkv_all, layer, p, nb, tt, scan_cfg):
    nseq, t, d = x.shape
    pnb, ptt = (RW_PROJ_ROWS // t, t) if nb > 1 else (1, min(RW_PROJ_ROWS, t))
    r, k, v, lw, a, gg, h_tail = _rwproj_call(x, shift_prev[:, None, :], g, shift, scale, p["mu"], p["w1"], p["a1"], p["g1"],
                                               p["wr"], p["wk"], p["wv"], p["w2"], p["a2"], p["g2"], p["w0"], p["a0"], pnb, ptt)
    y, wkv = _rwscan_call(r, k, v, lw, a, p["k_k"], p["k_a"], p["r_k"], p["gn_w"], p["gn_b"], wkv_all, layer, *scan_cfg)
    (x_new,) = _mm_call("rwkv_out", [y, gg], [], [], [(p["w_o"], 0)], [x], [gm], [], d, 1, _pro_mul, _epi_residual, nb, tt)
    return x_new, h_tail[:, -1], wkv


def _gla_mixer(x, g, shift, scale, gm, s0, p, nb, tt):
    nseq, t, d = x.shape
    n_main = p["w_main"].shape[1]
    (qkvr,) = _mm_call("gla_in", [x], [shift, scale], [g], [(p["w_main"], 0)], [], [], [], n_main, 1,
                       _pro_normmod, _epi_id, nb, tt)
    (a_low,) = _mm_call("gla_gate_in", [x], [shift, scale], [g], [(p["w_gate"], 0)], [], [], [], GLA_LORA_PAD, 1,
                        _pro_normmod, _epi_id, nb, tt)

    def epi_la(accs, erows, emods, evecs):
        return [jax.nn.log_sigmoid(accs[0] + evecs[0]) / GLA_GATE_NORM]

    dk = p["a_w2"].shape[1]
    (la,) = _mm_call("gla_gate", [a_low], [], [], [(p["a_w2"], 0)], [], [], [p["a_b"]], dk, 1, _pro_id, epi_la, nb, tt)
    hb = min(s0.shape[1], GLA_SCAN_HEADS_SAMPLE if nb > 1 else GLA_SCAN_HEADS_PROMPT)
    o, s_new = _glascan_call(qkvr, la, p["o_norm"], s0, GLA_SCAN_TBLK, hb)
    (x_new,) = _mm_call("gla_out", [o], [], [], [(p["w_o"], 0)], [x], [gm], [], d, 1, _pro_id, _epi_residual, nb, tt)
    return x_new, s_new


def _s5_mixer(x, g, shift, scale, gm, x0_re, x0_im, p, nb, tt):
    nseq, t, d = x.shape
    n_state = p["ab_re"].shape[1]
    s5_nb = min(nseq, S5_SEQS_SAMPLE) if nb > 1 else 1
    yg, xre, xim = _s5scan_call(x, g, shift, scale, p["wbu"], p["wc"], p["ab_re"], p["ab_im"], p["d_skip"],
                                x0_re, x0_im, S5_TBLK, s5_nb)
    (x_new,) = _mm_call("s5_glu", [yg], [], [], [(p["glu_w"], 0), (p["glu_w"], d // min(MM_TILE_N, d))], [x], [gm],
                        [p["glu_b_val"], p["glu_b_gate"]], d, 1, _pro_id, _epi_glu, nb, tt)
    return x_new, xre.reshape(x0_re.shape), xim.reshape(x0_im.shape)


def _s5_params(a_re, a_im, log_dt, b_re, b_im, c_re, c_im, d_skip, glu_w, glu_b):
    g, n, cg = b_re.shape
    d = g * cg
    ab_re, ab_im, bb_re, bb_im = _s5prep_call(a_re, a_im, log_dt, b_re.transpose(0, 2, 1), b_im.transpose(0, 2, 1))
    sg = S5_SLAB_GROUPS
    eye = jnp.eye(sg, dtype=f32)

    def bdiag(m):
        gq, rr, cc = m.shape
        m = m.reshape(gq // sg, sg, rr, cc)
        return jnp.einsum("sjrc,jk->sjrkc", m, eye).reshape(gq // sg, sg * rr, sg * cc)

    wbu = jnp.concatenate([bdiag(bb_re), bdiag(bb_im)], axis=2).astype(bf16)
    wc = jnp.concatenate([bdiag(c_re.transpose(0, 2, 1)), bdiag(-c_im.transpose(0, 2, 1))], axis=1).astype(bf16)
    return dict(wbu=wbu, wc=wc, ab_re=ab_re.reshape(1, g * n), ab_im=ab_im.reshape(1, g * n),
                d_skip=d_skip.reshape(1, d), glu_w=glu_w.astype(bf16),
                glu_b_val=glu_b[:d].reshape(1, 1, d), glu_b_gate=glu_b[d:].reshape(1, 1, d))


def _trunk(x, ada, states, weights, nb, tt, rw_scan_cfg):
    nseq, t, d = x.shape
    depth = ada.shape[0]
    wkv, shift, gla, s5_re, s5_im = states
    new = dict(wkv=[], shift=[], gla=[], re=[], im=[])
    ia = ib = ic = 0
    vec = lambda l, j: ada[l, :, j, :].reshape(nseq, 1, d)
    for l in range(depth):
        ng = lambda s: weights["norm_g"][l, s].reshape(1, 1, d)
        x = _ffn_call(x, ng(0), vec(l, 0), vec(l, 1), vec(l, 2), weights["ffn_w_in"], weights["ffn_w_out"], l, 0, nb, tt)
        kind = l % 3
        if kind == 0:
            x, sh, st = _rwkv_mixer(x, ng(1), vec(l, 3), vec(l, 4), vec(l, 5), shift[ia], wkv, ia, weights["rw"][ia],
                                    nb, tt, rw_scan_cfg)
            new["shift"].append(sh)
            new["wkv"].append(st)
            ia += 1
        elif kind == 1:
            x, st = _gla_mixer(x, ng(1), vec(l, 3), vec(l, 4), vec(l, 5), gla[ib], weights["gla"][ib], nb, tt)
            new["gla"].append(st)
            ib += 1
        else:
            x, sr, si = _s5_mixer(x, ng(1), vec(l, 3), vec(l, 4), vec(l, 5), s5_re[ic], s5_im[ic], weights["s5"][ic], nb, tt)
            new["re"].append(sr)
            new["im"].append(si)
            ic += 1
        x = _ffn_call(x, ng(2), vec(l, 6), vec(l, 7), vec(l, 8), weights["ffn_w_in"], weights["ffn_w_out"], l, 1, nb, tt)
    y = _rowwise_call(_rms_kernel, "final_norm", x, weights["final_g"].reshape(1, 1, d), [], nb, tt)
    return (y, jnp.stack(new["wkv"]), jnp.stack(new["shift"]), jnp.stack(new["gla"]),
            jnp.stack(new["re"]), jnp.stack(new["im"]))


def _pad_to(a, axis, size):
    pad = [(0, 0)] * a.ndim
    pad[axis] = (0, size - a.shape[axis])
    return jnp.pad(a, pad)


def kernel(x_prompt, x_sample, state_rwkv_wkv, state_rwkv_shift, state_gla, state_s5_re, state_s5_im, c_prompt, c_sample, norm_g, ada_w, ada_b, ffn_w_in, ffn_w_out, rw_mu, rw_w_rkv, rw_w0, rw_w1, rw_w2, rw_a0, rw_a1, rw_a2, rw_g1, rw_g2, rw_k_k, rw_k_a, rw_r_k, rw_gn_w, rw_gn_b, rw_w_o, gla_w_in, gla_a_w2, gla_a_b, gla_o_norm, gla_w_o, s5_A_re, s5_A_im, s5_log_dt, s5_B_re, s5_B_im, s5_C_re, s5_C_im, s5_D, s5_glu_w, s5_glu_b, final_g):
    bp, t_p, d = x_prompt.shape
    bs, t_s, _ = x_sample.shape
    depth = ada_w.shape[0]
    n_a, n_b, n_c = rw_mu.shape[0], gla_w_in.shape[0], s5_A_re.shape[0]

    n_c_rows = bp + bs
    c_all = _pad_to(jnp.concatenate([c_prompt, c_sample], axis=0), 0, -(-n_c_rows // SUBLANE) * SUBLANE)
    ada = _ada_call(c_all, ada_w, ada_b)
    ada_p = ada[:, :bp].reshape(depth, bp, N_ADA, d)
    ada_s = ada[:, bp:n_c_rows].reshape(depth, bs, N_ADA, d)

    rw = []
    for i in range(n_a):
        vec = lambda a: a[i].reshape(1, 1, d)
        rw.append(dict(
            mu=rw_mu[i], wr=rw_w_rkv[i, 0].astype(bf16), wk=rw_w_rkv[i, 1].astype(bf16), wv=rw_w_rkv[i, 2].astype(bf16),
            w1=_pad_to(rw_w1[i], 1, RW_LORA_PAD).astype(bf16), w2=_pad_to(rw_w2[i], 0, RW_LORA_PAD).astype(bf16),
            a1=_pad_to(rw_a1[i], 1, RW_LORA_PAD).astype(bf16), a2=_pad_to(rw_a2[i], 0, RW_LORA_PAD).astype(bf16),
            g1=rw_g1[i].astype(bf16), g2=rw_g2[i].astype(bf16), w0=vec(rw_w0), a0=vec(rw_a0),
            k_k=vec(rw_k_k), k_a=vec(rw_k_a), r_k=rw_r_k[i].reshape(1, 1, d), gn_w=vec(rw_gn_w), gn_b=vec(rw_gn_b),
            w_o=rw_w_o[i].astype(bf16)))
    gl = []
    for i in range(n_b):
        n_main = gla_w_in.shape[2] - gla_a_w2.shape[1]
        gl.append(dict(
            w_main=gla_w_in[i, :, :n_main].astype(bf16),
            w_gate=_pad_to(gla_w_in[i, :, n_main:], 1, GLA_LORA_PAD).astype(bf16),
            a_w2=_pad_to(gla_a_w2[i], 0, GLA_LORA_PAD).astype(bf16), a_b=gla_a_b[i].reshape(1, 1, -1),
            o_norm=gla_o_norm[i].reshape(1, -1), w_o=gla_w_o[i].astype(bf16)))
    s5 = [_s5_params(s5_A_re[i], s5_A_im[i], s5_log_dt[i], s5_B_re[i], s5_B_im[i], s5_C_re[i], s5_C_im[i],
                     s5_D[i], s5_glu_w[i], s5_glu_b[i]) for i in range(n_c)]
    weights = dict(norm_g=norm_g, ffn_w_in=ffn_w_in.astype(bf16), ffn_w_out=ffn_w_out.astype(bf16),
                   rw=rw, gla=gl, s5=s5, final_g=final_g)

    zeros = lambda s: jnp.zeros((s.shape[0], bp) + s.shape[2:], f32)
    p_states = tuple(zeros(s) for s in (state_rwkv_wkv, state_rwkv_shift, state_gla, state_s5_re, state_s5_im))
    s_states = (state_rwkv_wkv, state_rwkv_shift, state_gla, state_s5_re, state_s5_im)

    tt_p = min(ROW_TILE, t_p)
    nb_s = min(bs, ROW_TILE // t_s)
    rw_chunk = min(RW_HEAD, t_p)
    n_heads = d // RW_HEAD
    y_p, p_wkv, p_shift, p_gla, p_re, p_im = _trunk(x_prompt, ada_p, p_states, weights, 1, tt_p,
                                                     (1, rw_chunk, min(RW_SCAN_TBLK, t_p), min(RW_SCAN_HEADS_PROMPT, n_heads)))
    nb_scan = max(1, min(bs, RW_HEAD // t_s))
    y_s, s_wkv, s_shift, s_gla, s_re, s_im = _trunk(x_sample, ada_s, s_states, weights, nb_s, t_s,
                                                     (nb_scan, t_s, t_s, min(RW_SCAN_HEADS_SAMPLE, n_heads)))
    return (y_p, y_s, p_wkv, p_shift, p_gla, p_re, p_im, s_wkv, s_shift, s_gla, s_re, s_im)
```

```python
import functools
import math

import jax
import jax.numpy as jnp
from jax import lax
from jax.experimental import pallas as pl
from jax.experimental.pallas import tpu as pltpu

f32 = jnp.float32
bf16 = jnp.bfloat16
HI = lax.Precision.HIGHEST

EPS = 1e-6
N_ADA = 9
RW_HEAD = 64
RW_GN_EPS = 64e-5
RW_LORA_PAD = 128
GLA_HEADS = 4
GLA_GATE_NORM = 16.0
GLA_CHUNK = 64
GLA_LORA_PAD = 128
S5_GROUP = 16
S5_STATE = 64
S5_SLAB_GROUPS = 8
LANE = 128
SUBLANE = 8
VMEM_LIMIT_BYTES = 58 * 1024 * 1024

ROW_TILE = 512
FFN_ROWS = 1024
FFN_TILE_F = 512
FFN_EW_ROWS = 256
MM_TILE_N = 512
ADA_TILE_N = 1024
RW_PROJ_ROWS = 512
RW_PROJ_TILE_N = 256
RW_SCAN_TBLK = 256
RW_SCAN_HEADS_PROMPT = 4
RW_SCAN_HEADS_SAMPLE = 8
GLA_SCAN_TBLK = 256
GLA_SCAN_HEADS_PROMPT = 2
GLA_SCAN_HEADS_SAMPLE = 4
S5_TBLK = 128
S5_SEQS_SAMPLE = 8
S5_CARRY_ELEMS = 8192


def _cparams(*sem):
    return pltpu.CompilerParams(dimension_semantics=sem, vmem_limit_bytes=VMEM_LIMIT_BYTES)


def _dot(a, b):
    return jnp.dot(a, b, preferred_element_type=f32)


def _dot_hi(a, b):
    return jnp.dot(a, b, precision=HI, preferred_element_type=f32)


def _dot_nt_hi(a, b):
    return lax.dot_general(a, b, (((1,), (1,)), ((), ())), precision=HI, preferred_element_type=f32)


def _dot_tn_hi(a, b):
    return lax.dot_general(a, b, (((0,), (0,)), ((), ())), precision=HI, preferred_element_type=f32)


def _normmod(x, g, shift, scale):
    ms = jnp.mean(x * x, axis=-1, keepdims=True)
    y = x * lax.rsqrt(ms + EPS) * g
    return y * (1.0 + scale) + shift


def _ada_kernel(c_ref, w_ref, b_ref, o_ref):
    c = c_ref[...]
    s = (c * jax.nn.sigmoid(c)).astype(bf16)
    o_ref[0] = _dot(s, w_ref[0].astype(bf16)) + b_ref[0]


def _ada_call(c_all, ada_w, ada_b):
    depth, d, n = ada_w.shape
    m = c_all.shape[0]
    tn = min(ADA_TILE_N, n)
    return pl.pallas_call(
        _ada_kernel,
        grid=(depth, n // tn),
        in_specs=[pl.BlockSpec((m, d), lambda l, j: (0, 0)),
                  pl.BlockSpec((1, d, tn), lambda l, j: (l, 0, j)),
                  pl.BlockSpec((1, 1, tn), lambda l, j: (l, 0, j))],
        out_specs=pl.BlockSpec((1, m, tn), lambda l, j: (l, 0, j)),
        out_shape=jax.ShapeDtypeStruct((depth, m, n), f32),
        compiler_params=_cparams("arbitrary", "arbitrary"),
        name="ada",
    )(c_all, ada_w, ada_b.reshape(depth, 1, n))


def _ffn_kernel(x_ref, g_ref, sh_ref, sc_ref, gt_ref, wg_ref, wu_ref, wo_ref, o_ref, h_scr):
    j = pl.program_id(1)
    nb, tt, d = x_ref.shape
    if nb > 1:
        step = max(1, min(nb, FFN_EW_ROWS // tt))
        chunks = [(slice(a, a + step), slice(None), a * tt, step * tt) for a in range(0, nb, step)]
    else:
        step = min(tt, FFN_EW_ROWS)
        chunks = [(slice(None), slice(a, a + step), a, step) for a in range(0, tt, step)]

    @pl.when(j == 0)
    def _():
        for s0, s1, r0, nr in chunks:
            h = _normmod(x_ref[s0, s1, :], g_ref[...], _seq_vec(sh_ref), _seq_vec(sc_ref))
            h_scr[r0:r0 + nr, :] = h.reshape(nr, d).astype(bf16)

    hb = h_scr[...]
    gate = _dot(hb, wg_ref[...].astype(bf16))
    up = _dot(hb, wu_ref[...].astype(bf16))
    act = (gate * jax.nn.sigmoid(gate) * up).astype(bf16)
    tn = wg_ref.shape[-1]
    for n0 in range(0, d, tn):
        cols = slice(n0, n0 + tn)
        part = _dot(act, wo_ref[:, cols].astype(bf16)).reshape(nb, tt, tn)

        @pl.when(j == 0)
        def _():
            o_ref[:, :, cols] = part

        @pl.when(j > 0)
        def _():
            o_ref[:, :, cols] += part

    @pl.when(j == pl.num_programs(1) - 1)
    def _():
        for s0, s1, _, _ in chunks:
            o_ref[s0, s1, :] = x_ref[s0, s1, :] + 0.5 * _seq_vec(gt_ref) * o_ref[s0, s1, :]


def _ffn_call(x, g, shift, scale, gate, w_in, w_out, l, s, nb, tt):
    d = x.shape[-1]
    f = w_out.shape[2]
    tf = min(FFN_TILE_F, f)
    nf = f // tf
    til = _row_tiling(x.shape, nb, tt)
    rows = pl.BlockSpec(til.block(d), til.rmap, pipeline_mode=pl.Buffered(1))
    mod = til.mod_spec(d, pipeline_mode=pl.Buffered(1))
    return pl.pallas_call(
        _ffn_kernel,
        grid=(til.grid, nf),
        in_specs=[rows, pl.BlockSpec((1, 1, d), lambda i, j: (0, 0, 0)), mod, mod, mod,
                  pl.BlockSpec((None, None, d, tf), lambda i, j: (l, s, 0, j)),
                  pl.BlockSpec((None, None, d, tf), lambda i, j: (l, s, 0, nf + j)),
                  pl.BlockSpec((None, None, tf, d), lambda i, j: (l, s, j, 0))],
        out_specs=rows,
        out_shape=jax.ShapeDtypeStruct(x.shape, f32),
        scratch_shapes=[pltpu.VMEM((til.rows, d), bf16)],
        compiler_params=_cparams("parallel", "arbitrary"),
        name="ffn",
    )(x, g, _dense_seq_vec(shift, nb), _dense_seq_vec(scale, nb), _dense_seq_vec(gate, nb), w_in, w_in, w_out)


class _RowTiling:
    def __init__(self, shape, nb, tt):
        self.nb = nb
        if nb > 1:
            t, nseq, _ = shape
            assert tt == t and nseq % nb == 0, (shape, nb, tt)
            self.grid, self.rows = nseq // nb, t * nb
            self.block = lambda w: (t, nb, w)
            self.rmap = lambda i, j: (0, i, 0)
            self.cmap = lambda i, j: (0, i, j)
            self.mod_spec = lambda w, **kw: pl.BlockSpec((nb, w), lambda i, j: (i, 0), **kw)
            self.emod_spec = lambda w: pl.BlockSpec((nb, w), lambda i, j: (i, j))
        else:
            nseq, t, _ = shape
            per = t // tt
            assert per * tt == t, (shape, tt)
            self.per = per
            self.grid, self.rows = nseq * per, tt
            self.block = lambda w: (1, tt, w)
            self.rmap = lambda i, j: (i // per, i % per, 0)
            self.cmap = lambda i, j: (i // per, i % per, j)
            self.mod_spec = lambda w, **kw: pl.BlockSpec((1, 1, w), lambda i, j: (i // per, 0, 0), **kw)
            self.emod_spec = lambda w: pl.BlockSpec((1, 1, w), lambda i, j: (i // per, 0, j))


def _row_tiling(shape, nb, tt):
    return _RowTiling(shape, nb, tt)


def _dense_seq_vec(a, nb):
    return a.reshape(a.shape[0], a.shape[2]) if nb > 1 else a


def _seq_vec(ref):
    v = ref[...]
    return v if v.ndim == 3 else v[None, :, :]


def _mm_kernel(*refs, n_row, n_mod, n_vec, n_w, n_erow, n_emod, n_evec, n_out, prologue, epilogue):
    pos = 0

    def take(n):
        nonlocal pos
        out = refs[pos:pos + n]
        pos += n
        return out

    rows, mods, vecs, ws = take(n_row), take(n_mod), take(n_vec), take(n_w)
    erows, emods, evecs, outs = take(n_erow), take(n_emod), take(n_evec), take(n_out)
    a_scr = refs[pos]
    nb, tt, k = rows[0].shape
    tn = ws[0].shape[-1]

    @pl.when(pl.program_id(1) == 0)
    def _():
        a = prologue([r[...] for r in rows], [_seq_vec(m) for m in mods], [v[...] for v in vecs])
        a_scr[...] = a.reshape(nb * tt, k).astype(bf16)

    ab = a_scr[...]
    accs = [_dot(ab, w[...]).reshape(nb, tt, tn) for w in ws]
    res = epilogue(accs, [r[...] for r in erows], [_seq_vec(m) for m in emods], [v[...] for v in evecs])
    for o_ref, o in zip(outs, res):
        o_ref[...] = o


def _mm_call(name, rows, mods, vecs, ws, erows, emods, evecs, n_total, n_out, prologue, epilogue, nb, tt, tn=MM_TILE_N):
    k = rows[0].shape[-1]
    tn = min(tn, n_total)
    assert n_total % tn == 0, (name, n_total, tn)
    nj = n_total // tn
    til = _row_tiling(rows[0].shape, nb, tt)
    mods = [_dense_seq_vec(m, nb) for m in mods]
    emods = [_dense_seq_vec(m, nb) for m in emods]
    in_specs = ([pl.BlockSpec(til.block(k), til.rmap)] * len(rows)
                + [til.mod_spec(k)] * len(mods)
                + [pl.BlockSpec((1, 1, k), lambda i, j: (0, 0, 0))] * len(vecs)
                + [pl.BlockSpec((k, tn), functools.partial(lambda i, j, off: (0, off + j), off=off)) for _, off in ws]
                + [pl.BlockSpec(til.block(tn), til.cmap)] * len(erows)
                + [til.emod_spec(tn)] * len(emods)
                + [pl.BlockSpec((1, 1, tn), lambda i, j: (0, 0, j))] * len(evecs))
    kern = functools.partial(_mm_kernel, n_row=len(rows), n_mod=len(mods), n_vec=len(vecs), n_w=len(ws),
                             n_erow=len(erows), n_emod=len(emods), n_evec=len(evecs), n_out=n_out,
                             prologue=prologue, epilogue=epilogue)
    out = pl.pallas_call(
        kern,
        grid=(til.grid, nj),
        in_specs=in_specs,
        out_specs=[pl.BlockSpec(til.block(tn), til.cmap)] * n_out,
        out_shape=[jax.ShapeDtypeStruct(rows[0].shape[:2] + (n_total,), f32)] * n_out,
        scratch_shapes=[pltpu.VMEM((til.rows, k), bf16)],
        compiler_params=_cparams("parallel", "arbitrary"),
        name=name,
    )(*rows, *mods, *vecs, *[w for w, _ in ws], *erows, *emods, *evecs)
    return out


def _pro_normmod(rows, mods, vecs):
    return _normmod(rows[0], vecs[0], mods[0], mods[1])


def _pro_mul(rows, mods, vecs):
    return rows[0] * rows[1]


def _pro_id(rows, mods, vecs):
    return rows[0]


def _epi_id(accs, erows, emods, evecs):
    return accs


def _epi_residual(accs, erows, emods, evecs):
    return [erows[0] + emods[0] * accs[0]]


def _rms_kernel(x_ref, g_ref, o_ref):
    x = x_ref[...]
    o_ref[...] = x * lax.rsqrt(jnp.mean(x * x, axis=-1, keepdims=True) + EPS) * g_ref[...]


def _final_norm_call(x, g, nb, tt):
    d = x.shape[-1]
    til = _row_tiling(x.shape, nb, tt)
    rmap = lambda i: til.rmap(i, 0)
    return pl.pallas_call(
        _rms_kernel,
        grid=(til.grid,),
        in_specs=[pl.BlockSpec(til.block(d), rmap), pl.BlockSpec((1, 1, d), lambda i: (0, 0, 0))],
        out_specs=pl.BlockSpec(til.block(d), rmap),
        out_shape=jax.ShapeDtypeStruct(x.shape, f32),
        compiler_params=_cparams("parallel"),
        name="final_norm",
    )(x, g)


def _rwproj_kernel(x_ref, xprev_ref, sp_ref, ng_ref, sh_ref, sc_ref, mu_ref, w1_ref, a1_ref, g1_ref,
                   wr_ref, wk_ref, wv_ref, w2_ref, a2_ref, g2_ref, w0_ref, a0_ref,
                   r_ref, k_ref, v_ref, lw_ref, a_ref, g_ref, hl_ref, xm_scr, tw_scr, ta_scr, tg_scr, *, tiles_per_seq,
                   time_major):
    b0, b1, d = x_ref.shape
    rows = b0 * b1
    tn = wr_ref.shape[-1]

    @pl.when(pl.program_id(1) == 0)
    def _():
        ng, sh, sc = ng_ref[...], _seq_vec(sh_ref), _seq_vec(sc_ref)
        h3 = _normmod(x_ref[...], ng, sh, sc)
        first = _seq_vec(sp_ref)
        if time_major:
            hp3 = jnp.concatenate([first, h3[:-1]], axis=0)
            hl_ref[...] = h3[b0 - 1:b0]
        else:
            if tiles_per_seq > 1:
                h_before = _normmod(xprev_ref[...], ng, sh, sc)[:, SUBLANE - 1:SUBLANE, :]
                first = jnp.where(pl.program_id(0) % tiles_per_seq == 0, first, h_before)
            tok = lax.broadcasted_iota(jnp.int32, h3.shape, 1)
            hp3 = jnp.where(tok == 0, first, pltpu.roll(h3, 1, axis=1))
            hl_ref[...] = h3[:, b1 - SUBLANE:, :]
        h = h3.reshape(rows, d)
        dlt = hp3.reshape(rows, d) - h
        mu = mu_ref[...]
        for p in range(3):
            xm_scr[p] = (h + dlt * mu[p:p + 1]).astype(bf16)
        xw = (h + dlt * mu[3:4]).astype(bf16)
        tw_scr[...] = jnp.tanh(_dot(xw, w1_ref[...])).astype(bf16)
        xa = (h + dlt * mu[4:5]).astype(bf16)
        ta_scr[...] = _dot(xa, a1_ref[...]).astype(bf16)
        xg = (h + dlt * mu[5:6]).astype(bf16)
        tg_scr[...] = jax.nn.sigmoid(_dot(xg, g1_ref[...])).astype(bf16)

    shp = (b0, b1, tn)
    r_ref[...] = _dot(xm_scr[0], wr_ref[...]).reshape(shp)
    k_ref[...] = _dot(xm_scr[1], wk_ref[...]).reshape(shp)
    v_ref[...] = _dot(xm_scr[2], wv_ref[...]).reshape(shp)
    w_log = -jax.nn.softplus(-(w0_ref[0] + _dot(tw_scr[...], w2_ref[...]))) - 0.5
    lw_ref[...] = (-jnp.exp(w_log)).reshape(shp)
    a_ref[...] = jax.nn.sigmoid(a0_ref[0] + _dot(ta_scr[...], a2_ref[...])).reshape(shp)
    g_ref[...] = _dot(tg_scr[...], g2_ref[...]).reshape(shp)


def _rwproj_call(x, shift_prev, ng, shift, scale, mu, w1, a1, g1, wr, wk, wv, w2, a2, g2, w0, a0, nb, tt):
    d = x.shape[-1]
    tn = min(RW_PROJ_TILE_N, d)
    til = _row_tiling(x.shape, nb, tt)
    if nb > 1:
        per = 1
        pmap = til.rmap
        prev_block = til.block(d)
        tail_spec = pl.BlockSpec((1, nb, d), til.rmap)
        tail_shape = (1, x.shape[1], d)
    else:
        per = til.per
        pmap = lambda i, j: (i // per, jnp.maximum((i % per) * (tt // SUBLANE) - 1, 0), 0)
        prev_block = (1, SUBLANE, d)
        tail_spec = pl.BlockSpec((1, SUBLANE, d), til.rmap)
        tail_shape = (x.shape[0], per * SUBLANE, d)
    mod = til.mod_spec(d)
    full = lambda a: pl.BlockSpec(a.shape, lambda i, j: (0,) * a.ndim)
    col = lambda a: pl.BlockSpec((a.shape[0], tn), lambda i, j: (0, j))
    shift_prev, shift, scale = (_dense_seq_vec(a, nb) for a in (shift_prev, shift, scale))
    rows = til.rows
    kern = functools.partial(_rwproj_kernel, tiles_per_seq=per, time_major=nb > 1)
    return pl.pallas_call(
        kern,
        grid=(til.grid, d // tn),
        in_specs=[pl.BlockSpec(til.block(d), til.rmap), pl.BlockSpec(prev_block, pmap), mod,
                  pl.BlockSpec((1, 1, d), lambda i, j: (0, 0, 0)), mod, mod, full(mu), full(w1), full(a1), full(g1),
                  col(wr), col(wk), col(wv), col(w2), col(a2), col(g2),
                  pl.BlockSpec((1, 1, tn), lambda i, j: (0, 0, j)), pl.BlockSpec((1, 1, tn), lambda i, j: (0, 0, j))],
        out_specs=[pl.BlockSpec(til.block(tn), til.cmap)] * 6 + [tail_spec],
        out_shape=[jax.ShapeDtypeStruct(x.shape, f32)] * 6 + [jax.ShapeDtypeStruct(tail_shape, f32)],
        scratch_shapes=[pltpu.VMEM((3, rows, d), bf16), pltpu.VMEM((rows, w1.shape[1]), bf16),
                        pltpu.VMEM((rows, a1.shape[1]), bf16), pltpu.VMEM((rows, g1.shape[1]), bf16)],
        compiler_params=_cparams("parallel", "arbitrary"),
        name="rwkv_proj",
    )(x, x, shift_prev, ng, shift, scale, mu, w1, a1, g1, wr, wk, wv, w2, a2, g2, w0, a0)


def _dot_nt(a, b):
    return lax.dot_general(a, b, (((1,), (1,)), ((), ())), preferred_element_type=f32)


def _dot_tn(a, b):
    return lax.dot_general(a, b, (((0,), (0,)), ((), ())), preferred_element_type=f32)


def _rwkv_chunk_kernel(r_ref, k_ref, v_ref, lw_ref, a_ref, kk_ref, ka_ref, rk_ref, gw_ref, gb_ref, s0_ref,
                       y_ref, sout_ref, s_scr, rp_scr, y0_scr, bon_scr, m_scr, n_scr, *, nb, tt, n_inner):
    c = pl.program_id(2)
    n = RW_HEAD
    big = nb * tt
    n_dbl = max(1, math.ceil(math.log2(tt)))

    @pl.when(c == 0)
    def _():
        s_scr[...] = s0_ref[...]

    ti = lax.broadcasted_iota(jnp.int32, (big, big), 0)
    si = lax.broadcasted_iota(jnp.int32, (big, big), 1)
    same = (ti // tt) == (si // tt)
    incl = jnp.logical_and(same, si <= ti)
    strict = jnp.logical_and(same, si < ti)
    cumsum = _make_masked_sum(lambda t_, s_: jnp.logical_and((t_ // tt) == (s_ // tt), s_ <= t_), big)
    seqsum = _make_masked_sum(lambda t_, s_: (t_ // tt) == (s_ // tt), big)
    eye_f = (ti == si).astype(f32)
    eye_n = (lax.broadcasted_iota(jnp.int32, (n, n), 0) == lax.broadcasted_iota(jnp.int32, (n, n), 1)).astype(f32)

    kkp, kap, rkp, gwp, gbp = kk_ref[0], ka_ref[0], rk_ref[0], gw_ref[0], gb_ref[0]

    hp = r_ref.shape[-1] // n

    def load(ref, ci):
        if nb > 1:
            return ref[...].reshape(big, hp * n)
        return ref[0, ci * big:(ci + 1) * big, :]

    ch = []
    for ci in range(n_inner):
        r, k, v, lw, a = (load(ref, ci) for ref in (r_ref, k_ref, v_ref, lw_ref, a_ref))
        cum = cumsum(lw)
        tot = cum[big - 1:big, :] if nb == 1 else seqsum(lw)
        g_in = jnp.exp(cum)
        g_prev = jnp.exp(cum - lw)
        g_inv = jnp.exp(-cum)
        g_rest = jnp.exp(tot - cum)
        g_tot = jnp.exp(tot)
        kk = k * kkp
        k2 = k * (1.0 + (a - 1.0) * kap)
        for hh in range(hp):
            sl = slice(hh * n, (hh + 1) * n)
            kk_h = kk[:, sl]
            kk_h = kk_h * jnp.minimum(lax.rsqrt(jnp.sum(kk_h * kk_h, axis=1, keepdims=True)), 1e12)
            b_h = kk_h * a[:, sl]
            r_h, k_h, v_h = r[:, sl], k2[:, sl], v[:, sl]
            rt = r_h * g_in[:, sl]
            at = -kk_h * g_prev[:, sl]
            bon_scr[ci, hh] = jnp.sum(r_h * k_h * rkp[:, sl], axis=1, keepdims=True) * v_h
            ch.append(dict(
                ci=ci, hh=hh, rt=rt, at=at, v=v_h, vb=v_h.astype(bf16), gt=g_tot[:, sl],
                bhat=b_h * g_rest[:, sl], khat=k_h * g_rest[:, sl],
                ra=jnp.concatenate([rt, at], axis=0).astype(bf16),
                bk=jnp.concatenate([b_h * g_inv[:, sl], k_h * g_inv[:, sl]], axis=0).astype(bf16)))
    for q in ch:
        q["amat"] = _dot_nt(q["ra"], q["bk"])
    for q in ch:
        amat = q["amat"]
        q["a_rb"] = jnp.where(incl, amat[:big, :big], 0.0).astype(bf16)
        q["a_rk"] = jnp.where(incl, amat[:big, big:], 0.0).astype(bf16)
        q["pw"] = jnp.where(strict, amat[big:, :big], 0.0)
        q["a_ak"] = jnp.where(strict, amat[big:, big:], 0.0).astype(bf16)
        q["tm"] = eye_f + q["pw"]
    for _ in range(n_dbl - 1):
        for q in ch:
            pwb = q["pw"].astype(bf16)
            q["pw"] = _dot(pwb, pwb)
        for q in ch:
            q["tm"] = q["tm"] + _dot(q["tm"].astype(bf16), q["pw"].astype(bf16))
    for q in ch:
        q["akv"] = _dot(q["a_ak"], q["vb"])
    for q in ch:
        q["wa"] = _dot(q["tm"].astype(bf16), jnp.concatenate([q["akv"], q["at"]], axis=1).astype(bf16))
    for q in ch:
        q["aw"] = _dot(q["a_rb"], q["wa"].astype(bf16))
        q["arkv"] = _dot(q["a_rk"], q["vb"])
    for q in ch:
        ci, hh, wa = q["ci"], q["hh"], q["wa"]
        rp_scr[ci, hh] = q["rt"] + q["aw"][:, n:]
        y0_scr[ci, hh] = q["aw"][:, :n] + q["arkv"]
        for i in range(nb):
            rows = slice(i * tt, (i + 1) * tt)
            bh_i = q["bhat"][rows].astype(bf16)
            m_scr[ci, hh, i] = eye_n * q["gt"][i * tt:i * tt + 1] + _dot_tn(wa[rows, n:].astype(bf16), bh_i)
            n_scr[ci, hh, i] = _dot_tn(jnp.concatenate([wa[rows, :n], q["v"][rows]], axis=0).astype(bf16),
                                       jnp.concatenate([q["bhat"][rows], q["khat"][rows]], axis=0).astype(bf16))

    for ci in range(n_inner):
        ys = []
        for hh in range(hp):
            sl = slice(hh * n, (hh + 1) * n)
            rp = rp_scr[ci, hh]
            parts = []
            for i in range(nb):
                sb = s_scr[i, hh].astype(bf16)
                parts.append(_dot_nt(rp[i * tt:(i + 1) * tt].astype(bf16), sb))
                s_scr[i, hh] = _dot(sb, m_scr[ci, hh, i].astype(bf16)) + n_scr[ci, hh, i]
            y = (parts[0] if nb == 1 else jnp.concatenate(parts, axis=0)) + y0_scr[ci, hh]
            mean = jnp.mean(y, axis=1, keepdims=True)
            var = jnp.mean(jnp.square(y - mean), axis=1, keepdims=True)
            ys.append((y - mean) * lax.rsqrt(var + RW_GN_EPS) * gwp[:, sl] + gbp[:, sl] + bon_scr[ci, hh])
        yo = jnp.concatenate(ys, axis=1)
        if nb > 1:
            y_ref[...] = yo.reshape(nb, tt, hp * n)
        else:
            y_ref[0, ci * big:(ci + 1) * big, :] = yo

    @pl.when(c == pl.num_programs(2) - 1)
    def _():
        sout_ref[...] = s_scr[...]


def _rwscan_call(r, k, v, lw, a, k_k, k_a, r_k, gn_w, gn_b, s0_all, layer, nb, tt, tblk, hp):
    nseq, t, d = r.shape
    nh = d // RW_HEAD
    lanes = hp * RW_HEAD
    nc = t // tblk
    n_inner = tblk // tt if nb == 1 else 1
    xspec = pl.BlockSpec((nb, tblk, lanes), lambda i, p, c: (i, c, p))
    pspec = pl.BlockSpec((1, 1, lanes), lambda i, p, c: (0, 0, p))
    sspec = pl.BlockSpec((nb, hp, RW_HEAD, RW_HEAD), lambda i, p, c: (i, p, 0, 0))
    kern = functools.partial(_rwkv_chunk_kernel, nb=nb, tt=tt, n_inner=n_inner)
    chunk_rows = pltpu.VMEM((n_inner, hp, nb * tt, RW_HEAD), f32)
    chunk_mats = pltpu.VMEM((n_inner, hp, nb, RW_HEAD, RW_HEAD), f32)
    return pl.pallas_call(
        kern,
        grid=(nseq // nb, nh // hp, nc),
        in_specs=[xspec] * 5 + [pspec] * 5
                 + [pl.BlockSpec((None, nb, hp, RW_HEAD, RW_HEAD), lambda i, p, c: (layer, i, p, 0, 0))],
        out_specs=[xspec, sspec],
        out_shape=[jax.ShapeDtypeStruct((nseq, t, d), f32), jax.ShapeDtypeStruct(s0_all.shape[1:], f32)],
        scratch_shapes=[pltpu.VMEM((nb, hp, RW_HEAD, RW_HEAD), f32), chunk_rows, chunk_rows, chunk_rows,
                        chunk_mats, chunk_mats],
        compiler_params=_cparams("parallel", "parallel", "arbitrary"),
        name="rwkv_scan",
    )(r, k, v, lw, a, k_k, k_a, r_k, gn_w, gn_b, s0_all)


def _split3(x):
    hi = x.astype(bf16)
    r1 = x - hi.astype(f32)
    mid = r1.astype(bf16)
    return hi, mid, (r1 - mid.astype(f32)).astype(bf16)


def _make_masked_sum(mask_fn, rows):
    if (3 * rows) % 16:
        ti = lax.broadcasted_iota(jnp.int32, (rows, rows), 0)
        si = lax.broadcasted_iota(jnp.int32, (rows, rows), 1)
        mask_f = mask_fn(ti, si).astype(f32)
        return lambda x: _dot_hi(mask_f, x)
    ti = lax.broadcasted_iota(jnp.int32, (rows, 3 * rows), 0)
    si = lax.broadcasted_iota(jnp.int32, (rows, 3 * rows), 1) % rows
    mask3 = jnp.where(mask_fn(ti, si), 1.0, 0.0).astype(bf16)
    return lambda x: _dot(mask3, jnp.concatenate(_split3(x), axis=0))


def _glascan_kernel(q_ref, k_ref, v_ref, r_ref, la_ref, on_ref, s0_ref, o_ref, sout_ref, s_scr, *, chunk, n_inner, hb):
    c = pl.program_id(2)
    hk = q_ref.shape[-1] // hb
    hv = v_ref.shape[-1] // hb

    @pl.when(c == 0)
    def _():
        s_scr[...] = s0_ref[0]

    ti = lax.broadcasted_iota(jnp.int32, (chunk, chunk), 0)
    si = lax.broadcasted_iota(jnp.int32, (chunk, chunk), 1)
    causal = si <= ti
    cumsum = _make_masked_sum(lambda t_, s_: s_ <= t_, chunk)

    ch = []
    for ci in range(n_inner):
        rows = slice(ci * chunk, (ci + 1) * chunk)
        for h in range(hb):
            ks, vs = slice(h * hk, (h + 1) * hk), slice(h * hv, (h + 1) * hv)
            ch.append(dict(rows=rows, h=h, vs=vs, q=q_ref[0, rows, ks], k=k_ref[0, rows, ks],
                           vb=v_ref[0, rows, vs].astype(bf16), la=la_ref[0, rows, ks]))
    for z in ch:
        z["b"] = cumsum(z["la"])
    for z in ch:
        b = z["b"]
        b_last = b[chunk - 1:chunk, :]
        z["q_in"] = (z["q"] * (hk ** -0.5) * jnp.exp(b)).astype(bf16)
        z["k_in"] = (z["k"] * jnp.exp(-b)).astype(bf16)
        z["kd"] = (z["k"] * jnp.exp(b_last - b)).astype(bf16)
        z["dec"] = jnp.transpose(jnp.broadcast_to(jnp.exp(b_last), (SUBLANE, hk)))[:, 0:1]
    for z in ch:
        z["att"] = _dot_nt(z["q_in"], z["k_in"])
    for z in ch:
        z["ov"] = _dot(jnp.where(causal, z["att"], 0.0).astype(bf16), z["vb"])
        z["inc"] = _dot_tn(z["kd"], z["vb"])

    for z in ch:
        h, rows, vs = z["h"], z["rows"], z["vs"]
        s = s_scr[h]
        o = z["ov"] + _dot(z["q_in"], s.astype(bf16))
        s_scr[h] = s * z["dec"] + z["inc"]
        o = o * lax.rsqrt(jnp.mean(o * o, axis=-1, keepdims=True) + EPS) * on_ref[...]
        rr = r_ref[0, rows, vs]
        o_ref[0, rows, vs] = o * (rr * jax.nn.sigmoid(rr))

    @pl.when(c == pl.num_programs(2) - 1)
    def _():
        sout_ref[0] = s_scr[...]


def _glascan_call(qkvr, la, o_norm, s0, tblk, hb):
    nseq, t, _ = qkvr.shape
    _, gh, hk, hv = s0.shape
    chunk = math.gcd(t, GLA_CHUNK)
    tblk = min(tblk, t)
    k_off = (gh * hk) // (hb * hk)
    v_off = (2 * gh * hk) // (hb * hv)
    r_off = (2 * gh * hk + gh * hv) // (hb * hv)
    kern = functools.partial(_glascan_kernel, chunk=chunk, n_inner=tblk // chunk, hb=hb)
    return pl.pallas_call(
        kern,
        grid=(nseq, gh // hb, t // tblk),
        in_specs=[pl.BlockSpec((1, tblk, hb * hk), lambda b, h, c: (b, c, h)),
                  pl.BlockSpec((1, tblk, hb * hk), lambda b, h, c: (b, c, k_off + h)),
                  pl.BlockSpec((1, tblk, hb * hv), lambda b, h, c: (b, c, v_off + h)),
                  pl.BlockSpec((1, tblk, hb * hv), lambda b, h, c: (b, c, r_off + h)),
                  pl.BlockSpec((1, tblk, hb * hk), lambda b, h, c: (b, c, h)),
                  pl.BlockSpec((1, hv), lambda b, h, c: (0, 0)),
                  pl.BlockSpec((1, hb, hk, hv), lambda b, h, c: (b, h, 0, 0))],
        out_specs=[pl.BlockSpec((1, tblk, hb * hv), lambda b, h, c: (b, c, h)),
                   pl.BlockSpec((1, hb, hk, hv), lambda b, h, c: (b, h, 0, 0))],
        out_shape=[jax.ShapeDtypeStruct((nseq, t, gh * hv), f32), jax.ShapeDtypeStruct(s0.shape, f32)],
        scratch_shapes=[pltpu.VMEM((hb, hk, hv), f32)],
        compiler_params=_cparams("parallel", "parallel", "arbitrary"),
        name="gla_scan",
    )(qkvr, qkvr, qkvr, qkvr, la, o_norm, s0)


def _s5prep_kernel(are_ref, aim_ref, ldt_ref, bre_ref, bim_ref, abre_ref, abim_ref, bbre_ref, bbim_ref):
    a_re, a_im = are_ref[...], aim_ref[...]
    dt = jnp.exp(ldt_ref[...])
    mag = jnp.exp(a_re * dt)
    ab_re, ab_im = mag * jnp.cos(a_im * dt), mag * jnp.sin(a_im * dt)
    den = a_re * a_re + a_im * a_im
    nr = ab_re - 1.0
    cf_re = (nr * a_re + ab_im * a_im) / den
    cf_im = (ab_im * a_re - nr * a_im) / den
    abre_ref[...] = ab_re
    abim_ref[...] = ab_im
    b_re, b_im = bre_ref[...], bim_ref[...]
    bbre_ref[...] = cf_re * b_re - cf_im * b_im
    bbim_ref[...] = cf_re * b_im + cf_im * b_re


def _s5prep_call(a_re, a_im, log_dt, b_re_t, b_im_t):
    g, n = a_re.shape
    return pl.pallas_call(
        _s5prep_kernel,
        out_shape=[jax.ShapeDtypeStruct((g, 1, n), f32)] * 2 + [jax.ShapeDtypeStruct(b_re_t.shape, f32)] * 2,
        name="s5_prep",
    )(a_re.reshape(g, 1, n), a_im.reshape(g, 1, n), log_dt.reshape(g, 1, 1), b_re_t, b_im_t)


def _s5scan_kernel(x_ref, g_ref, sh_ref, sc_ref, wbu_ref, wc_ref, abre_ref, abim_ref, dsk_ref, x0re_ref, x0im_ref,
                   y_ref, xre_ref, xim_ref, bre_scr, bim_scr, sre_scr, sim_scr, *, lane_chunk, nb):
    c = pl.program_id(1)
    d = x_ref.shape[-1]
    rows = x_ref.shape[0] * x_ref.shape[1]
    tblk = rows // nb
    n_slab = wbu_ref.shape[0]
    half = wbu_ref.shape[2] // 2
    n_state = n_slab * half

    @pl.when(c == 0)
    def _():
        sre_scr[...] = x0re_ref[0]
        sim_scr[...] = x0im_ref[0]

    h = _normmod(x_ref[...], g_ref[...], sh_ref[...], sc_ref[...]).reshape(rows, d)
    hb = h.astype(bf16)
    for s in range(n_slab):
        bu = _dot(hb[:, s * LANE:(s + 1) * LANE], wbu_ref[s])
        bre_scr[:, s * half:(s + 1) * half] = bu[:, :half]
        bim_scr[:, s * half:(s + 1) * half] = bu[:, half:]

    for q in range(n_state // lane_chunk):
        ls = slice(q * lane_chunk, (q + 1) * lane_chunk)
        ar, ai = abre_ref[:, ls], abim_ref[:, ls]

        def step(t, carry):
            xr, xi = carry
            row = pl.ds(pl.multiple_of(t * nb, nb), nb)
            nxr = ar * xr - ai * xi + bre_scr[row, ls]
            nxi = ar * xi + ai * xr + bim_scr[row, ls]
            bre_scr[row, ls] = nxr
            bim_scr[row, ls] = nxi
            return nxr, nxi

        xr, xi = lax.fori_loop(0, tblk, step, (sre_scr[:, ls], sim_scr[:, ls]))
        sre_scr[:, ls] = xr
        sim_scr[:, ls] = xi

    for s in range(n_slab):
        xs = jnp.concatenate([bre_scr[:, s * half:(s + 1) * half], bim_scr[:, s * half:(s + 1) * half]], axis=1)
        ys = _dot(xs.astype(bf16), wc_ref[s])
        sl = slice(s * LANE, (s + 1) * LANE)
        y_ref[:, :, sl] = jax.nn.gelu(ys + dsk_ref[:, sl] * h[:, sl]).reshape(y_ref.shape[0], y_ref.shape[1], LANE)

    @pl.when(c == pl.num_programs(1) - 1)
    def _():
        xre_ref[0] = sre_scr[...]
        xim_ref[0] = sim_scr[...]


def _s5scan_call(x, g, shift, scale, wbu, wc, ab_re, ab_im, d_skip, x0_re, x0_im, tblk, nb):
    if nb > 1:
        t, nseq, d = x.shape
        tblk = t
        shift, scale = shift.reshape(1, nseq, d), scale.reshape(1, nseq, d)
        xspec = pl.BlockSpec((t, nb, d), lambda b, c: (0, b, 0))
        mod = pl.BlockSpec((1, nb, d), lambda b, c: (0, b, 0))
    else:
        nseq, t, d = x.shape
        tblk = min(tblk, t)
        xspec = pl.BlockSpec((1, tblk, d), lambda b, c: (b, c, 0))
        mod = pl.BlockSpec((1, 1, d), lambda b, c: (b, 0, 0))
    n_state = ab_re.shape[1]
    full = lambda a: pl.BlockSpec(a.shape, lambda b, c: (0,) * a.ndim)
    st = pl.BlockSpec((1, nb, n_state), lambda b, c: (b, 0, 0))
    st_shape = jax.ShapeDtypeStruct((nseq // nb, nb, n_state), f32)
    kern = functools.partial(_s5scan_kernel, lane_chunk=min(n_state, S5_CARRY_ELEMS // nb), nb=nb)
    rows = nb * tblk
    return pl.pallas_call(
        kern,
        grid=(nseq // nb, t // tblk),
        in_specs=[xspec, pl.BlockSpec((1, 1, d), lambda b, c: (0, 0, 0)),
                  mod, mod, full(wbu), full(wc), full(ab_re), full(ab_im), full(d_skip), st, st],
        out_specs=[xspec, st, st],
        out_shape=[jax.ShapeDtypeStruct(x.shape, f32), st_shape, st_shape],
        scratch_shapes=[pltpu.VMEM((rows, n_state), f32), pltpu.VMEM((rows, n_state), f32),
                        pltpu.VMEM((nb, n_state), f32), pltpu.VMEM((nb, n_state), f32)],
        compiler_params=_cparams("parallel", "arbitrary"),
        name="s5_scan",
    )(x, g, shift, scale, wbu, wc, ab_re, ab_im, d_skip,
      x0_re.reshape(nseq // nb, nb, n_state), x0_im.reshape(nseq // nb, nb, n_state))


def _epi_glu(accs, erows, emods, evecs):
    val = accs[0] + evecs[0]
    gate = accs[1] + evecs[1]
    return [erows[0] + emods[0] * (val * jax.nn.sigmoid(gate))]


def _rwkv_mixer(x, g, shift, scale, gm, shift_prev, wkv_all, layer, p, nb, tt, scan_cfg):
    t, d = (x.shape[0] if nb > 1 else x.shape[1]), x.shape[-1]
    pnb, ptt = (RW_PROJ_ROWS // t, t) if nb > 1 else (1, min(RW_PROJ_ROWS, t))
    r, k, v, lw, a, gg, h_tail = _rwproj_call(x, shift_prev[:, None, :], g, shift, scale, p["mu"], p["w1"], p["a1"], p["g1"],
                                               p["wr"], p["wk"], p["wv"], p["w2"], p["a2"], p["g2"], p["w0"], p["a0"], pnb, ptt)
    seq_major = (lambda z: z.transpose(1, 0, 2)) if nb > 1 else (lambda z: z)
    y, wkv = _rwscan_call(*(seq_major(z) for z in (r, k, v, lw, a)), p["k_k"], p["k_a"], p["r_k"], p["gn_w"], p["gn_b"],
                          wkv_all, layer, *scan_cfg)
    (x_new,) = _mm_call("rwkv_out", [seq_major(y), gg], [], [], [(p["w_o"], 0)], [x], [gm], [], d, 1,
                        _pro_mul, _epi_residual, nb, tt)
    return x_new, (h_tail[0] if nb > 1 else h_tail[:, -1]), wkv


def _gla_mixer(x, g, shift, scale, gm, s0, p, nb, tt):
    d = x.shape[-1]
    n_main = p["w_main"].shape[1]
    (qkvr,) = _mm_call("gla_in", [x], [shift, scale], [g], [(p["w_main"], 0)], [], [], [], n_main, 1,
                       _pro_normmod, _epi_id, nb, tt)
    (a_low,) = _mm_call("gla_gate_in", [x], [shift, scale], [g], [(p["w_gate"], 0)], [], [], [], GLA_LORA_PAD, 1,
                        _pro_normmod, _epi_id, nb, tt)

    def epi_la(accs, erows, emods, evecs):
        return [jax.nn.log_sigmoid(accs[0] + evecs[0]) / GLA_GATE_NORM]

    dk = p["a_w2"].shape[1]
    (la,) = _mm_call("gla_gate", [a_low], [], [], [(p["a_w2"], 0)], [], [], [p["a_b"]], dk, 1, _pro_id, epi_la, nb, tt)
    hb = min(s0.shape[1], GLA_SCAN_HEADS_SAMPLE if nb > 1 else GLA_SCAN_HEADS_PROMPT)
    seq_major = (lambda z: z.transpose(1, 0, 2)) if nb > 1 else (lambda z: z)
    o, s_new = _glascan_call(seq_major(qkvr), seq_major(la), p["o_norm"], s0, GLA_SCAN_TBLK, hb)
    (x_new,) = _mm_call("gla_out", [seq_major(o)], [], [], [(p["w_o"], 0)], [x], [gm], [], d, 1,
                        _pro_id, _epi_residual, nb, tt)
    return x_new, s_new


def _s5_mixer(x, g, shift, scale, gm, x0_re, x0_im, p, nb, tt):
    d = x.shape[-1]
    s5_nb = min(x.shape[1], S5_SEQS_SAMPLE) if nb > 1 else 1
    yg, xre, xim = _s5scan_call(x, g, shift, scale, p["wbu"], p["wc"], p["ab_re"], p["ab_im"], p["d_skip"],
                                x0_re, x0_im, S5_TBLK, s5_nb)
    (x_new,) = _mm_call("s5_glu", [yg], [], [], [(p["glu_w"], 0), (p["glu_w"], d // min(MM_TILE_N, d))], [x], [gm],
                        [p["glu_b_val"], p["glu_b_gate"]], d, 1, _pro_id, _epi_glu, nb, tt)
    return x_new, xre.reshape(x0_re.shape), xim.reshape(x0_im.shape)


def _s5_params(a_re, a_im, log_dt, b_re, b_im, c_re, c_im, d_skip, glu_w, glu_b):
    g, n, cg = b_re.shape
    d = g * cg
    ab_re, ab_im, bb_re, bb_im = _s5prep_call(a_re, a_im, log_dt, b_re.transpose(0, 2, 1), b_im.transpose(0, 2, 1))
    sg = S5_SLAB_GROUPS
    eye = jnp.eye(sg, dtype=f32)

    def bdiag(m):
        gq, rr, cc = m.shape
        m = m.reshape(gq // sg, sg, rr, cc)
        return jnp.einsum("sjrc,jk->sjrkc", m, eye).reshape(gq // sg, sg * rr, sg * cc)

    wbu = jnp.concatenate([bdiag(bb_re), bdiag(bb_im)], axis=2).astype(bf16)
    wc = jnp.concatenate([bdiag(c_re.transpose(0, 2, 1)), bdiag(-c_im.transpose(0, 2, 1))], axis=1).astype(bf16)
    return dict(wbu=wbu, wc=wc, ab_re=ab_re.reshape(1, g * n), ab_im=ab_im.reshape(1, g * n),
                d_skip=d_skip.reshape(1, d), glu_w=glu_w.astype(bf16),
                glu_b_val=glu_b[:d].reshape(1, 1, d), glu_b_gate=glu_b[d:].reshape(1, 1, d))


def _trunk(x, ada, states, weights, nb, tt, rw_scan_cfg, ffn_cfg):
    d = x.shape[-1]
    depth, nseq = ada.shape[:2]
    wkv, shift, gla, s5_re, s5_im = states
    new = dict(wkv=[], shift=[], gla=[], re=[], im=[])
    ia = ib = ic = 0
    vec = lambda l, j: ada[l, :, j, :].reshape(nseq, 1, d)
    for l in range(depth):
        ng = lambda s: weights["norm_g"][l, s].reshape(1, 1, d)
        x = _ffn_call(x, ng(0), vec(l, 0), vec(l, 1), vec(l, 2), weights["ffn_w_in"], weights["ffn_w_out"], l, 0, *ffn_cfg)
        kind = l % 3
        if kind == 0:
            x, sh, st = _rwkv_mixer(x, ng(1), vec(l, 3), vec(l, 4), vec(l, 5), shift[ia], wkv, ia, weights["rw"][ia],
                                    nb, tt, rw_scan_cfg)
            new["shift"].append(sh)
            new["wkv"].append(st)
            ia += 1
        elif kind == 1:
            x, st = _gla_mixer(x, ng(1), vec(l, 3), vec(l, 4), vec(l, 5), gla[ib], weights["gla"][ib], nb, tt)
            new["gla"].append(st)
            ib += 1
        else:
            x, sr, si = _s5_mixer(x, ng(1), vec(l, 3), vec(l, 4), vec(l, 5), s5_re[ic], s5_im[ic], weights["s5"][ic], nb, tt)
            new["re"].append(sr)
            new["im"].append(si)
            ic += 1
        x = _ffn_call(x, ng(2), vec(l, 6), vec(l, 7), vec(l, 8), weights["ffn_w_in"], weights["ffn_w_out"], l, 1, *ffn_cfg)
    y = _final_norm_call(x, weights["final_g"].reshape(1, 1, d), nb, tt)
    return (y, jnp.stack(new["wkv"]), jnp.stack(new["shift"]), jnp.stack(new["gla"]),
            jnp.stack(new["re"]), jnp.stack(new["im"]))


def _pad_to(a, axis, size):
    pad = [(0, 0)] * a.ndim
    pad[axis] = (0, size - a.shape[axis])
    return jnp.pad(a, pad)


def kernel(x_prompt, x_sample, state_rwkv_wkv, state_rwkv_shift, state_gla, state_s5_re, state_s5_im, c_prompt, c_sample, norm_g, ada_w, ada_b, ffn_w_in, ffn_w_out, rw_mu, rw_w_rkv, rw_w0, rw_w1, rw_w2, rw_a0, rw_a1, rw_a2, rw_g1, rw_g2, rw_k_k, rw_k_a, rw_r_k, rw_gn_w, rw_gn_b, rw_w_o, gla_w_in, gla_a_w2, gla_a_b, gla_o_norm, gla_w_o, s5_A_re, s5_A_im, s5_log_dt, s5_B_re, s5_B_im, s5_C_re, s5_C_im, s5_D, s5_glu_w, s5_glu_b, final_g):
    bp, t_p, d = x_prompt.shape
    bs, t_s, _ = x_sample.shape
    depth = ada_w.shape[0]
    n_a, n_b, n_c = rw_mu.shape[0], gla_w_in.shape[0], s5_A_re.shape[0]

    n_c_rows = bp + bs
    c_all = _pad_to(jnp.concatenate([c_prompt, c_sample], axis=0), 0, -(-n_c_rows // SUBLANE) * SUBLANE)
    ada = _ada_call(c_all, ada_w, ada_b)
    ada_p = ada[:, :bp].reshape(depth, bp, N_ADA, d)
    ada_s = ada[:, bp:n_c_rows].reshape(depth, bs, N_ADA, d)

    rw = []
    for i in range(n_a):
        vec = lambda a: a[i].reshape(1, 1, d)
        rw.append(dict(
            mu=rw_mu[i], wr=rw_w_rkv[i, 0].astype(bf16), wk=rw_w_rkv[i, 1].astype(bf16), wv=rw_w_rkv[i, 2].astype(bf16),
            w1=_pad_to(rw_w1[i], 1, RW_LORA_PAD).astype(bf16), w2=_pad_to(rw_w2[i], 0, RW_LORA_PAD).astype(bf16),
            a1=_pad_to(rw_a1[i], 1, RW_LORA_PAD).astype(bf16), a2=_pad_to(rw_a2[i], 0, RW_LORA_PAD).astype(bf16),
            g1=rw_g1[i].astype(bf16), g2=rw_g2[i].astype(bf16), w0=vec(rw_w0), a0=vec(rw_a0),
            k_k=vec(rw_k_k), k_a=vec(rw_k_a), r_k=rw_r_k[i].reshape(1, 1, d), gn_w=vec(rw_gn_w), gn_b=vec(rw_gn_b),
            w_o=rw_w_o[i].astype(bf16)))
    gl = []
    for i in range(n_b):
        n_main = gla_w_in.shape[2] - gla_a_w2.shape[1]
        gl.append(dict(
            w_main=gla_w_in[i, :, :n_main].astype(bf16),
            w_gate=_pad_to(gla_w_in[i, :, n_main:], 1, GLA_LORA_PAD).astype(bf16),
            a_w2=_pad_to(gla_a_w2[i], 0, GLA_LORA_PAD).astype(bf16), a_b=gla_a_b[i].reshape(1, 1, -1),
            o_norm=gla_o_norm[i].reshape(1, -1), w_o=gla_w_o[i].astype(bf16)))
    s5 = [_s5_params(s5_A_re[i], s5_A_im[i], s5_log_dt[i], s5_B_re[i], s5_B_im[i], s5_C_re[i], s5_C_im[i],
                     s5_D[i], s5_glu_w[i], s5_glu_b[i]) for i in range(n_c)]
    weights = dict(norm_g=norm_g, ffn_w_in=ffn_w_in, ffn_w_out=ffn_w_out,
                   rw=rw, gla=gl, s5=s5, final_g=final_g)

    zeros = lambda s: jnp.zeros((s.shape[0], bp) + s.shape[2:], f32)
    p_states = tuple(zeros(s) for s in (state_rwkv_wkv, state_rwkv_shift, state_gla, state_s5_re, state_s5_im))
    s_states = (state_rwkv_wkv, state_rwkv_shift, state_gla, state_s5_re, state_s5_im)

    tt_p = min(ROW_TILE, t_p)
    nb_s = min(bs, ROW_TILE // t_s)
    rw_chunk = min(RW_HEAD, t_p)
    n_heads = d // RW_HEAD
    y_p, p_wkv, p_shift, p_gla, p_re, p_im = _trunk(x_prompt, ada_p, p_states, weights, 1, tt_p,
                                                     (1, rw_chunk, min(RW_SCAN_TBLK, t_p), min(RW_SCAN_HEADS_PROMPT, n_heads)),
                                                     (1, min(FFN_ROWS, t_p)))
    nb_scan = max(1, min(bs, RW_HEAD // t_s))
    y_s, s_wkv, s_shift, s_gla, s_re, s_im = _trunk(x_sample.transpose(1, 0, 2), ada_s, s_states, weights, nb_s, t_s,
                                                     (nb_scan, t_s, t_s, min(RW_SCAN_HEADS_SAMPLE, n_heads)),
                                                     (min(bs, FFN_ROWS // t_s), t_s))
    y_s = y_s.transpose(1, 0, 2)
    return (y_p, y_s, p_wkv, p_shift, p_gla, p_re, p_im, s_wkv, s_shift, s_gla, s_re, s_im)
```

```python
import functools
import math

import jax
import jax.numpy as jnp
from jax import lax
from jax.experimental import pallas as pl
from jax.experimental.pallas import tpu as pltpu

f32 = jnp.float32
bf16 = jnp.bfloat16
HI = lax.Precision.HIGHEST

EPS = 1e-6
N_ADA = 9
RW_HEAD = 64
RW_GN_EPS = 64e-5
RW_LORA_PAD = 128
GLA_HEADS = 4
GLA_GATE_NORM = 16.0
GLA_CHUNK = 64
GLA_LORA_PAD = 128
S5_GROUP = 16
S5_STATE = 64
S5_SLAB_GROUPS = 8
LANE = 128
SUBLANE = 8
VMEM_LIMIT_BYTES = 56 * 1024 * 1024

ROW_TILE = 512
FFN_ROWS = 512
FFN_TILE_F = 512
FFN_EW_ROWS = 256
MM_TILE_N = 512
ADA_TILE_N = 1024
RW_PROJ_ROWS = 512
RW_PROJ_TILE_N = 256
RW_SCAN_TBLK = 256
RW_SCAN_HEADS_PROMPT = 8
RW_SCAN_HEADS_SAMPLE = 16
GLA_SCAN_TBLK = 256
GLA_SCAN_HEADS_PROMPT = 2
GLA_SCAN_HEADS_SAMPLE = 4
S5_TBLK = 128
S5_SEQS_SAMPLE = 8
S5_CARRY_ELEMS = 8192


def _cparams(*sem):
    return pltpu.CompilerParams(dimension_semantics=sem, vmem_limit_bytes=VMEM_LIMIT_BYTES)


def _dot(a, b):
    return jnp.dot(a, b, preferred_element_type=f32)


def _dot_hi(a, b):
    return jnp.dot(a, b, precision=HI, preferred_element_type=f32)


def _dot_nt_hi(a, b):
    return lax.dot_general(a, b, (((1,), (1,)), ((), ())), precision=HI, preferred_element_type=f32)


def _dot_tn_hi(a, b):
    return lax.dot_general(a, b, (((0,), (0,)), ((), ())), precision=HI, preferred_element_type=f32)


def _normmod(x, g, shift, scale):
    ms = jnp.mean(x * x, axis=-1, keepdims=True)
    y = x * lax.rsqrt(ms + EPS) * g
    return y * (1.0 + scale) + shift


def _ada_kernel(c_ref, w_ref, b_ref, o_ref):
    c = c_ref[...]
    s = (c * jax.nn.sigmoid(c)).astype(bf16)
    o_ref[0] = _dot(s, w_ref[0].astype(bf16)) + b_ref[0]


def _ada_call(c_all, ada_w, ada_b):
    depth, d, n = ada_w.shape
    m = c_all.shape[0]
    tn = min(ADA_TILE_N, n)
    return pl.pallas_call(
        _ada_kernel,
        grid=(depth, n // tn),
        in_specs=[pl.BlockSpec((m, d), lambda l, j: (0, 0)),
                  pl.BlockSpec((1, d, tn), lambda l, j: (l, 0, j)),
                  pl.BlockSpec((1, 1, tn), lambda l, j: (l, 0, j))],
        out_specs=pl.BlockSpec((1, m, tn), lambda l, j: (l, 0, j)),
        out_shape=jax.ShapeDtypeStruct((depth, m, n), f32),
        compiler_params=_cparams("arbitrary", "arbitrary"),
        name="ada",
    )(c_all, ada_w, ada_b.reshape(depth, 1, n))


def _ffn_kernel(x_ref, g_ref, sh_ref, sc_ref, gt_ref, wg_ref, wu_ref, wo_ref, o_ref, h_scr, acc_scr):
    j = pl.program_id(1)
    nb, tt, d = x_ref.shape
    if nb > 1:
        step = max(1, min(nb, FFN_EW_ROWS // tt))
        chunks = [(slice(a, a + step), slice(None), a * tt, step * tt) for a in range(0, nb, step)]
    else:
        step = min(tt, FFN_EW_ROWS)
        chunks = [(slice(None), slice(a, a + step), a, step) for a in range(0, tt, step)]

    @pl.when(j == 0)
    def _():
        for s0, s1, r0, nr in chunks:
            h = _normmod(x_ref[s0, s1, :], g_ref[...], _seq_vec(sh_ref), _seq_vec(sc_ref))
            h_scr[r0:r0 + nr, :] = h.reshape(nr, d).astype(bf16)

    @pl.when(j == 0)
    def _():
        acc_scr[...] = jnp.zeros_like(acc_scr)

    hb = h_scr[...]
    gate = _dot(hb, wg_ref[...])
    up = _dot(hb, wu_ref[...])
    act = (gate * jax.nn.sigmoid(gate) * up).astype(bf16)
    acc_scr[...] += _dot(act, wo_ref[...])

    @pl.when(j == pl.num_programs(1) - 1)
    def _():
        for s0, s1, r0, nr in chunks:
            shp = x_ref[s0, s1, :].shape
            o_ref[s0, s1, :] = x_ref[s0, s1, :] + 0.5 * _seq_vec(gt_ref) * acc_scr[r0:r0 + nr, :].reshape(shp)


def _ffn_call(x, g, shift, scale, gate, w_in, w_out, l, s, nb, tt):
    d = x.shape[-1]
    f = w_out.shape[2]
    tf = min(FFN_TILE_F, f)
    nf = f // tf
    til = _row_tiling(x.shape, nb, tt)
    rows = pl.BlockSpec(til.block(d), til.rmap)
    mod = til.mod_spec(d)
    return pl.pallas_call(
        _ffn_kernel,
        grid=(til.grid, nf),
        in_specs=[rows, pl.BlockSpec((1, 1, d), lambda i, j: (0, 0, 0)), mod, mod, mod,
                  pl.BlockSpec((None, None, d, tf), lambda i, j: (l, s, 0, j)),
                  pl.BlockSpec((None, None, d, tf), lambda i, j: (l, s, 0, nf + j)),
                  pl.BlockSpec((None, None, tf, d), lambda i, j: (l, s, j, 0))],
        out_specs=rows,
        out_shape=jax.ShapeDtypeStruct(x.shape, f32),
        scratch_shapes=[pltpu.VMEM((til.rows, d), bf16), pltpu.VMEM((til.rows, d), f32)],
        compiler_params=_cparams("parallel", "arbitrary"),
        name="ffn",
    )(x, g, _dense_seq_vec(shift, nb), _dense_seq_vec(scale, nb), _dense_seq_vec(gate, nb), w_in, w_in, w_out)


class _RowTiling:
    def __init__(self, shape, nb, tt):
        self.nb = nb
        if nb > 1:
            t, nseq, _ = shape
            assert tt == t and nseq % nb == 0, (shape, nb, tt)
            self.grid, self.rows = nseq // nb, t * nb
            self.block = lambda w: (t, nb, w)
            self.rmap = lambda i, j: (0, i, 0)
            self.cmap = lambda i, j: (0, i, j)
            self.mod_spec = lambda w, **kw: pl.BlockSpec((nb, w), lambda i, j: (i, 0), **kw)
            self.emod_spec = lambda w: pl.BlockSpec((nb, w), lambda i, j: (i, j))
        else:
            nseq, t, _ = shape
            per = t // tt
            assert per * tt == t, (shape, tt)
            self.per = per
            self.grid, self.rows = nseq * per, tt
            self.block = lambda w: (1, tt, w)
            self.rmap = lambda i, j: (i // per, i % per, 0)
            self.cmap = lambda i, j: (i // per, i % per, j)
            self.mod_spec = lambda w, **kw: pl.BlockSpec((1, 1, w), lambda i, j: (i // per, 0, 0), **kw)
            self.emod_spec = lambda w: pl.BlockSpec((1, 1, w), lambda i, j: (i // per, 0, j))


def _row_tiling(shape, nb, tt):
    return _RowTiling(shape, nb, tt)


def _dense_seq_vec(a, nb):
    return a.reshape(a.shape[0], a.shape[2]) if nb > 1 else a


def _seq_vec(ref):
    v = ref[...]
    return v if v.ndim == 3 else v[None, :, :]


def _mm_kernel(*refs, n_row, n_mod, n_vec, n_w, n_erow, n_emod, n_evec, n_out, prologue, epilogue):
    pos = 0

    def take(n):
        nonlocal pos
        out = refs[pos:pos + n]
        pos += n
        return out

    rows, mods, vecs, ws = take(n_row), take(n_mod), take(n_vec), take(n_w)
    erows, emods, evecs, outs = take(n_erow), take(n_emod), take(n_evec), take(n_out)
    a_scr = refs[pos]
    nb, tt, k = rows[0].shape
    tn = ws[0].shape[-1]

    @pl.when(pl.program_id(1) == 0)
    def _():
        a = prologue([r[...] for r in rows], [_seq_vec(m) for m in mods], [v[...] for v in vecs])
        a_scr[...] = a.reshape(nb * tt, k).astype(bf16)

    ab = a_scr[...]
    accs = [_dot(ab, w[...]).reshape(nb, tt, tn) for w in ws]
    res = epilogue(accs, [r[...] for r in erows], [_seq_vec(m) for m in emods], [v[...] for v in evecs])
    for o_ref, o in zip(outs, res):
        o_ref[...] = o


def _mm_call(name, rows, mods, vecs, ws, erows, emods, evecs, n_total, n_out, prologue, epilogue, nb, tt, tn=MM_TILE_N):
    k = rows[0].shape[-1]
    tn = min(tn, n_total)
    assert n_total % tn == 0, (name, n_total, tn)
    nj = n_total // tn
    til = _row_tiling(rows[0].shape, nb, tt)
    mods = [_dense_seq_vec(m, nb) for m in mods]
    emods = [_dense_seq_vec(m, nb) for m in emods]
    in_specs = ([pl.BlockSpec(til.block(k), til.rmap)] * len(rows)
                + [til.mod_spec(k)] * len(mods)
                + [pl.BlockSpec((1, 1, k), lambda i, j: (0, 0, 0))] * len(vecs)
                + [pl.BlockSpec((k, tn), functools.partial(lambda i, j, off: (0, off + j), off=off)) for _, off in ws]
                + [pl.BlockSpec(til.block(tn), til.cmap)] * len(erows)
                + [til.emod_spec(tn)] * len(emods)
                + [pl.BlockSpec((1, 1, tn), lambda i, j: (0, 0, j))] * len(evecs))
    kern = functools.partial(_mm_kernel, n_row=len(rows), n_mod=len(mods), n_vec=len(vecs), n_w=len(ws),
                             n_erow=len(erows), n_emod=len(emods), n_evec=len(evecs), n_out=n_out,
                             prologue=prologue, epilogue=epilogue)
    out = pl.pallas_call(
        kern,
        grid=(til.grid, nj),
        in_specs=in_specs,
        out_specs=[pl.BlockSpec(til.block(tn), til.cmap)] * n_out,
        out_shape=[jax.ShapeDtypeStruct(rows[0].shape[:2] + (n_total,), f32)] * n_out,
        scratch_shapes=[pltpu.VMEM((til.rows, k), bf16)],
        compiler_params=_cparams("parallel", "arbitrary"),
        name=name,
    )(*rows, *mods, *vecs, *[w for w, _ in ws], *erows, *emods, *evecs)
    return out


def _pro_normmod(rows, mods, vecs):
    return _normmod(rows[0], vecs[0], mods[0], mods[1])


def _pro_mul(rows, mods, vecs):
    return rows[0] * rows[1]


def _pro_id(rows, mods, vecs):
    return rows[0]


def _epi_id(accs, erows, emods, evecs):
    return accs


def _epi_residual(accs, erows, emods, evecs):
    return [erows[0] + emods[0] * accs[0]]


def _rms_kernel(x_ref, g_ref, o_ref):
    x = x_ref[...]
    o_ref[...] = x * lax.rsqrt(jnp.mean(x * x, axis=-1, keepdims=True) + EPS) * g_ref[...]


def _final_norm_call(x, g, nb, tt):
    d = x.shape[-1]
    til = _row_tiling(x.shape, nb, tt)
    rmap = lambda i: til.rmap(i, 0)
    return pl.pallas_call(
        _rms_kernel,
        grid=(til.grid,),
        in_specs=[pl.BlockSpec(til.block(d), rmap), pl.BlockSpec((1, 1, d), lambda i: (0, 0, 0))],
        out_specs=pl.BlockSpec(til.block(d), rmap),
        out_shape=jax.ShapeDtypeStruct(x.shape, f32),
        compiler_params=_cparams("parallel"),
        name="final_norm",
    )(x, g)


def _rwproj_kernel(x_ref, xprev_ref, sp_ref, ng_ref, sh_ref, sc_ref, mu_ref, w1_ref, a1_ref, g1_ref,
                   wr_ref, wk_ref, wv_ref, w2_ref, a2_ref, g2_ref, w0_ref, a0_ref,
                   r_ref, k_ref, v_ref, lw_ref, a_ref, g_ref, hl_ref, xm_scr, tw_scr, ta_scr, tg_scr, *, tiles_per_seq,
                   time_major):
    b0, b1, d = x_ref.shape
    rows = b0 * b1
    tn = wr_ref.shape[-1]

    @pl.when(pl.program_id(1) == 0)
    def _():
        ng, sh, sc = ng_ref[...], _seq_vec(sh_ref), _seq_vec(sc_ref)
        h3 = _normmod(x_ref[...], ng, sh, sc)
        first = _seq_vec(sp_ref)
        if time_major:
            hp3 = jnp.concatenate([first, h3[:-1]], axis=0)
            hl_ref[...] = h3[b0 - 1:b0]
        else:
            if tiles_per_seq > 1:
                h_before = _normmod(xprev_ref[...], ng, sh, sc)[:, SUBLANE - 1:SUBLANE, :]
                first = jnp.where(pl.program_id(0) % tiles_per_seq == 0, first, h_before)
            tok = lax.broadcasted_iota(jnp.int32, h3.shape, 1)
            hp3 = jnp.where(tok == 0, first, pltpu.roll(h3, 1, axis=1))
            hl_ref[...] = h3[:, b1 - SUBLANE:, :]
        h = h3.reshape(rows, d)
        dlt = hp3.reshape(rows, d) - h
        mu = mu_ref[...]
        for p in range(3):
            xm_scr[p] = (h + dlt * mu[p:p + 1]).astype(bf16)
        xw = (h + dlt * mu[3:4]).astype(bf16)
        tw_scr[...] = jnp.tanh(_dot(xw, w1_ref[...])).astype(bf16)
        xa = (h + dlt * mu[4:5]).astype(bf16)
        ta_scr[...] = _dot(xa, a1_ref[...]).astype(bf16)
        xg = (h + dlt * mu[5:6]).astype(bf16)
        tg_scr[...] = jax.nn.sigmoid(_dot(xg, g1_ref[...])).astype(bf16)

    shp = (b0, b1, tn)
    r_ref[...] = _dot(xm_scr[0], wr_ref[...]).reshape(shp)
    k_ref[...] = _dot(xm_scr[1], wk_ref[...]).reshape(shp)
    v_ref[...] = _dot(xm_scr[2], wv_ref[...]).reshape(shp)
    w_log = -jax.nn.softplus(-(w0_ref[0] + _dot(tw_scr[...], w2_ref[...]))) - 0.5
    lw_ref[...] = (-jnp.exp(w_log)).reshape(shp)
    a_ref[...] = jax.nn.sigmoid(a0_ref[0] + _dot(ta_scr[...], a2_ref[...])).reshape(shp)
    g_ref[...] = _dot(tg_scr[...], g2_ref[...]).reshape(shp)


def _rwproj_call(x, shift_prev, ng, shift, scale, mu, w1, a1, g1, wr, wk, wv, w2, a2, g2, w0, a0, nb, tt):
    d = x.shape[-1]
    tn = min(RW_PROJ_TILE_N, d)
    til = _row_tiling(x.shape, nb, tt)
    if nb > 1:
        per = 1
        pmap = til.rmap
        prev_block = til.block(d)
        tail_spec = pl.BlockSpec((1, nb, d), til.rmap)
        tail_shape = (1, x.shape[1], d)
    else:
        per = til.per
        pmap = lambda i, j: (i // per, jnp.maximum((i % per) * (tt // SUBLANE) - 1, 0), 0)
        prev_block = (1, SUBLANE, d)
        tail_spec = pl.BlockSpec((1, SUBLANE, d), til.rmap)
        tail_shape = (x.shape[0], per * SUBLANE, d)
    mod = til.mod_spec(d)
    full = lambda a: pl.BlockSpec(a.shape, lambda i, j: (0,) * a.ndim)
    col = lambda a: pl.BlockSpec((a.shape[0], tn), lambda i, j: (0, j))
    shift_prev, shift, scale = (_dense_seq_vec(a, nb) for a in (shift_prev, shift, scale))
    rows = til.rows
    kern = functools.partial(_rwproj_kernel, tiles_per_seq=per, time_major=nb > 1)
    return pl.pallas_call(
        kern,
        grid=(til.grid, d // tn),
        in_specs=[pl.BlockSpec(til.block(d), til.rmap), pl.BlockSpec(prev_block, pmap), mod,
                  pl.BlockSpec((1, 1, d), lambda i, j: (0, 0, 0)), mod, mod, full(mu), full(w1), full(a1), full(g1),
                  col(wr), col(wk), col(wv), col(w2), col(a2), col(g2),
                  pl.BlockSpec((1, 1, tn), lambda i, j: (0, 0, j)), pl.BlockSpec((1, 1, tn), lambda i, j: (0, 0, j))],
        out_specs=[pl.BlockSpec(til.block(tn), til.cmap)] * 6 + [tail_spec],
        out_shape=[jax.ShapeDtypeStruct(x.shape, f32)] * 6 + [jax.ShapeDtypeStruct(tail_shape, f32)],
        scratch_shapes=[pltpu.VMEM((3, rows, d), bf16), pltpu.VMEM((rows, w1.shape[1]), bf16),
                        pltpu.VMEM((rows, a1.shape[1]), bf16), pltpu.VMEM((rows, g1.shape[1]), bf16)],
        compiler_params=_cparams("parallel", "arbitrary"),
        name="rwkv_proj",
    )(x, x, shift_prev, ng, shift, scale, mu, w1, a1, g1, wr, wk, wv, w2, a2, g2, w0, a0)


def _dot_nt(a, b):
    return lax.dot_general(a, b, (((1,), (1,)), ((), ())), preferred_element_type=f32)


def _dot_tn(a, b):
    return lax.dot_general(a, b, (((0,), (0,)), ((), ())), preferred_element_type=f32)


def _rwkv_chunk_kernel(r_ref, k_ref, v_ref, lw_ref, a_ref, kk_ref, ka_ref, rk_ref, gw_ref, gb_ref, s0_ref,
                       y_ref, sout_ref, s_scr, rp_scr, y0_scr, bon_scr, m_scr, n_scr, *, nb, tt, n_inner):
    c = pl.program_id(2)
    n = RW_HEAD
    big = nb * tt
    n_dbl = max(1, math.ceil(math.log2(tt)))

    @pl.when(c == 0)
    def _():
        s_scr[...] = s0_ref[...]

    ti = lax.broadcasted_iota(jnp.int32, (big, big), 0)
    si = lax.broadcasted_iota(jnp.int32, (big, big), 1)
    same = (ti // tt) == (si // tt)
    incl = jnp.logical_and(same, si <= ti)
    strict = jnp.logical_and(same, si < ti)
    cumsum = _make_masked_sum(lambda t_, s_: jnp.logical_and((t_ // tt) == (s_ // tt), s_ <= t_), big)
    seqsum = _make_masked_sum(lambda t_, s_: (t_ // tt) == (s_ // tt), big)
    eye_f = (ti == si).astype(f32)
    eye_n = (lax.broadcasted_iota(jnp.int32, (n, n), 0) == lax.broadcasted_iota(jnp.int32, (n, n), 1)).astype(f32)

    kkp, kap, rkp, gwp, gbp = kk_ref[0], ka_ref[0], rk_ref[0], gw_ref[0], gb_ref[0]

    hp = r_ref.shape[-1] // n

    def load(ref, ci):
        if nb > 1:
            return ref[...].reshape(big, hp * n)
        return ref[0, ci * big:(ci + 1) * big, :]

    ch = []
    for ci in range(n_inner):
        r, k, v, lw, a = (load(ref, ci) for ref in (r_ref, k_ref, v_ref, lw_ref, a_ref))
        cum = cumsum(lw)
        tot = cum[big - 1:big, :] if nb == 1 else seqsum(lw)
        g_in = jnp.exp(cum)
        g_prev = jnp.exp(cum - lw)
        g_inv = jnp.exp(-cum)
        g_rest = jnp.exp(tot - cum)
        g_tot = jnp.exp(tot)
        kk = k * kkp
        k2 = k * (1.0 + (a - 1.0) * kap)
        for hh in range(hp):
            sl = slice(hh * n, (hh + 1) * n)
            kk_h = kk[:, sl]
            kk_h = kk_h * jnp.minimum(lax.rsqrt(jnp.sum(kk_h * kk_h, axis=1, keepdims=True)), 1e12)
            b_h = kk_h * a[:, sl]
            r_h, k_h, v_h = r[:, sl], k2[:, sl], v[:, sl]
            rt = r_h * g_in[:, sl]
            at = -kk_h * g_prev[:, sl]
            bon_scr[ci, hh] = jnp.sum(r_h * k_h * rkp[:, sl], axis=1, keepdims=True) * v_h
            ch.append(dict(
                ci=ci, hh=hh, rt=rt, at=at, v=v_h, vb=v_h.astype(bf16), gt=g_tot[:, sl],
                bhat=b_h * g_rest[:, sl], khat=k_h * g_rest[:, sl],
                ra=jnp.concatenate([rt, at], axis=0).astype(bf16),
                bk=jnp.concatenate([b_h * g_inv[:, sl], k_h * g_inv[:, sl]], axis=0).astype(bf16)))
    for q in ch:
        q["amat"] = _dot_nt(q["ra"], q["bk"])
    for q in ch:
        amat = q["amat"]
        q["a_rb"] = jnp.where(incl, amat[:big, :big], 0.0).astype(bf16)
        q["a_rk"] = jnp.where(incl, amat[:big, big:], 0.0).astype(bf16)
        q["pw"] = jnp.where(strict, amat[big:, :big], 0.0)
        q["a_ak"] = jnp.where(strict, amat[big:, big:], 0.0).astype(bf16)
        q["tm"] = eye_f + q["pw"]
    for _ in range(n_dbl - 1):
        for q in ch:
            pwb = q["pw"].astype(bf16)
            q["pw"] = _dot(pwb, pwb)
        for q in ch:
            q["tm"] = q["tm"] + _dot(q["tm"].astype(bf16), q["pw"].astype(bf16))
    for q in ch:
        q["akv"] = _dot(q["a_ak"], q["vb"])
    for q in ch:
        q["wa"] = _dot(q["tm"].astype(bf16), jnp.concatenate([q["akv"], q["at"]], axis=1).astype(bf16))
    for q in ch:
        q["aw"] = _dot(q["a_rb"], q["wa"].astype(bf16))
        q["arkv"] = _dot(q["a_rk"], q["vb"])
    for q in ch:
        ci, hh, wa = q["ci"], q["hh"], q["wa"]
        rp_scr[ci, hh] = q["rt"] + q["aw"][:, n:]
        y0_scr[ci, hh] = q["aw"][:, :n] + q["arkv"]
        for i in range(nb):
            rows = slice(i * tt, (i + 1) * tt)
            bh_i = q["bhat"][rows].astype(bf16)
            m_scr[ci, hh, i] = eye_n * q["gt"][i * tt:i * tt + 1] + _dot_tn(wa[rows, n:].astype(bf16), bh_i)
            n_scr[ci, hh, i] = _dot_tn(jnp.concatenate([wa[rows, :n], q["v"][rows]], axis=0).astype(bf16),
                                       jnp.concatenate([q["bhat"][rows], q["khat"][rows]], axis=0).astype(bf16))

    for ci in range(n_inner):
        ys = []
        for hh in range(hp):
            sl = slice(hh * n, (hh + 1) * n)
            rp = rp_scr[ci, hh]
            parts = []
            for i in range(nb):
                sb = s_scr[i, hh].astype(bf16)
                parts.append(_dot_nt(rp[i * tt:(i + 1) * tt].astype(bf16), sb))
                s_scr[i, hh] = _dot(sb, m_scr[ci, hh, i].astype(bf16)) + n_scr[ci, hh, i]
            y = (parts[0] if nb == 1 else jnp.concatenate(parts, axis=0)) + y0_scr[ci, hh]
            mean = jnp.mean(y, axis=1, keepdims=True)
            var = jnp.mean(jnp.square(y - mean), axis=1, keepdims=True)
            ys.append((y - mean) * lax.rsqrt(var + RW_GN_EPS) * gwp[:, sl] + gbp[:, sl] + bon_scr[ci, hh])
        yo = jnp.concatenate(ys, axis=1)
        if nb > 1:
            y_ref[...] = yo.reshape(nb, tt, hp * n)
        else:
            y_ref[0, ci * big:(ci + 1) * big, :] = yo

    @pl.when(c == pl.num_programs(2) - 1)
    def _():
        sout_ref[...] = s_scr[...]


def _rwkv_pair_kernel(r_ref, k_ref, v_ref, lw_ref, a_ref, kk_ref, ka_ref, rk_ref, gw_ref, gb_ref, s0_ref,
                      y_ref, sout_ref, s_scr, rp_scr, y0_scr, bon_scr, m_scr, n_scr, *, nb, tt, n_inner):
    c = pl.program_id(2)
    n = RW_HEAD
    w2 = 2 * n
    big = nb * tt
    n_dbl = max(1, math.ceil(math.log2(tt)))
    pp = r_ref.shape[-1] // w2

    lane_r = lax.broadcasted_iota(jnp.int32, (w2, w2), 0)
    lane_c = lax.broadcasted_iota(jnp.int32, (w2, w2), 1)
    bd = (lane_r // n) == (lane_c // n)
    eye_w = (lane_r == lane_c).astype(f32)

    @pl.when(c == 0)
    def _():
        s_scr[...] = jnp.zeros_like(s_scr)
        for i in range(nb):
            for p in range(pp):
                s_scr[i, p, 0:n, 0:n] = s0_ref[i, 2 * p]
                s_scr[i, p, n:w2, n:w2] = s0_ref[i, 2 * p + 1]

    ti = lax.broadcasted_iota(jnp.int32, (big, w2), 0)
    si = lax.broadcasted_iota(jnp.int32, (big, w2), 1) % big
    same2 = (ti // tt) == (si // tt)
    incl2 = jnp.logical_and(same2, si <= ti)
    strict2 = jnp.logical_and(same2, si < ti)
    upper = lax.broadcasted_iota(jnp.int32, (big, w2), 1) >= big
    cumsum = _make_masked_sum(lambda t_, s_: jnp.logical_and((t_ // tt) == (s_ // tt), s_ <= t_), big)
    seqsum = _make_masked_sum(lambda t_, s_: (t_ // tt) == (s_ // tt), big)
    eye_f = (lax.broadcasted_iota(jnp.int32, (big, big), 0) == lax.broadcasted_iota(jnp.int32, (big, big), 1)).astype(f32)
    head_a = lax.broadcasted_iota(jnp.int32, (big, w2), 1) < n
    zeros_lw = jnp.zeros((big, w2), bf16)

    def head_sum(x):
        s_a = jnp.sum(jnp.where(head_a, x, 0.0), axis=1, keepdims=True)
        s_b = jnp.sum(jnp.where(head_a, 0.0, x), axis=1, keepdims=True)
        return jnp.where(head_a, s_a, s_b)

    def load(ref, ci):
        if nb > 1:
            return ref[...].reshape(big, pp * w2)
        return ref[0, ci * big:(ci + 1) * big, :]

    ch, slabs = [], []
    for ci in range(n_inner):
        r, k, v, lw, a = (load(ref, ci) for ref in (r_ref, k_ref, v_ref, lw_ref, a_ref))
        cum = cumsum(lw)
        tot = cum[big - 1:big, :] if nb == 1 else seqsum(lw)
        for p in range(pp):
            sl = slice(p * w2, (p + 1) * w2)
            r_p, k_p, v_p, lw_p, a_p, cum_p, tot_p = r[:, sl], k[:, sl], v[:, sl], lw[:, sl], a[:, sl], cum[:, sl], tot[:, sl]
            g_inv = jnp.exp(-cum_p)
            g_rest = jnp.exp(tot_p - cum_p)
            kk = k_p * kk_ref[0][:, sl]
            kk = kk * jnp.minimum(lax.rsqrt(head_sum(kk * kk)), 1e12)
            k2 = k_p * (1.0 + (a_p - 1.0) * ka_ref[0][:, sl])
            b = kk * a_p
            rt = r_p * jnp.exp(cum_p)
            at = -kk * jnp.exp(cum_p - lw_p)
            bhat, khat = b * g_rest, k2 * g_rest
            bon_scr[ci, p] = head_sum(r_p * k2 * rk_ref[0][:, sl]) * v_p
            slab = dict(ci=ci, p=p, bhat=bhat, khat=khat, v=v_p, gt=jnp.exp(tot_p),
                        bk=jnp.concatenate([b * g_inv, k2 * g_inv], axis=0).astype(bf16))
            slabs.append(slab)
            for hd in range(2):
                mh = head_a if hd == 0 else jnp.logical_not(head_a)
                at_h = jnp.where(mh, at, 0.0)
                rt_h = jnp.where(mh, rt, 0.0)
                ch.append(dict(slab=slab, rt=rt_h, at=at_h,
                               ra=jnp.concatenate([rt_h, at_h], axis=0).astype(bf16),
                               vv=jnp.concatenate([zeros_lw, jnp.where(mh, v_p, 0.0).astype(bf16)], axis=0)))
    for q in ch:
        q["amat"] = _dot_nt(q["ra"], q["slab"]["bk"])
    for q in ch:
        amat = q["amat"]
        q["top"] = jnp.where(incl2, amat[:big], 0.0).astype(bf16)
        bot = jnp.where(strict2, amat[big:], 0.0)
        q["bot_k"] = jnp.where(upper, bot, 0.0).astype(bf16)
        q["pw"] = bot[:, :big]
        q["tm"] = eye_f + q["pw"]
    for _ in range(n_dbl - 1):
        for q in ch:
            pwb = q["pw"].astype(bf16)
            q["pw"] = _dot(pwb, pwb)
        for q in ch:
            q["tm"] = q["tm"] + _dot(q["tm"].astype(bf16), q["pw"].astype(bf16))
    for q in ch:
        q["akv"] = _dot(q["bot_k"], q["vv"])
    for q in ch:
        q["wa"] = _dot(q["tm"].astype(bf16), jnp.concatenate([q["akv"], q["at"]], axis=1).astype(bf16))
    for q in ch:
        wab = q["wa"].astype(bf16)
        q["arb_a"] = _dot(q["top"][:, :big], wab[:, w2:])
        q["y0"] = _dot(q["top"], jnp.concatenate([wab[:, :w2], q["vv"][big:]], axis=0))
    for idx, slab in enumerate(slabs):
        qa, qb = ch[2 * idx], ch[2 * idx + 1]
        ci, p = slab["ci"], slab["p"]
        rp_scr[ci, p] = qa["rt"] + qb["rt"] + qa["arb_a"] + qb["arb_a"]
        y0_scr[ci, p] = qa["y0"] + qb["y0"]
        wa = qa["wa"] + qb["wa"]
        w_pair, a_pair = wa[:, :w2], wa[:, w2:]
        for i in range(nb):
            rows = slice(i * tt, (i + 1) * tt)
            bh_i = slab["bhat"][rows].astype(bf16)
            m_scr[ci, p, i] = eye_w * slab["gt"][i * tt:i * tt + 1] + jnp.where(bd, _dot_tn(a_pair[rows].astype(bf16), bh_i), 0.0)
            n_scr[ci, p, i] = jnp.where(bd, _dot_tn(
                jnp.concatenate([w_pair[rows], slab["v"][rows]], axis=0).astype(bf16),
                jnp.concatenate([slab["bhat"][rows], slab["khat"][rows]], axis=0).astype(bf16)), 0.0)

    for ci in range(n_inner):
        ys = []
        for p in range(pp):
            sl = slice(p * w2, (p + 1) * w2)
            rp = rp_scr[ci, p]
            parts = []
            for i in range(nb):
                sb = s_scr[i, p].astype(bf16)
                parts.append(_dot_nt(rp[i * tt:(i + 1) * tt].astype(bf16), sb))
                s_scr[i, p] = _dot(sb, m_scr[ci, p, i].astype(bf16)) + n_scr[ci, p, i]
            y = (parts[0] if nb == 1 else jnp.concatenate(parts, axis=0)) + y0_scr[ci, p]
            mean = head_sum(y) * (1.0 / n)
            var = head_sum(jnp.square(y - mean)) * (1.0 / n)
            ys.append((y - mean) * lax.rsqrt(var + RW_GN_EPS) * gw_ref[0][:, sl] + gb_ref[0][:, sl] + bon_scr[ci, p])
        yo = ys[0] if pp == 1 else jnp.concatenate(ys, axis=1)
        if nb > 1:
            y_ref[...] = yo.reshape(nb, tt, pp * w2)
        else:
            y_ref[0, ci * big:(ci + 1) * big, :] = yo

    @pl.when(c == pl.num_programs(2) - 1)
    def _():
        for i in range(nb):
            for p in range(pp):
                sout_ref[i, 2 * p] = s_scr[i, p, 0:n, 0:n]
                sout_ref[i, 2 * p + 1] = s_scr[i, p, n:w2, n:w2]


def _rwscan_call(r, k, v, lw, a, k_k, k_a, r_k, gn_w, gn_b, s0_all, layer, nb, tt, tblk, hp):
    nseq, t, d = r.shape
    nh = d // RW_HEAD
    lanes = hp * RW_HEAD
    nc = t // tblk
    n_inner = tblk // tt if nb == 1 else 1
    xspec = pl.BlockSpec((nb, tblk, lanes), lambda i, p, c: (i, c, p))
    pspec = pl.BlockSpec((1, 1, lanes), lambda i, p, c: (0, 0, p))
    sspec = pl.BlockSpec((nb, hp, RW_HEAD, RW_HEAD), lambda i, p, c: (i, p, 0, 0))
    assert hp % 2 == 0 and nb * tt == RW_HEAD, (hp, nb, tt)
    pp, w2 = hp // 2, 2 * RW_HEAD
    kern = functools.partial(_rwkv_pair_kernel, nb=nb, tt=tt, n_inner=n_inner)
    chunk_rows = pltpu.VMEM((n_inner, pp, nb * tt, w2), f32)
    chunk_mats = pltpu.VMEM((n_inner, pp, nb, w2, w2), f32)
    return pl.pallas_call(
        kern,
        grid=(nseq // nb, nh // hp, nc),
        in_specs=[xspec] * 5 + [pspec] * 5
                 + [pl.BlockSpec((None, nb, hp, RW_HEAD, RW_HEAD), lambda i, p, c: (layer, i, p, 0, 0))],
        out_specs=[xspec, sspec],
        out_shape=[jax.ShapeDtypeStruct((nseq, t, d), f32), jax.ShapeDtypeStruct(s0_all.shape[1:], f32)],
        scratch_shapes=[pltpu.VMEM((nb, pp, w2, w2), f32), chunk_rows, chunk_rows, chunk_rows,
                        chunk_mats, chunk_mats],
        compiler_params=_cparams("parallel", "parallel", "arbitrary"),
        name="rwkv_scan",
    )(r, k, v, lw, a, k_k, k_a, r_k, gn_w, gn_b, s0_all)


def _split3(x):
    hi = x.astype(bf16)
    r1 = x - hi.astype(f32)
    mid = r1.astype(bf16)
    return hi, mid, (r1 - mid.astype(f32)).astype(bf16)


def _make_masked_sum(mask_fn, rows):
    if (3 * rows) % 16:
        ti = lax.broadcasted_iota(jnp.int32, (rows, rows), 0)
        si = lax.broadcasted_iota(jnp.int32, (rows, rows), 1)
        mask_f = mask_fn(ti, si).astype(f32)
        return lambda x: _dot_hi(mask_f, x)
    ti = lax.broadcasted_iota(jnp.int32, (rows, 3 * rows), 0)
    si = lax.broadcasted_iota(jnp.int32, (rows, 3 * rows), 1) % rows
    mask3 = jnp.where(mask_fn(ti, si), 1.0, 0.0).astype(bf16)
    return lambda x: _dot(mask3, jnp.concatenate(_split3(x), axis=0))


def _glascan_kernel(q_ref, k_ref, v_ref, r_ref, la_ref, on_ref, s0_ref, o_ref, sout_ref, s_scr, *, chunk, n_inner, hb):
    c = pl.program_id(2)
    hk = q_ref.shape[-1] // hb
    hv = v_ref.shape[-1] // hb

    @pl.when(c == 0)
    def _():
        s_scr[...] = s0_ref[0]

    ti = lax.broadcasted_iota(jnp.int32, (chunk, chunk), 0)
    si = lax.broadcasted_iota(jnp.int32, (chunk, chunk), 1)
    causal = si <= ti
    cumsum = _make_masked_sum(lambda t_, s_: s_ <= t_, chunk)

    ch = []
    for ci in range(n_inner):
        rows = slice(ci * chunk, (ci + 1) * chunk)
        for h in range(hb):
            ks, vs = slice(h * hk, (h + 1) * hk), slice(h * hv, (h + 1) * hv)
            ch.append(dict(rows=rows, h=h, vs=vs, q=q_ref[0, rows, ks], k=k_ref[0, rows, ks],
                           vb=v_ref[0, rows, vs].astype(bf16), la=la_ref[0, rows, ks]))
    for z in ch:
        z["b"] = cumsum(z["la"])
    for z in ch:
        b = z["b"]
        b_last = b[chunk - 1:chunk, :]
        z["q_in"] = (z["q"] * (hk ** -0.5) * jnp.exp(b)).astype(bf16)
        z["k_in"] = (z["k"] * jnp.exp(-b)).astype(bf16)
        z["kd"] = (z["k"] * jnp.exp(b_last - b)).astype(bf16)
        z["dec"] = jnp.transpose(jnp.broadcast_to(jnp.exp(b_last), (SUBLANE, hk)))[:, 0:1]
    for z in ch:
        z["att"] = _dot_nt(z["q_in"], z["k_in"])
    for z in ch:
        z["ov"] = _dot(jnp.where(causal, z["att"], 0.0).astype(bf16), z["vb"])
        z["inc"] = _dot_tn(z["kd"], z["vb"])

    for z in ch:
        h, rows, vs = z["h"], z["rows"], z["vs"]
        s = s_scr[h]
        o = z["ov"] + _dot(z["q_in"], s.astype(bf16))
        s_scr[h] = s * z["dec"] + z["inc"]
        o = o * lax.rsqrt(jnp.mean(o * o, axis=-1, keepdims=True) + EPS) * on_ref[...]
        rr = r_ref[0, rows, vs]
        o_ref[0, rows, vs] = o * (rr * jax.nn.sigmoid(rr))

    @pl.when(c == pl.num_programs(2) - 1)
    def _():
        sout_ref[0] = s_scr[...]


def _glascan_call(qkvr, la, o_norm, s0, tblk, hb):
    nseq, t, _ = qkvr.shape
    _, gh, hk, hv = s0.shape
    chunk = math.gcd(t, GLA_CHUNK)
    tblk = min(tblk, t)
    k_off = (gh * hk) // (hb * hk)
    v_off = (2 * gh * hk) // (hb * hv)
    r_off = (2 * gh * hk + gh * hv) // (hb * hv)
    kern = functools.partial(_glascan_kernel, chunk=chunk, n_inner=tblk // chunk, hb=hb)
    return pl.pallas_call(
        kern,
        grid=(nseq, gh // hb, t // tblk),
        in_specs=[pl.BlockSpec((1, tblk, hb * hk), lambda b, h, c: (b, c, h)),
                  pl.BlockSpec((1, tblk, hb * hk), lambda b, h, c: (b, c, k_off + h)),
                  pl.BlockSpec((1, tblk, hb * hv), lambda b, h, c: (b, c, v_off + h)),
                  pl.BlockSpec((1, tblk, hb * hv), lambda b, h, c: (b, c, r_off + h)),
                  pl.BlockSpec((1, tblk, hb * hk), lambda b, h, c: (b, c, h)),
                  pl.BlockSpec((1, hv), lambda b, h, c: (0, 0)),
                  pl.BlockSpec((1, hb, hk, hv), lambda b, h, c: (b, h, 0, 0))],
        out_specs=[pl.BlockSpec((1, tblk, hb * hv), lambda b, h, c: (b, c, h)),
                   pl.BlockSpec((1, hb, hk, hv), lambda b, h, c: (b, h, 0, 0))],
        out_shape=[jax.ShapeDtypeStruct((nseq, t, gh * hv), f32), jax.ShapeDtypeStruct(s0.shape, f32)],
        scratch_shapes=[pltpu.VMEM((hb, hk, hv), f32)],
        compiler_params=_cparams("parallel", "parallel", "arbitrary"),
        name="gla_scan",
    )(qkvr, qkvr, qkvr, qkvr, la, o_norm, s0)


def _s5prep_kernel(are_ref, aim_ref, ldt_ref, bre_ref, bim_ref, abre_ref, abim_ref, bbre_ref, bbim_ref):
    a_re, a_im = are_ref[...], aim_ref[...]
    dt = jnp.exp(ldt_ref[...])
    mag = jnp.exp(a_re * dt)
    ab_re, ab_im = mag * jnp.cos(a_im * dt), mag * jnp.sin(a_im * dt)
    den = a_re * a_re + a_im * a_im
    nr = ab_re - 1.0
    cf_re = (nr * a_re + ab_im * a_im) / den
    cf_im = (ab_im * a_re - nr * a_im) / den
    abre_ref[...] = ab_re
    abim_ref[...] = ab_im
    b_re, b_im = bre_ref[...], bim_ref[...]
    bbre_ref[...] = cf_re * b_re - cf_im * b_im
    bbim_ref[...] = cf_re * b_im + cf_im * b_re


def _s5prep_call(a_re, a_im, log_dt, b_re_t, b_im_t):
    g, n = a_re.shape
    return pl.pallas_call(
        _s5prep_kernel,
        out_shape=[jax.ShapeDtypeStruct((g, 1, n), f32)] * 2 + [jax.ShapeDtypeStruct(b_re_t.shape, f32)] * 2,
        name="s5_prep",
    )(a_re.reshape(g, 1, n), a_im.reshape(g, 1, n), log_dt.reshape(g, 1, 1), b_re_t, b_im_t)


def _s5scan_kernel(x_ref, g_ref, sh_ref, sc_ref, wbu_ref, wc_ref, abre_ref, abim_ref, dsk_ref, x0re_ref, x0im_ref,
                   y_ref, xre_ref, xim_ref, bre_scr, bim_scr, sre_scr, sim_scr, *, lane_chunk, nb):
    c = pl.program_id(1)
    d = x_ref.shape[-1]
    rows = x_ref.shape[0] * x_ref.shape[1]
    tblk = rows // nb
    n_slab = wbu_ref.shape[0]
    half = wbu_ref.shape[2] // 2
    n_state = n_slab * half

    @pl.when(c == 0)
    def _():
        sre_scr[...] = x0re_ref[0]
        sim_scr[...] = x0im_ref[0]

    h = _normmod(x_ref[...], g_ref[...], sh_ref[...], sc_ref[...]).reshape(rows, d)
    hb = h.astype(bf16)
    for s in range(n_slab):
        bu = _dot(hb[:, s * LANE:(s + 1) * LANE], wbu_ref[s])
        bre_scr[:, s * half:(s + 1) * half] = bu[:, :half]
        bim_scr[:, s * half:(s + 1) * half] = bu[:, half:]

    for q in range(n_state // lane_chunk):
        ls = slice(q * lane_chunk, (q + 1) * lane_chunk)
        ar, ai = abre_ref[:, ls], abim_ref[:, ls]

        def step(t, carry):
            xr, xi = carry
            row = pl.ds(pl.multiple_of(t * nb, nb), nb)
            nxr = ar * xr - ai * xi + bre_scr[row, ls]
            nxi = ar * xi + ai * xr + bim_scr[row, ls]
            bre_scr[row, ls] = nxr
            bim_scr[row, ls] = nxi
            return nxr, nxi

        xr, xi = lax.fori_loop(0, tblk, step, (sre_scr[:, ls], sim_scr[:, ls]))
        sre_scr[:, ls] = xr
        sim_scr[:, ls] = xi

    for s in range(n_slab):
        xs = jnp.concatenate([bre_scr[:, s * half:(s + 1) * half], bim_scr[:, s * half:(s + 1) * half]], axis=1)
        ys = _dot(xs.astype(bf16), wc_ref[s])
        sl = slice(s * LANE, (s + 1) * LANE)
        y_ref[:, :, sl] = jax.nn.gelu(ys + dsk_ref[:, sl] * h[:, sl]).reshape(y_ref.shape[0], y_ref.shape[1], LANE)

    @pl.when(c == pl.num_programs(1) - 1)
    def _():
        xre_ref[0] = sre_scr[...]
        xim_ref[0] = sim_scr[...]


def _s5scan_call(x, g, shift, scale, wbu, wc, ab_re, ab_im, d_skip, x0_re, x0_im, tblk, nb):
    if nb > 1:
        t, nseq, d = x.shape
        tblk = t
        shift, scale = shift.reshape(1, nseq, d), scale.reshape(1, nseq, d)
        xspec = pl.BlockSpec((t, nb, d), lambda b, c: (0, b, 0))
        mod = pl.BlockSpec((1, nb, d), lambda b, c: (0, b, 0))
    else:
        nseq, t, d = x.shape
        tblk = min(tblk, t)
        xspec = pl.BlockSpec((1, tblk, d), lambda b, c: (b, c, 0))
        mod = pl.BlockSpec((1, 1, d), lambda b, c: (b, 0, 0))
    n_state = ab_re.shape[1]
    full = lambda a: pl.BlockSpec(a.shape, lambda b, c: (0,) * a.ndim)
    st = pl.BlockSpec((1, nb, n_state), lambda b, c: (b, 0, 0))
    st_shape = jax.ShapeDtypeStruct((nseq // nb, nb, n_state), f32)
    kern = functools.partial(_s5scan_kernel, lane_chunk=min(n_state, S5_CARRY_ELEMS // nb), nb=nb)
    rows = nb * tblk
    return pl.pallas_call(
        kern,
        grid=(nseq // nb, t // tblk),
        in_specs=[xspec, pl.BlockSpec((1, 1, d), lambda b, c: (0, 0, 0)),
                  mod, mod, full(wbu), full(wc), full(ab_re), full(ab_im), full(d_skip), st, st],
        out_specs=[xspec, st, st],
        out_shape=[jax.ShapeDtypeStruct(x.shape, f32), st_shape, st_shape],
        scratch_shapes=[pltpu.VMEM((rows, n_state), f32), pltpu.VMEM((rows, n_state), f32),
                        pltpu.VMEM((nb, n_state), f32), pltpu.VMEM((nb, n_state), f32)],
        compiler_params=_cparams("parallel", "arbitrary"),
        name="s5_scan",
    )(x, g, shift, scale, wbu, wc, ab_re, ab_im, d_skip,
      x0_re.reshape(nseq // nb, nb, n_state), x0_im.reshape(nseq // nb, nb, n_state))


def _epi_glu(accs, erows, emods, evecs):
    val = accs[0] + evecs[0]
    gate = accs[1] + evecs[1]
    return [erows[0] + emods[0] * (val * jax.nn.sigmoid(gate))]


def _rwkv_mixer(x, g, shift, scale, gm, shift_prev, wkv_all, layer, p, nb, tt, scan_cfg):
    t, d = (x.shape[0] if nb > 1 else x.shape[1]), x.shape[-1]
    pnb, ptt = (RW_PROJ_ROWS // t, t) if nb > 1 else (1, min(RW_PROJ_ROWS, t))
    r, k, v, lw, a, gg, h_tail = _rwproj_call(x, shift_prev[:, None, :], g, shift, scale, p["mu"], p["w1"], p["a1"], p["g1"],
                                               p["wr"], p["wk"], p["wv"], p["w2"], p["a2"], p["g2"], p["w0"], p["a0"], pnb, ptt)
    seq_major = (lambda z: z.transpose(1, 0, 2)) if nb > 1 else (lambda z: z)
    y, wkv = _rwscan_call(*(seq_major(z) for z in (r, k, v, lw, a)), p["k_k"], p["k_a"], p["r_k"], p["gn_w"], p["gn_b"],
                          wkv_all, layer, *scan_cfg)
    (x_new,) = _mm_call("rwkv_out", [seq_major(y), gg], [], [], [(p["w_o"], 0)], [x], [gm], [], d, 1,
                        _pro_mul, _epi_residual, nb, tt)
    return x_new, (h_tail[0] if nb > 1 else h_tail[:, -1]), wkv


def _gla_mixer(x, g, shift, scale, gm, s0, p, nb, tt):
    d = x.shape[-1]
    n_main = p["w_main"].shape[1]
    (qkvr,) = _mm_call("gla_in", [x], [shift, scale], [g], [(p["w_main"], 0)], [], [], [], n_main, 1,
                       _pro_normmod, _epi_id, nb, tt)
    (a_low,) = _mm_call("gla_gate_in", [x], [shift, scale], [g], [(p["w_gate"], 0)], [], [], [], GLA_LORA_PAD, 1,
                        _pro_normmod, _epi_id, nb, tt)

    def epi_la(accs, erows, emods, evecs):
        return [jax.nn.log_sigmoid(accs[0] + evecs[0]) / GLA_GATE_NORM]

    dk = p["a_w2"].shape[1]
    (la,) = _mm_call("gla_gate", [a_low], [], [], [(p["a_w2"], 0)], [], [], [p["a_b"]], dk, 1, _pro_id, epi_la, nb, tt)
    hb = min(s0.shape[1], GLA_SCAN_HEADS_SAMPLE if nb > 1 else GLA_SCAN_HEADS_PROMPT)
    seq_major = (lambda z: z.transpose(1, 0, 2)) if nb > 1 else (lambda z: z)
    o, s_new = _glascan_call(seq_major(qkvr), seq_major(la), p["o_norm"], s0, GLA_SCAN_TBLK, hb)
    (x_new,) = _mm_call("gla_out", [seq_major(o)], [], [], [(p["w_o"], 0)], [x], [gm], [], d, 1,
                        _pro_id, _epi_residual, nb, tt)
    return x_new, s_new


def _s5_mixer(x, g, shift, scale, gm, x0_re, x0_im, p, nb, tt):
    d = x.shape[-1]
    s5_nb = min(x.shape[1], S5_SEQS_SAMPLE) if nb > 1 else 1
    yg, xre, xim = _s5scan_call(x, g, shift, scale, p["wbu"], p["wc"], p["ab_re"], p["ab_im"], p["d_skip"],
                                x0_re, x0_im, S5_TBLK, s5_nb)
    (x_new,) = _mm_call("s5_glu", [yg], [], [], [(p["glu_w"], 0), (p["glu_w"], d // min(MM_TILE_N, d))], [x], [gm],
                        [p["glu_b_val"], p["glu_b_gate"]], d, 1, _pro_id, _epi_glu, nb, tt)
    return x_new, xre.reshape(x0_re.shape), xim.reshape(x0_im.shape)


def _s5_params(a_re, a_im, log_dt, b_re, b_im, c_re, c_im, d_skip, glu_w, glu_b):
    g, n, cg = b_re.shape
    d = g * cg
    ab_re, ab_im, bb_re, bb_im = _s5prep_call(a_re, a_im, log_dt, b_re.transpose(0, 2, 1), b_im.transpose(0, 2, 1))
    sg = S5_SLAB_GROUPS
    eye = jnp.eye(sg, dtype=f32)

    def bdiag(m):
        gq, rr, cc = m.shape
        m = m.reshape(gq // sg, sg, rr, cc)
        return jnp.einsum("sjrc,jk->sjrkc", m, eye).reshape(gq // sg, sg * rr, sg * cc)

    wbu = jnp.concatenate([bdiag(bb_re), bdiag(bb_im)], axis=2).astype(bf16)
    wc = jnp.concatenate([bdiag(c_re.transpose(0, 2, 1)), bdiag(-c_im.transpose(0, 2, 1))], axis=1).astype(bf16)
    return dict(wbu=wbu, wc=wc, ab_re=ab_re.reshape(1, g * n), ab_im=ab_im.reshape(1, g * n),
                d_skip=d_skip.reshape(1, d), glu_w=glu_w.astype(bf16),
                glu_b_val=glu_b[:d].reshape(1, 1, d), glu_b_gate=glu_b[d:].reshape(1, 1, d))


def _trunk(x, ada, states, weights, nb, tt, rw_scan_cfg, ffn_cfg):
    d = x.shape[-1]
    depth, nseq = ada.shape[:2]
    wkv, shift, gla, s5_re, s5_im = states
    new = dict(wkv=[], shift=[], gla=[], re=[], im=[])
    ia = ib = ic = 0
    vec = lambda l, j: ada[l, :, j, :].reshape(nseq, 1, d)
    for l in range(depth):
        ng = lambda s: weights["norm_g"][l, s].reshape(1, 1, d)
        x = _ffn_call(x, ng(0), vec(l, 0), vec(l, 1), vec(l, 2), weights["ffn_w_in"], weights["ffn_w_out"], l, 0, *ffn_cfg)
        kind = l % 3
        if kind == 0:
            x, sh, st = _rwkv_mixer(x, ng(1), vec(l, 3), vec(l, 4), vec(l, 5), shift[ia], wkv, ia, weights["rw"][ia],
                                    nb, tt, rw_scan_cfg)
            new["shift"].append(sh)
            new["wkv"].append(st)
            ia += 1
        elif kind == 1:
            x, st = _gla_mixer(x, ng(1), vec(l, 3), vec(l, 4), vec(l, 5), gla[ib], weights["gla"][ib], nb, tt)
            new["gla"].append(st)
            ib += 1
        else:
            x, sr, si = _s5_mixer(x, ng(1), vec(l, 3), vec(l, 4), vec(l, 5), s5_re[ic], s5_im[ic], weights["s5"][ic], nb, tt)
            new["re"].append(sr)
            new["im"].append(si)
            ic += 1
        x = _ffn_call(x, ng(2), vec(l, 6), vec(l, 7), vec(l, 8), weights["ffn_w_in"], weights["ffn_w_out"], l, 1, *ffn_cfg)
    y = _final_norm_call(x, weights["final_g"].reshape(1, 1, d), nb, tt)
    return (y, jnp.stack(new["wkv"]), jnp.stack(new["shift"]), jnp.stack(new["gla"]),
            jnp.stack(new["re"]), jnp.stack(new["im"]))


def _pad_to(a, axis, size):
    pad = [(0, 0)] * a.ndim
    pad[axis] = (0, size - a.shape[axis])
    return jnp.pad(a, pad)


def kernel(x_prompt, x_sample, state_rwkv_wkv, state_rwkv_shift, state_gla, state_s5_re, state_s5_im, c_prompt, c_sample, norm_g, ada_w, ada_b, ffn_w_in, ffn_w_out, rw_mu, rw_w_rkv, rw_w0, rw_w1, rw_w2, rw_a0, rw_a1, rw_a2, rw_g1, rw_g2, rw_k_k, rw_k_a, rw_r_k, rw_gn_w, rw_gn_b, rw_w_o, gla_w_in, gla_a_w2, gla_a_b, gla_o_norm, gla_w_o, s5_A_re, s5_A_im, s5_log_dt, s5_B_re, s5_B_im, s5_C_re, s5_C_im, s5_D, s5_glu_w, s5_glu_b, final_g):
    bp, t_p, d = x_prompt.shape
    bs, t_s, _ = x_sample.shape
    depth = ada_w.shape[0]
    n_a, n_b, n_c = rw_mu.shape[0], gla_w_in.shape[0], s5_A_re.shape[0]

    n_c_rows = bp + bs
    c_all = _pad_to(jnp.concatenate([c_prompt, c_sample], axis=0), 0, -(-n_c_rows // SUBLANE) * SUBLANE)
    ada = _ada_call(c_all, ada_w, ada_b)
    ada_p = ada[:, :bp].reshape(depth, bp, N_ADA, d)
    ada_s = ada[:, bp:n_c_rows].reshape(depth, bs, N_ADA, d)

    rw = []
    for i in range(n_a):
        vec = lambda a: a[i].reshape(1, 1, d)
        rw.append(dict(
            mu=rw_mu[i], wr=rw_w_rkv[i, 0].astype(bf16), wk=rw_w_rkv[i, 1].astype(bf16), wv=rw_w_rkv[i, 2].astype(bf16),
            w1=_pad_to(rw_w1[i], 1, RW_LORA_PAD).astype(bf16), w2=_pad_to(rw_w2[i], 0, RW_LORA_PAD).astype(bf16),
            a1=_pad_to(rw_a1[i], 1, RW_LORA_PAD).astype(bf16), a2=_pad_to(rw_a2[i], 0, RW_LORA_PAD).astype(bf16),
            g1=rw_g1[i].astype(bf16), g2=rw_g2[i].astype(bf16), w0=vec(rw_w0), a0=vec(rw_a0),
            k_k=vec(rw_k_k), k_a=vec(rw_k_a), r_k=rw_r_k[i].reshape(1, 1, d), gn_w=vec(rw_gn_w), gn_b=vec(rw_gn_b),
            w_o=rw_w_o[i].astype(bf16)))
    gl = []
    for i in range(n_b):
        n_main = gla_w_in.shape[2] - gla_a_w2.shape[1]
        gl.append(dict(
            w_main=gla_w_in[i, :, :n_main].astype(bf16),
            w_gate=_pad_to(gla_w_in[i, :, n_main:], 1, GLA_LORA_PAD).astype(bf16),
            a_w2=_pad_to(gla_a_w2[i], 0, GLA_LORA_PAD).astype(bf16), a_b=gla_a_b[i].reshape(1, 1, -1),
            o_norm=gla_o_norm[i].reshape(1, -1), w_o=gla_w_o[i].astype(bf16)))
    s5 = [_s5_params(s5_A_re[i], s5_A_im[i], s5_log_dt[i], s5_B_re[i], s5_B_im[i], s5_C_re[i], s5_C_im[i],
                     s5_D[i], s5_glu_w[i], s5_glu_b[i]) for i in range(n_c)]
    weights = dict(norm_g=norm_g, ffn_w_in=ffn_w_in.astype(bf16), ffn_w_out=ffn_w_out.astype(bf16),
                   rw=rw, gla=gl, s5=s5, final_g=final_g)

    zeros = lambda s: jnp.zeros((s.shape[0], bp) + s.shape[2:], f32)
    p_states = tuple(zeros(s) for s in (state_rwkv_wkv, state_rwkv_shift, state_gla, state_s5_re, state_s5_im))
    s_states = (state_rwkv_wkv, state_rwkv_shift, state_gla, state_s5_re, state_s5_im)

    tt_p = min(ROW_TILE, t_p)
    nb_s = min(bs, ROW_TILE // t_s)
    rw_chunk = min(RW_HEAD, t_p)
    n_heads = d // RW_HEAD
    y_p, p_wkv, p_shift, p_gla, p_re, p_im = _trunk(x_prompt, ada_p, p_states, weights, 1, tt_p,
                                                     (1, rw_chunk, min(RW_SCAN_TBLK, t_p), min(RW_SCAN_HEADS_PROMPT, n_heads)),
                                                     (1, min(FFN_ROWS, t_p)))
    nb_scan = max(1, min(bs, RW_HEAD // t_s))
    y_s, s_wkv, s_shift, s_gla, s_re, s_im = _trunk(x_sample.transpose(1, 0, 2), ada_s, s_states, weights, nb_s, t_s,
                                                     (nb_scan, t_s, t_s, min(RW_SCAN_HEADS_SAMPLE, n_heads)),
                                                     (min(bs, FFN_ROWS // t_s), t_s))
    y_s = y_s.transpose(1, 0, 2)
    return (y_p, y_s, p_wkv, p_shift, p_gla, p_re, p_im, s_wkv, s_shift, s_gla, s_re, s_im)
```

```python
import functools
import math

import jax
import jax.numpy as jnp
from jax import lax
from jax.experimental import pallas as pl
from jax.experimental.pallas import tpu as pltpu

f32 = jnp.float32
bf16 = jnp.bfloat16
HI = lax.Precision.HIGHEST

EPS = 1e-6
N_ADA = 9
RW_HEAD = 64
RW_GN_EPS = 64e-5
RW_LORA_PAD = 128
GLA_HEADS = 4
GLA_GATE_NORM = 16.0
GLA_CHUNK = 64
GLA_LORA_PAD = 128
S5_GROUP = 16
S5_STATE = 64
S5_SLAB_GROUPS = 8
LANE = 128
SUBLANE = 8
VMEM_LIMIT_BYTES = 56 * 1024 * 1024

ROW_TILE = 512
FFN_ROWS = 512
FFN_TILE_F = 512
FFN_EW_ROWS = 256
MM_TILE_N = 512
ADA_TILE_N = 1024
RW_PROJ_ROWS = 512
RW_PROJ_TILE_N = 256
RW_SCAN_TBLK = 256
RW_SCAN_HEADS_PROMPT = 8
RW_SCAN_HEADS_SAMPLE = 16
GLA_SCAN_TBLK = 256
GLA_SCAN_HEADS_PROMPT = 2
GLA_SCAN_HEADS_SAMPLE = 4
S5_TBLK = 128
S5_SEQS_SAMPLE = 8
S5_CARRY_ELEMS = 8192


def _cparams(*sem):
    return pltpu.CompilerParams(dimension_semantics=sem, vmem_limit_bytes=VMEM_LIMIT_BYTES)


def _dot(a, b):
    return jnp.dot(a, b, preferred_element_type=f32)


def _dot_hi(a, b):
    return jnp.dot(a, b, precision=HI, preferred_element_type=f32)


def _dot_nt_hi(a, b):
    return lax.dot_general(a, b, (((1,), (1,)), ((), ())), precision=HI, preferred_element_type=f32)


def _dot_tn_hi(a, b):
    return lax.dot_general(a, b, (((0,), (0,)), ((), ())), precision=HI, preferred_element_type=f32)


def _normmod(x, g, shift, scale):
    ms = jnp.mean(x * x, axis=-1, keepdims=True)
    y = x * lax.rsqrt(ms + EPS) * g
    return y * (1.0 + scale) + shift


def _ada_kernel(c_ref, w_ref, b_ref, o_ref):
    c = c_ref[...]
    s = (c * jax.nn.sigmoid(c)).astype(bf16)
    o_ref[0, 0] = _dot(s, w_ref[0].astype(bf16)) + b_ref[0]


def _ada_call(c_all, ada_w, ada_b, d_model):
    depth, d, n = ada_w.shape
    m = c_all.shape[0]
    tn = min(ADA_TILE_N, d_model)
    per_vec = d_model // tn
    return pl.pallas_call(
        _ada_kernel,
        grid=(depth, n // tn),
        in_specs=[pl.BlockSpec((m, d), lambda l, j: (0, 0)),
                  pl.BlockSpec((1, d, tn), lambda l, j: (l, 0, j)),
                  pl.BlockSpec((1, 1, tn), lambda l, j: (l, 0, j))],
        out_specs=pl.BlockSpec((1, 1, m, tn), lambda l, j: (l, j // per_vec, 0, j % per_vec)),
        out_shape=jax.ShapeDtypeStruct((depth, n // d_model, m, d_model), f32),
        compiler_params=_cparams("arbitrary", "arbitrary"),
        name="ada",
    )(c_all, ada_w, ada_b.reshape(depth, 1, n))


def _ffn_kernel(x_ref, g_ref, sh_ref, sc_ref, gt_ref, wg_ref, wu_ref, wo_ref, o_ref, h_scr, acc_scr):
    j = pl.program_id(1)
    nb, tt, d = x_ref.shape
    if nb > 1:
        step = max(1, min(nb, FFN_EW_ROWS // tt))
        chunks = [(slice(a, a + step), slice(None), a * tt, step * tt) for a in range(0, nb, step)]
    else:
        step = min(tt, FFN_EW_ROWS)
        chunks = [(slice(None), slice(a, a + step), a, step) for a in range(0, tt, step)]

    @pl.when(j == 0)
    def _():
        for s0, s1, r0, nr in chunks:
            h = _normmod(x_ref[s0, s1, :], g_ref[...], _seq_vec(sh_ref), _seq_vec(sc_ref))
            h_scr[r0:r0 + nr, :] = h.reshape(nr, d).astype(bf16)

    @pl.when(j == 0)
    def _():
        acc_scr[...] = jnp.zeros_like(acc_scr)

    hb = h_scr[...]
    gate = _dot(hb, wg_ref[...])
    up = _dot(hb, wu_ref[...])
    act = (gate * jax.nn.sigmoid(gate) * up).astype(bf16)
    acc_scr[...] += _dot(act, wo_ref[...])

    @pl.when(j == pl.num_programs(1) - 1)
    def _():
        for s0, s1, r0, nr in chunks:
            shp = x_ref[s0, s1, :].shape
            o_ref[s0, s1, :] = x_ref[s0, s1, :] + 0.5 * _seq_vec(gt_ref) * acc_scr[r0:r0 + nr, :].reshape(shp)


def _ffn_call(x, g, shift, scale, gate, w_in, w_out, l, s, nb, tt):
    d = x.shape[-1]
    f = w_out.shape[2]
    tf = min(FFN_TILE_F, f)
    nf = f // tf
    til = _row_tiling(x.shape, nb, tt)
    rows = pl.BlockSpec(til.block(d), til.rmap)
    mod = til.mod_spec(d)
    w_specs = [pl.BlockSpec((None, None, d, tf), lambda i, j: (l, s, 0, j)),
               pl.BlockSpec((None, None, d, tf), lambda i, j: (l, s, 0, nf + j)),
               pl.BlockSpec((None, None, tf, d), lambda i, j: (l, s, j, 0))]
    return pl.pallas_call(
        _ffn_kernel,
        grid=(til.grid, nf),
        in_specs=[rows, pl.BlockSpec((1, 1, d), lambda i, j: (0, 0, 0)), mod, mod, mod,
                  pl.BlockSpec((None, None, d, tf), lambda i, j: (l, s, 0, j)),
                  pl.BlockSpec((None, None, d, tf), lambda i, j: (l, s, 0, nf + j)),
                  pl.BlockSpec((None, None, tf, d), lambda i, j: (l, s, j, 0))],
        out_specs=rows,
        out_shape=jax.ShapeDtypeStruct(x.shape, f32),
        scratch_shapes=[pltpu.VMEM((til.rows, d), bf16), pltpu.VMEM((til.rows, d), f32)],
        compiler_params=_cparams("parallel", "arbitrary"),
        name="ffn",
    )(x, g, _dense_seq_vec(shift, nb), _dense_seq_vec(scale, nb), _dense_seq_vec(gate, nb), w_in, w_in, w_out)


class _RowTiling:
    def __init__(self, shape, nb, tt):
        self.nb = nb
        if nb > 1:
            t, nseq, _ = shape
            assert tt == t and nseq % nb == 0, (shape, nb, tt)
            self.grid, self.rows = nseq // nb, t * nb
            self.block = lambda w: (t, nb, w)
            self.rmap = lambda i, j: (0, i, 0)
            self.cmap = lambda i, j: (0, i, j)
            self.mod_spec = lambda w, **kw: pl.BlockSpec((nb, w), lambda i, j: (i, 0), **kw)
            self.emod_spec = lambda w: pl.BlockSpec((nb, w), lambda i, j: (i, j))
        else:
            nseq, t, _ = shape
            per = t // tt
            assert per * tt == t, (shape, tt)
            self.per = per
            self.grid, self.rows = nseq * per, tt
            self.block = lambda w: (1, tt, w)
            self.rmap = lambda i, j: (i // per, i % per, 0)
            self.cmap = lambda i, j: (i // per, i % per, j)
            self.mod_spec = lambda w, **kw: pl.BlockSpec((1, 1, w), lambda i, j: (i // per, 0, 0), **kw)
            self.emod_spec = lambda w: pl.BlockSpec((1, 1, w), lambda i, j: (i // per, 0, j))


def _row_tiling(shape, nb, tt):
    return _RowTiling(shape, nb, tt)


def _dense_seq_vec(a, nb):
    return a.reshape(a.shape[0], a.shape[-1]) if nb > 1 else a


def _seq_vec(ref):
    v = ref[...]
    return v if v.ndim == 3 else v[None, :, :]


def _mm_kernel(*refs, n_row, n_mod, n_vec, n_w, n_erow, n_emod, n_evec, n_out, prologue, epilogue):
    pos = 0

    def take(n):
        nonlocal pos
        out = refs[pos:pos + n]
        pos += n
        return out

    rows, mods, vecs, ws = take(n_row), take(n_mod), take(n_vec), take(n_w)
    erows, emods, evecs, outs = take(n_erow), take(n_emod), take(n_evec), take(n_out)
    a_scr = refs[pos]
    nb, tt, k = rows[0].shape
    tn = ws[0].shape[-1]

    @pl.when(pl.program_id(1) == 0)
    def _():
        a = prologue([r[...] for r in rows], [_seq_vec(m) for m in mods], [v[...] for v in vecs])
        a_scr[...] = a.reshape(nb * tt, k).astype(bf16)

    ab = a_scr[...]
    accs = [_dot(ab, w[...]).reshape(nb, tt, tn) for w in ws]
    res = epilogue(accs, [r[...] for r in erows], [_seq_vec(m) for m in emods], [v[...] for v in evecs])
    for o_ref, o in zip(outs, res):
        o_ref[...] = o


def _mm_call(name, rows, mods, vecs, ws, erows, emods, evecs, n_total, n_out, prologue, epilogue, nb, tt, tn=MM_TILE_N,
             resident_w=False):
    k = rows[0].shape[-1]
    tn = min(tn, n_total)
    assert n_total % tn == 0, (name, n_total, tn)
    nj = n_total // tn
    w_mode = dict(pipeline_mode=pl.Buffered(1)) if resident_w else {}
    assert not resident_w or nj == 1
    til = _row_tiling(rows[0].shape, nb, tt)
    mods = [_dense_seq_vec(m, nb) for m in mods]
    emods = [_dense_seq_vec(m, nb) for m in emods]
    in_specs = ([pl.BlockSpec(til.block(k), til.rmap)] * len(rows)
                + [til.mod_spec(k)] * len(mods)
                + [pl.BlockSpec((1, 1, k), lambda i, j: (0, 0, 0))] * len(vecs)
                + [pl.BlockSpec((k, tn), functools.partial(lambda i, j, off: (0, off + j), off=off), **w_mode)
                   for _, off in ws]
                + [pl.BlockSpec(til.block(tn), til.cmap)] * len(erows)
                + [til.emod_spec(tn)] * len(emods)
                + [pl.BlockSpec((1, 1, tn), lambda i, j: (0, 0, j))] * len(evecs))
    kern = functools.partial(_mm_kernel, n_row=len(rows), n_mod=len(mods), n_vec=len(vecs), n_w=len(ws),
                             n_erow=len(erows), n_emod=len(emods), n_evec=len(evecs), n_out=n_out,
                             prologue=prologue, epilogue=epilogue)
    out = pl.pallas_call(
        kern,
        grid=(til.grid, nj),
        in_specs=in_specs,
        out_specs=[pl.BlockSpec(til.block(tn), til.cmap)] * n_out,
        out_shape=[jax.ShapeDtypeStruct(rows[0].shape[:2] + (n_total,), f32)] * n_out,
        scratch_shapes=[pltpu.VMEM((til.rows, k), bf16)],
        compiler_params=_cparams("parallel", "arbitrary"),
        name=name,
    )(*rows, *mods, *vecs, *[w for w, _ in ws], *erows, *emods, *evecs)
    return out


def _pro_normmod(rows, mods, vecs):
    return _normmod(rows[0], vecs[0], mods[0], mods[1])


def _pro_mul(rows, mods, vecs):
    return rows[0] * rows[1]


def _pro_id(rows, mods, vecs):
    return rows[0]


def _epi_id(accs, erows, emods, evecs):
    return accs


def _epi_residual(accs, erows, emods, evecs):
    return [erows[0] + emods[0] * accs[0]]


def _rms_kernel(x_ref, g_ref, o_ref):
    x = x_ref[...]
    o_ref[...] = x * lax.rsqrt(jnp.mean(x * x, axis=-1, keepdims=True) + EPS) * g_ref[...]


def _final_norm_call(x, g, nb, tt):
    d = x.shape[-1]
    til = _row_tiling(x.shape, nb, tt)
    rmap = lambda i: til.rmap(i, 0)
    return pl.pallas_call(
        _rms_kernel,
        grid=(til.grid,),
        in_specs=[pl.BlockSpec(til.block(d), rmap), pl.BlockSpec((1, 1, d), lambda i: (0, 0, 0))],
        out_specs=pl.BlockSpec(til.block(d), rmap),
        out_shape=jax.ShapeDtypeStruct(x.shape, f32),
        compiler_params=_cparams("parallel"),
        name="final_norm",
    )(x, g)


def _rwproj_kernel(x_ref, xprev_ref, sp_ref, ng_ref, sh_ref, sc_ref, mu_ref, w1_ref, a1_ref, g1_ref,
                   wr_ref, wk_ref, wv_ref, w2_ref, a2_ref, g2_ref, w0_ref, a0_ref,
                   r_ref, k_ref, v_ref, lw_ref, a_ref, g_ref, hl_ref, xm_scr, tw_scr, ta_scr, tg_scr, *, tiles_per_seq,
                   time_major):
    b0, b1, d = x_ref.shape
    rows = b0 * b1
    tn = wr_ref.shape[-1]

    @pl.when(pl.program_id(1) == 0)
    def _():
        ng, sh, sc = ng_ref[...], _seq_vec(sh_ref), _seq_vec(sc_ref)
        h3 = _normmod(x_ref[...], ng, sh, sc)
        first = _seq_vec(sp_ref)
        if time_major:
            hp3 = jnp.concatenate([first, h3[:-1]], axis=0)
            hl_ref[...] = h3[b0 - 1:b0]
        else:
            if tiles_per_seq > 1:
                h_before = _normmod(xprev_ref[...], ng, sh, sc)[:, SUBLANE - 1:SUBLANE, :]
                first = jnp.where(pl.program_id(0) % tiles_per_seq == 0, first, h_before)
            tok = lax.broadcasted_iota(jnp.int32, h3.shape, 1)
            hp3 = jnp.where(tok == 0, first, pltpu.roll(h3, 1, axis=1))
            hl_ref[...] = h3[:, b1 - SUBLANE:, :]
        h = h3.reshape(rows, d)
        dlt = hp3.reshape(rows, d) - h
        mu = mu_ref[...]
        for p in range(3):
            xm_scr[p] = (h + dlt * mu[p:p + 1]).astype(bf16)
        xw = (h + dlt * mu[3:4]).astype(bf16)
        tw_scr[...] = jnp.tanh(_dot(xw, w1_ref[...])).astype(bf16)
        xa = (h + dlt * mu[4:5]).astype(bf16)
        ta_scr[...] = _dot(xa, a1_ref[...]).astype(bf16)
        xg = (h + dlt * mu[5:6]).astype(bf16)
        tg_scr[...] = jax.nn.sigmoid(_dot(xg, g1_ref[...])).astype(bf16)

    shp = (b0, b1, tn)
    r_ref[...] = _dot(xm_scr[0], wr_ref[...]).reshape(shp)
    k_ref[...] = _dot(xm_scr[1], wk_ref[...]).reshape(shp)
    v_ref[...] = _dot(xm_scr[2], wv_ref[...]).reshape(shp)
    w_log = -jax.nn.softplus(-(w0_ref[0] + _dot(tw_scr[...], w2_ref[...]))) - 0.5
    lw_ref[...] = (-jnp.exp(w_log)).reshape(shp)
    a_ref[...] = jax.nn.sigmoid(a0_ref[0] + _dot(ta_scr[...], a2_ref[...])).reshape(shp)
    g_ref[...] = _dot(tg_scr[...], g2_ref[...]).reshape(shp)


def _rwproj_call(x, shift_prev, ng, shift, scale, mu, w1, a1, g1, wr, wk, wv, w2, a2, g2, w0, a0, nb, tt):
    d = x.shape[-1]
    tn = min(RW_PROJ_TILE_N, d)
    til = _row_tiling(x.shape, nb, tt)
    if nb > 1:
        per = 1
        pmap = til.rmap
        prev_block = til.block(d)
        tail_spec = pl.BlockSpec((1, nb, d), til.rmap)
        tail_shape = (1, x.shape[1], d)
    else:
        per = til.per
        pmap = lambda i, j: (i // per, jnp.maximum((i % per) * (tt // SUBLANE) - 1, 0), 0)
        prev_block = (1, SUBLANE, d)
        tail_spec = pl.BlockSpec((1, SUBLANE, d), til.rmap)
        tail_shape = (x.shape[0], per * SUBLANE, d)
    mod = til.mod_spec(d)
    full = lambda a: pl.BlockSpec(a.shape, lambda i, j: (0,) * a.ndim)
    col = lambda a: pl.BlockSpec((a.shape[0], tn), lambda i, j: (0, j))
    shift_prev, shift, scale = (_dense_seq_vec(a, nb) for a in (shift_prev, shift, scale))
    rows = til.rows
    kern = functools.partial(_rwproj_kernel, tiles_per_seq=per, time_major=nb > 1)
    return pl.pallas_call(
        kern,
        grid=(til.grid, d // tn),
        in_specs=[pl.BlockSpec(til.block(d), til.rmap), pl.BlockSpec(prev_block, pmap), mod,
                  pl.BlockSpec((1, 1, d), lambda i, j: (0, 0, 0)), mod, mod, full(mu), full(w1), full(a1), full(g1),
                  col(wr), col(wk), col(wv), col(w2), col(a2), col(g2),
                  pl.BlockSpec((1, 1, tn), lambda i, j: (0, 0, j)), pl.BlockSpec((1, 1, tn), lambda i, j: (0, 0, j))],
        out_specs=[pl.BlockSpec(til.block(tn), til.cmap)] * 6 + [tail_spec],
        out_shape=[jax.ShapeDtypeStruct(x.shape, f32)] * 6 + [jax.ShapeDtypeStruct(tail_shape, f32)],
        scratch_shapes=[pltpu.VMEM((3, rows, d), bf16), pltpu.VMEM((rows, w1.shape[1]), bf16),
                        pltpu.VMEM((rows, a1.shape[1]), bf16), pltpu.VMEM((rows, g1.shape[1]), bf16)],
        compiler_params=_cparams("parallel", "arbitrary"),
        name="rwkv_proj",
    )(x, x, shift_prev, ng, shift, scale, mu, w1, a1, g1, wr, wk, wv, w2, a2, g2, w0, a0)


def _dot_nt(a, b):
    return lax.dot_general(a, b, (((1,), (1,)), ((), ())), preferred_element_type=f32)


def _dot_tn(a, b):
    return lax.dot_general(a, b, (((0,), (0,)), ((), ())), preferred_element_type=f32)


def _rwkv_chunk_kernel(r_ref, k_ref, v_ref, lw_ref, a_ref, kk_ref, ka_ref, rk_ref, gw_ref, gb_ref, s0_ref,
                       y_ref, sout_ref, s_scr, rp_scr, y0_scr, bon_scr, m_scr, n_scr, *, nb, tt, n_inner):
    c = pl.program_id(2)
    n = RW_HEAD
    big = nb * tt
    n_dbl = max(1, math.ceil(math.log2(tt)))

    @pl.when(c == 0)
    def _():
        s_scr[...] = s0_ref[...]

    ti = lax.broadcasted_iota(jnp.int32, (big, big), 0)
    si = lax.broadcasted_iota(jnp.int32, (big, big), 1)
    same = (ti // tt) == (si // tt)
    incl = jnp.logical_and(same, si <= ti)
    strict = jnp.logical_and(same, si < ti)
    cumsum = _make_masked_sum(lambda t_, s_: jnp.logical_and((t_ // tt) == (s_ // tt), s_ <= t_), big)
    seqsum = _make_masked_sum(lambda t_, s_: (t_ // tt) == (s_ // tt), big)
    eye_f = (ti == si).astype(f32)
    eye_n = (lax.broadcasted_iota(jnp.int32, (n, n), 0) == lax.broadcasted_iota(jnp.int32, (n, n), 1)).astype(f32)

    kkp, kap, rkp, gwp, gbp = kk_ref[0], ka_ref[0], rk_ref[0], gw_ref[0], gb_ref[0]

    hp = r_ref.shape[-1] // n

    def load(ref, ci):
        if nb > 1:
            return ref[...].reshape(big, hp * n)
        return ref[0, ci * big:(ci + 1) * big, :]

    ch = []
    for ci in range(n_inner):
        r, k, v, lw, a = (load(ref, ci) for ref in (r_ref, k_ref, v_ref, lw_ref, a_ref))
        cum = cumsum(lw)
        tot = cum[big - 1:big, :] if nb == 1 else seqsum(lw)
        g_in = jnp.exp(cum)
        g_prev = jnp.exp(cum - lw)
        g_inv = jnp.exp(-cum)
        g_rest = jnp.exp(tot - cum)
        g_tot = jnp.exp(tot)
        kk = k * kkp
        k2 = k * (1.0 + (a - 1.0) * kap)
        for hh in range(hp):
            sl = slice(hh * n, (hh + 1) * n)
            kk_h = kk[:, sl]
            kk_h = kk_h * jnp.minimum(lax.rsqrt(jnp.sum(kk_h * kk_h, axis=1, keepdims=True)), 1e12)
            b_h = kk_h * a[:, sl]
            r_h, k_h, v_h = r[:, sl], k2[:, sl], v[:, sl]
            rt = r_h * g_in[:, sl]
            at = -kk_h * g_prev[:, sl]
            bon_scr[ci, hh] = jnp.sum(r_h * k_h * rkp[:, sl], axis=1, keepdims=True) * v_h
            ch.append(dict(
                ci=ci, hh=hh, rt=rt, at=at, v=v_h, vb=v_h.astype(bf16), gt=g_tot[:, sl],
                bhat=b_h * g_rest[:, sl], khat=k_h * g_rest[:, sl],
                ra=jnp.concatenate([rt, at], axis=0).astype(bf16),
                bk=jnp.concatenate([b_h * g_inv[:, sl], k_h * g_inv[:, sl]], axis=0).astype(bf16)))
    for q in ch:
        q["amat"] = _dot_nt(q["ra"], q["bk"])
    for q in ch:
        amat = q["amat"]
        q["a_rb"] = jnp.where(incl, amat[:big, :big], 0.0).astype(bf16)
        q["a_rk"] = jnp.where(incl, amat[:big, big:], 0.0).astype(bf16)
        q["pw"] = jnp.where(strict, amat[big:, :big], 0.0)
        q["a_ak"] = jnp.where(strict, amat[big:, big:], 0.0).astype(bf16)
        q["tm"] = eye_f + q["pw"]
    for _ in range(n_dbl - 1):
        for q in ch:
            pwb = q["pw"].astype(bf16)
            q["pw"] = _dot(pwb, pwb)
        for q in ch:
            q["tm"] = q["tm"] + _dot(q["tm"].astype(bf16), q["pw"].astype(bf16))
    for q in ch:
        q["akv"] = _dot(q["a_ak"], q["vb"])
    for q in ch:
        q["wa"] = _dot(q["tm"].astype(bf16), jnp.concatenate([q["akv"], q["at"]], axis=1).astype(bf16))
    for q in ch:
        q["aw"] = _dot(q["a_rb"], q["wa"].astype(bf16))
        q["arkv"] = _dot(q["a_rk"], q["vb"])
    for q in ch:
        ci, hh, wa = q["ci"], q["hh"], q["wa"]
        rp_scr[ci, hh] = q["rt"] + q["aw"][:, n:]
        y0_scr[ci, hh] = q["aw"][:, :n] + q["arkv"]
        for i in range(nb):
            rows = slice(i * tt, (i + 1) * tt)
            bh_i = q["bhat"][rows].astype(bf16)
            m_scr[ci, hh, i] = eye_n * q["gt"][i * tt:i * tt + 1] + _dot_tn(wa[rows, n:].astype(bf16), bh_i)
            n_scr[ci, hh, i] = _dot_tn(jnp.concatenate([wa[rows, :n], q["v"][rows]], axis=0).astype(bf16),
                                       jnp.concatenate([q["bhat"][rows], q["khat"][rows]], axis=0).astype(bf16))

    for ci in range(n_inner):
        ys = []
        for hh in range(hp):
            sl = slice(hh * n, (hh + 1) * n)
            rp = rp_scr[ci, hh]
            parts = []
            for i in range(nb):
                sb = s_scr[i, hh].astype(bf16)
                parts.append(_dot_nt(rp[i * tt:(i + 1) * tt].astype(bf16), sb))
                s_scr[i, hh] = _dot(sb, m_scr[ci, hh, i].astype(bf16)) + n_scr[ci, hh, i]
            y = (parts[0] if nb == 1 else jnp.concatenate(parts, axis=0)) + y0_scr[ci, hh]
            mean = jnp.mean(y, axis=1, keepdims=True)
            var = jnp.mean(jnp.square(y - mean), axis=1, keepdims=True)
            ys.append((y - mean) * lax.rsqrt(var + RW_GN_EPS) * gwp[:, sl] + gbp[:, sl] + bon_scr[ci, hh])
        yo = jnp.concatenate(ys, axis=1)
        if nb > 1:
            y_ref[...] = yo.reshape(nb, tt, hp * n)
        else:
            y_ref[0, ci * big:(ci + 1) * big, :] = yo

    @pl.when(c == pl.num_programs(2) - 1)
    def _():
        sout_ref[...] = s_scr[...]


def _rwkv_pair_kernel(r_ref, k_ref, v_ref, lw_ref, a_ref, kk_ref, ka_ref, rk_ref, gw_ref, gb_ref, s0_ref,
                      y_ref, sout_ref, s_scr, rp_scr, y0_scr, bon_scr, m_scr, n_scr, *, nb, tt, n_inner):
    c = pl.program_id(2)
    n = RW_HEAD
    w2 = 2 * n
    big = nb * tt
    n_dbl = max(1, math.ceil(math.log2(tt)))
    pp = r_ref.shape[-1] // w2

    lane_r = lax.broadcasted_iota(jnp.int32, (w2, w2), 0)
    lane_c = lax.broadcasted_iota(jnp.int32, (w2, w2), 1)
    bd = (lane_r // n) == (lane_c // n)
    eye_w = (lane_r == lane_c).astype(f32)

    @pl.when(c == 0)
    def _():
        s_scr[...] = jnp.zeros_like(s_scr)
        for i in range(nb):
            for p in range(pp):
                s_scr[i, p, 0:n, 0:n] = s0_ref[i, 2 * p]
                s_scr[i, p, n:w2, n:w2] = s0_ref[i, 2 * p + 1]

    ti = lax.broadcasted_iota(jnp.int32, (big, w2), 0)
    si = lax.broadcasted_iota(jnp.int32, (big, w2), 1) % big
    same2 = (ti // tt) == (si // tt)
    incl2 = jnp.logical_and(same2, si <= ti)
    strict2 = jnp.logical_and(same2, si < ti)
    upper = lax.broadcasted_iota(jnp.int32, (big, w2), 1) >= big
    cumsum = _make_masked_sum(lambda t_, s_: jnp.logical_and((t_ // tt) == (s_ // tt), s_ <= t_), big)
    seqsum = _make_masked_sum(lambda t_, s_: (t_ // tt) == (s_ // tt), big)
    eye_f = (lax.broadcasted_iota(jnp.int32, (big, big), 0) == lax.broadcasted_iota(jnp.int32, (big, big), 1)).astype(f32)
    head_a = lax.broadcasted_iota(jnp.int32, (big, w2), 1) < n
    zeros_lw = jnp.zeros((big, w2), bf16)

    def head_sum(x):
        s_a = jnp.sum(jnp.where(head_a, x, 0.0), axis=1, keepdims=True)
        s_b = jnp.sum(jnp.where(head_a, 0.0, x), axis=1, keepdims=True)
        return jnp.where(head_a, s_a, s_b)

    def load(ref, ci):
        if nb > 1:
            return ref[...].reshape(big, pp * w2)
        return ref[0, ci * big:(ci + 1) * big, :]

    ch, slabs = [], []
    for ci in range(n_inner):
        r, k, v, lw, a = (load(ref, ci) for ref in (r_ref, k_ref, v_ref, lw_ref, a_ref))
        cum = cumsum(lw)
        tot = cum[big - 1:big, :] if nb == 1 else seqsum(lw)
        for p in range(pp):
            sl = slice(p * w2, (p + 1) * w2)
            r_p, k_p, v_p, lw_p, a_p, cum_p, tot_p = r[:, sl], k[:, sl], v[:, sl], lw[:, sl], a[:, sl], cum[:, sl], tot[:, sl]
            g_inv = jnp.exp(-cum_p)
            g_rest = jnp.exp(tot_p - cum_p)
            kk = k_p * kk_ref[0][:, sl]
            kk = kk * jnp.minimum(lax.rsqrt(head_sum(kk * kk)), 1e12)
            k2 = k_p * (1.0 + (a_p - 1.0) * ka_ref[0][:, sl])
            b = kk * a_p
            rt = r_p * jnp.exp(cum_p)
            at = -kk * jnp.exp(cum_p - lw_p)
            bhat, khat = b * g_rest, k2 * g_rest
            bon_scr[ci, p] = head_sum(r_p * k2 * rk_ref[0][:, sl]) * v_p
            slab = dict(ci=ci, p=p, bhat=bhat, khat=khat, v=v_p, gt=jnp.exp(tot_p),
                        bk=jnp.concatenate([b * g_inv, k2 * g_inv], axis=0).astype(bf16))
            slabs.append(slab)
            for hd in range(2):
                mh = head_a if hd == 0 else jnp.logical_not(head_a)
                at_h = jnp.where(mh, at, 0.0)
                rt_h = jnp.where(mh, rt, 0.0)
                ch.append(dict(slab=slab, rt=rt_h, at=at_h,
                               ra=jnp.concatenate([rt_h, at_h], axis=0).astype(bf16),
                               vv=jnp.concatenate([zeros_lw, jnp.where(mh, v_p, 0.0).astype(bf16)], axis=0)))
    for q in ch:
        q["amat"] = _dot_nt(q["ra"], q["slab"]["bk"])
    for q in ch:
        amat = q["amat"]
        q["top"] = jnp.where(incl2, amat[:big], 0.0).astype(bf16)
        bot = jnp.where(strict2, amat[big:], 0.0)
        q["bot_k"] = jnp.where(upper, bot, 0.0).astype(bf16)
        q["pw"] = bot[:, :big]
        q["tm"] = eye_f + q["pw"]
    for _ in range(n_dbl - 1):
        for q in ch:
            pwb = q["pw"].astype(bf16)
            q["pw"] = _dot(pwb, pwb)
        for q in ch:
            q["tm"] = q["tm"] + _dot(q["tm"].astype(bf16), q["pw"].astype(bf16))
    for q in ch:
        q["akv"] = _dot(q["bot_k"], q["vv"])
    for q in ch:
        q["wa"] = _dot(q["tm"].astype(bf16), jnp.concatenate([q["akv"], q["at"]], axis=1).astype(bf16))
    for q in ch:
        wab = q["wa"].astype(bf16)
        q["arb_a"] = _dot(q["top"][:, :big], wab[:, w2:])
        q["y0"] = _dot(q["top"], jnp.concatenate([wab[:, :w2], q["vv"][big:]], axis=0))
    for idx, slab in enumerate(slabs):
        qa, qb = ch[2 * idx], ch[2 * idx + 1]
        ci, p = slab["ci"], slab["p"]
        rp_scr[ci, p] = qa["rt"] + qb["rt"] + qa["arb_a"] + qb["arb_a"]
        y0_scr[ci, p] = qa["y0"] + qb["y0"]
        wa = qa["wa"] + qb["wa"]
        w_pair, a_pair = wa[:, :w2], wa[:, w2:]
        for i in range(nb):
            rows = slice(i * tt, (i + 1) * tt)
            bh_i = slab["bhat"][rows].astype(bf16)
            m_scr[ci, p, i] = eye_w * slab["gt"][i * tt:i * tt + 1] + jnp.where(bd, _dot_tn(a_pair[rows].astype(bf16), bh_i), 0.0)
            n_scr[ci, p, i] = jnp.where(bd, _dot_tn(
                jnp.concatenate([w_pair[rows], slab["v"][rows]], axis=0).astype(bf16),
                jnp.concatenate([slab["bhat"][rows], slab["khat"][rows]], axis=0).astype(bf16)), 0.0)

    for ci in range(n_inner):
        ys = []
        for p in range(pp):
            sl = slice(p * w2, (p + 1) * w2)
            rp = rp_scr[ci, p]
            parts = []
            for i in range(nb):
                sb = s_scr[i, p].astype(bf16)
                parts.append(_dot_nt(rp[i * tt:(i + 1) * tt].astype(bf16), sb))
                s_scr[i, p] = _dot(sb, m_scr[ci, p, i].astype(bf16)) + n_scr[ci, p, i]
            y = (parts[0] if nb == 1 else jnp.concatenate(parts, axis=0)) + y0_scr[ci, p]
            mean = head_sum(y) * (1.0 / n)
            var = head_sum(jnp.square(y - mean)) * (1.0 / n)
            ys.append((y - mean) * lax.rsqrt(var + RW_GN_EPS) * gw_ref[0][:, sl] + gb_ref[0][:, sl] + bon_scr[ci, p])
        yo = ys[0] if pp == 1 else jnp.concatenate(ys, axis=1)
        if nb > 1:
            y_ref[...] = yo.reshape(nb, tt, pp * w2)
        else:
            y_ref[0, ci * big:(ci + 1) * big, :] = yo

    @pl.when(c == pl.num_programs(2) - 1)
    def _():
        for i in range(nb):
            for p in range(pp):
                sout_ref[i, 2 * p] = s_scr[i, p, 0:n, 0:n]
                sout_ref[i, 2 * p + 1] = s_scr[i, p, n:w2, n:w2]


def _rwscan_call(r, k, v, lw, a, k_k, k_a, r_k, gn_w, gn_b, s0_all, s_new_all, layer, nb, tt, tblk, hp):
    nseq, t, d = r.shape
    nh = d // RW_HEAD
    lanes = hp * RW_HEAD
    nc = t // tblk
    n_inner = tblk // tt if nb == 1 else 1
    xspec = pl.BlockSpec((nb, tblk, lanes), lambda i, p, c: (i, c, p))
    pspec = pl.BlockSpec((1, 1, lanes), lambda i, p, c: (0, 0, p))
    sspec = pl.BlockSpec((nb, hp, RW_HEAD, RW_HEAD), lambda i, p, c: (i, p, 0, 0))
    assert hp % 2 == 0 and nb * tt == RW_HEAD, (hp, nb, tt)
    pp, w2 = hp // 2, 2 * RW_HEAD
    base = functools.partial(_rwkv_pair_kernel, nb=nb, tt=tt, n_inner=n_inner)
    chunk_rows = pltpu.VMEM((n_inner, pp, nb * tt, w2), f32)
    chunk_mats = pltpu.VMEM((n_inner, pp, nb, w2, w2), f32)
    layer_spec = pl.BlockSpec((None, nb, hp, RW_HEAD, RW_HEAD), lambda i, p, c: (layer, i, p, 0, 0))
    n_in = 11
    if s_new_all is None:
        kern, extra_specs, extra_args, aliases = base, [], [], {}
    else:
        kern = lambda *refs: base(*refs[:n_in], *refs[n_in + 1:])
        extra_specs, extra_args, aliases = [pl.BlockSpec(memory_space=pl.ANY)], [s_new_all], {n_in: 1}
    return pl.pallas_call(
        kern,
        grid=(nseq // nb, nh // hp, nc),
        in_specs=[xspec] * 5 + [pspec] * 5 + [layer_spec] + extra_specs,
        out_specs=[xspec, layer_spec],
        out_shape=[jax.ShapeDtypeStruct((nseq, t, d), f32), jax.ShapeDtypeStruct(s0_all.shape, f32)],
        input_output_aliases=aliases,
        scratch_shapes=[pltpu.VMEM((nb, pp, w2, w2), f32), chunk_rows, chunk_rows, chunk_rows,
                        chunk_mats, chunk_mats],
        compiler_params=_cparams("parallel", "parallel", "arbitrary"),
        name="rwkv_scan",
    )(r, k, v, lw, a, k_k, k_a, r_k, gn_w, gn_b, s0_all, *extra_args)


def _split3(x):
    hi = x.astype(bf16)
    r1 = x - hi.astype(f32)
    mid = r1.astype(bf16)
    return hi, mid, (r1 - mid.astype(f32)).astype(bf16)


def _make_masked_sum(mask_fn, rows):
    if (3 * rows) % 16:
        ti = lax.broadcasted_iota(jnp.int32, (rows, rows), 0)
        si = lax.broadcasted_iota(jnp.int32, (rows, rows), 1)
        mask_f = mask_fn(ti, si).astype(f32)
        return lambda x: _dot_hi(mask_f, x)
    ti = lax.broadcasted_iota(jnp.int32, (rows, 3 * rows), 0)
    si = lax.broadcasted_iota(jnp.int32, (rows, 3 * rows), 1) % rows
    mask3 = jnp.where(mask_fn(ti, si), 1.0, 0.0).astype(bf16)
    return lambda x: _dot(mask3, jnp.concatenate(_split3(x), axis=0))


def _glascan_kernel(q_ref, k_ref, v_ref, r_ref, la_ref, on_ref, s0_ref, o_ref, sout_ref, s_scr, *, chunk, n_inner, hb):
    c = pl.program_id(2)
    hk = q_ref.shape[-1] // hb
    hv = v_ref.shape[-1] // hb

    @pl.when(c == 0)
    def _():
        s_scr[...] = s0_ref[0]

    ti = lax.broadcasted_iota(jnp.int32, (chunk, chunk), 0)
    si = lax.broadcasted_iota(jnp.int32, (chunk, chunk), 1)
    causal = si <= ti
    cumsum = _make_masked_sum(lambda t_, s_: s_ <= t_, chunk)

    ch = []
    for ci in range(n_inner):
        rows = slice(ci * chunk, (ci + 1) * chunk)
        for h in range(hb):
            ks, vs = slice(h * hk, (h + 1) * hk), slice(h * hv, (h + 1) * hv)
            ch.append(dict(rows=rows, h=h, vs=vs, q=q_ref[0, rows, ks], k=k_ref[0, rows, ks],
                           vb=v_ref[0, rows, vs].astype(bf16), la=la_ref[0, rows, ks]))
    for z in ch:
        z["b"] = cumsum(z["la"])
    for z in ch:
        b = z["b"]
        b_last = b[chunk - 1:chunk, :]
        z["q_in"] = (z["q"] * (hk ** -0.5) * jnp.exp(b)).astype(bf16)
        z["k_in"] = (z["k"] * jnp.exp(-b)).astype(bf16)
        z["kd"] = (z["k"] * jnp.exp(b_last - b)).astype(bf16)
        z["dec"] = jnp.transpose(jnp.broadcast_to(jnp.exp(b_last), (SUBLANE, hk)))[:, 0:1]
    for z in ch:
        z["att"] = _dot_nt(z["q_in"], z["k_in"])
    for z in ch:
        z["ov"] = _dot(jnp.where(causal, z["att"], 0.0).astype(bf16), z["vb"])
        z["inc"] = _dot_tn(z["kd"], z["vb"])

    for z in ch:
        h, rows, vs = z["h"], z["rows"], z["vs"]
        s = s_scr[h]
        o = z["ov"] + _dot(z["q_in"], s.astype(bf16))
        s_scr[h] = s * z["dec"] + z["inc"]
        o = o * lax.rsqrt(jnp.mean(o * o, axis=-1, keepdims=True) + EPS) * on_ref[...]
        rr = r_ref[0, rows, vs]
        o_ref[0, rows, vs] = o * (rr * jax.nn.sigmoid(rr))

    @pl.when(c == pl.num_programs(2) - 1)
    def _():
        sout_ref[0] = s_scr[...]


def _glascan_call(qkvr, la, o_norm, s0, tblk, hb):
    nseq, t, _ = qkvr.shape
    _, gh, hk, hv = s0.shape
    chunk = math.gcd(t, GLA_CHUNK)
    tblk = min(tblk, t)
    k_off = (gh * hk) // (hb * hk)
    v_off = (2 * gh * hk) // (hb * hv)
    r_off = (2 * gh * hk + gh * hv) // (hb * hv)
    kern = functools.partial(_glascan_kernel, chunk=chunk, n_inner=tblk // chunk, hb=hb)
    return pl.pallas_call(
        kern,
        grid=(nseq, gh // hb, t // tblk),
        in_specs=[pl.BlockSpec((1, tblk, hb * hk), lambda b, h, c: (b, c, h)),
                  pl.BlockSpec((1, tblk, hb * hk), lambda b, h, c: (b, c, k_off + h)),
                  pl.BlockSpec((1, tblk, hb * hv), lambda b, h, c: (b, c, v_off + h)),
                  pl.BlockSpec((1, tblk, hb * hv), lambda b, h, c: (b, c, r_off + h)),
                  pl.BlockSpec((1, tblk, hb * hk), lambda b, h, c: (b, c, h)),
                  pl.BlockSpec((1, hv), lambda b, h, c: (0, 0)),
                  pl.BlockSpec((1, hb, hk, hv), lambda b, h, c: (b, h, 0, 0))],
        out_specs=[pl.BlockSpec((1, tblk, hb * hv), lambda b, h, c: (b, c, h)),
                   pl.BlockSpec((1, hb, hk, hv), lambda b, h, c: (b, h, 0, 0))],
        out_shape=[jax.ShapeDtypeStruct((nseq, t, gh * hv), f32), jax.ShapeDtypeStruct(s0.shape, f32)],
        scratch_shapes=[pltpu.VMEM((hb, hk, hv), f32)],
        compiler_params=_cparams("parallel", "parallel", "arbitrary"),
        name="gla_scan",
    )(qkvr, qkvr, qkvr, qkvr, la, o_norm, s0)


def _s5prep_kernel(are_ref, aim_ref, ldt_ref, bre_ref, bim_ref, abre_ref, abim_ref, bbre_ref, bbim_ref):
    a_re, a_im = are_ref[...], aim_ref[...]
    dt = jnp.exp(ldt_ref[...])
    mag = jnp.exp(a_re * dt)
    ab_re, ab_im = mag * jnp.cos(a_im * dt), mag * jnp.sin(a_im * dt)
    den = a_re * a_re + a_im * a_im
    nr = ab_re - 1.0
    cf_re = (nr * a_re + ab_im * a_im) / den
    cf_im = (ab_im * a_re - nr * a_im) / den
    abre_ref[...] = ab_re
    abim_ref[...] = ab_im
    b_re, b_im = bre_ref[...], bim_ref[...]
    bbre_ref[...] = cf_re * b_re - cf_im * b_im
    bbim_ref[...] = cf_re * b_im + cf_im * b_re


def _s5prep_call(a_re, a_im, log_dt, b_re_t, b_im_t):
    g, n = a_re.shape
    return pl.pallas_call(
        _s5prep_kernel,
        out_shape=[jax.ShapeDtypeStruct((g, 1, n), f32)] * 2 + [jax.ShapeDtypeStruct(b_re_t.shape, f32)] * 2,
        name="s5_prep",
    )(a_re.reshape(g, 1, n), a_im.reshape(g, 1, n), log_dt.reshape(g, 1, 1), b_re_t, b_im_t)


def _s5scan_kernel(x_ref, g_ref, sh_ref, sc_ref, wbu_ref, wc_ref, abre_ref, abim_ref, dsk_ref, x0re_ref, x0im_ref,
                   y_ref, xre_ref, xim_ref, bre_scr, bim_scr, sre_scr, sim_scr, *, lane_chunk, nb):
    c = pl.program_id(1)
    d = x_ref.shape[-1]
    rows = x_ref.shape[0] * x_ref.shape[1]
    tblk = rows // nb
    n_slab = wbu_ref.shape[0]
    half = wbu_ref.shape[2] // 2
    n_state = n_slab * half

    @pl.when(c == 0)
    def _():
        sre_scr[...] = x0re_ref[0]
        sim_scr[...] = x0im_ref[0]

    h = _normmod(x_ref[...], g_ref[...], sh_ref[...], sc_ref[...]).reshape(rows, d)
    hb = h.astype(bf16)
    for s in range(n_slab):
        bu = _dot(hb[:, s * LANE:(s + 1) * LANE], wbu_ref[s])
        bre_scr[:, s * half:(s + 1) * half] = bu[:, :half]
        bim_scr[:, s * half:(s + 1) * half] = bu[:, half:]

    for q in range(n_state // lane_chunk):
        ls = slice(q * lane_chunk, (q + 1) * lane_chunk)
        ar, ai = abre_ref[:, ls], abim_ref[:, ls]

        def step(t, carry):
            xr, xi = carry
            row = pl.ds(pl.multiple_of(t * nb, nb), nb)
            nxr = ar * xr - ai * xi + bre_scr[row, ls]
            nxi = ar * xi + ai * xr + bim_scr[row, ls]
            bre_scr[row, ls] = nxr
            bim_scr[row, ls] = nxi
            return nxr, nxi

        xr, xi = lax.fori_loop(0, tblk, step, (sre_scr[:, ls], sim_scr[:, ls]))
        sre_scr[:, ls] = xr
        sim_scr[:, ls] = xi

    for s in range(n_slab):
        xs = jnp.concatenate([bre_scr[:, s * half:(s + 1) * half], bim_scr[:, s * half:(s + 1) * half]], axis=1)
        ys = _dot(xs.astype(bf16), wc_ref[s])
        sl = slice(s * LANE, (s + 1) * LANE)
        y_ref[:, :, sl] = jax.nn.gelu(ys + dsk_ref[:, sl] * h[:, sl]).reshape(y_ref.shape[0], y_ref.shape[1], LANE)

    @pl.when(c == pl.num_programs(1) - 1)
    def _():
        xre_ref[0] = sre_scr[...]
        xim_ref[0] = sim_scr[...]


def _s5scan_call(x, g, shift, scale, wbu, wc, ab_re, ab_im, d_skip, x0_re, x0_im, tblk, nb):
    if nb > 1:
        t, nseq, d = x.shape
        tblk = t
        shift, scale = shift.reshape(1, nseq, d), scale.reshape(1, nseq, d)
        xspec = pl.BlockSpec((t, nb, d), lambda b, c: (0, b, 0))
        mod = pl.BlockSpec((1, nb, d), lambda b, c: (0, b, 0))
    else:
        nseq, t, d = x.shape
        tblk = min(tblk, t)
        xspec = pl.BlockSpec((1, tblk, d), lambda b, c: (b, c, 0))
        mod = pl.BlockSpec((1, 1, d), lambda b, c: (b, 0, 0))
    n_state = ab_re.shape[1]
    full = lambda a: pl.BlockSpec(a.shape, lambda b, c: (0,) * a.ndim)
    st = pl.BlockSpec((1, nb, n_state), lambda b, c: (b, 0, 0))
    st_shape = jax.ShapeDtypeStruct((nseq // nb, nb, n_state), f32)
    kern = functools.partial(_s5scan_kernel, lane_chunk=min(n_state, S5_CARRY_ELEMS // nb), nb=nb)
    rows = nb * tblk
    return pl.pallas_call(
        kern,
        grid=(nseq // nb, t // tblk),
        in_specs=[xspec, pl.BlockSpec((1, 1, d), lambda b, c: (0, 0, 0)),
                  mod, mod, full(wbu), full(wc), full(ab_re), full(ab_im), full(d_skip), st, st],
        out_specs=[xspec, st, st],
        out_shape=[jax.ShapeDtypeStruct(x.shape, f32), st_shape, st_shape],
        scratch_shapes=[pltpu.VMEM((rows, n_state), f32), pltpu.VMEM((rows, n_state), f32),
                        pltpu.VMEM((nb, n_state), f32), pltpu.VMEM((nb, n_state), f32)],
        compiler_params=_cparams("parallel", "arbitrary"),
        name="s5_scan",
    )(x, g, shift, scale, wbu, wc, ab_re, ab_im, d_skip,
      x0_re.reshape(nseq // nb, nb, n_state), x0_im.reshape(nseq // nb, nb, n_state))


def _epi_glu(accs, erows, emods, evecs):
    val = accs[0] + evecs[0]
    gate = accs[1] + evecs[1]
    return [erows[0] + emods[0] * (val * jax.nn.sigmoid(gate))]


def _rwkv_mixer(x, g, shift, scale, gm, shift_prev, wkv_all, wkv_new_all, layer, p, nb, tt, scan_cfg):
    t, d = (x.shape[0] if nb > 1 else x.shape[1]), x.shape[-1]
    pnb, ptt = (RW_PROJ_ROWS // t, t) if nb > 1 else (1, min(RW_PROJ_ROWS, t))
    r, k, v, lw, a, gg, h_tail = _rwproj_call(x, shift_prev[:, None, :], g, shift, scale, p["mu"], p["w1"], p["a1"], p["g1"],
                                               p["wr"], p["wk"], p["wv"], p["w2"], p["a2"], p["g2"], p["w0"], p["a0"], pnb, ptt)
    seq_major = (lambda z: z.transpose(1, 0, 2)) if nb > 1 else (lambda z: z)
    y, wkv = _rwscan_call(*(seq_major(z) for z in (r, k, v, lw, a)), p["k_k"], p["k_a"], p["r_k"], p["gn_w"], p["gn_b"],
                          wkv_all, wkv_new_all, layer, *scan_cfg)
    (x_new,) = _mm_call("rwkv_out", [seq_major(y), gg], [], [], [(p["w_o"], 0)], [x], [gm], [], d, 1,
                        _pro_mul, _epi_residual, nb, tt, tn=d, resident_w=True)
    return x_new, (h_tail[0] if nb > 1 else h_tail[:, -1]), wkv


def _gla_mixer(x, g, shift, scale, gm, s0, p, nb, tt):
    d = x.shape[-1]
    n_main = p["w_main"].shape[1]
    (qkvr,) = _mm_call("gla_in", [x], [shift, scale], [g], [(p["w_main"], 0)], [], [], [], n_main, 1,
                       _pro_normmod, _epi_id, nb, tt)
    (a_low,) = _mm_call("gla_gate_in", [x], [shift, scale], [g], [(p["w_gate"], 0)], [], [], [], GLA_LORA_PAD, 1,
                        _pro_normmod, _epi_id, nb, tt)

    def epi_la(accs, erows, emods, evecs):
        return [jax.nn.log_sigmoid(accs[0] + evecs[0]) / GLA_GATE_NORM]

    dk = p["a_w2"].shape[1]
    (la,) = _mm_call("gla_gate", [a_low], [], [], [(p["a_w2"], 0)], [], [], [p["a_b"]], dk, 1, _pro_id, epi_la, nb, tt)
    hb = min(s0.shape[1], GLA_SCAN_HEADS_SAMPLE if nb > 1 else GLA_SCAN_HEADS_PROMPT)
    seq_major = (lambda z: z.transpose(1, 0, 2)) if nb > 1 else (lambda z: z)
    o, s_new = _glascan_call(seq_major(qkvr), seq_major(la), p["o_norm"], s0, GLA_SCAN_TBLK, hb)
    (x_new,) = _mm_call("gla_out", [seq_major(o)], [], [], [(p["w_o"], 0)], [x], [gm], [], d, 1,
                        _pro_id, _epi_residual, nb, tt, tn=d, resident_w=True)
    return x_new, s_new


def _s5_mixer(x, g, shift, scale, gm, x0_re, x0_im, p, nb, tt):
    d = x.shape[-1]
    s5_nb = min(x.shape[1], S5_SEQS_SAMPLE) if nb > 1 else 1
    yg, xre, xim = _s5scan_call(x, g, shift, scale, p["wbu"], p["wc"], p["ab_re"], p["ab_im"], p["d_skip"],
                                x0_re, x0_im, S5_TBLK, s5_nb)
    (x_new,) = _mm_call("s5_glu", [yg], [], [], [(p["glu_w"], 0), (p["glu_w"], d // min(MM_TILE_N, d))], [x], [gm],
                        [p["glu_b_val"], p["glu_b_gate"]], d, 1, _pro_id, _epi_glu, nb, tt)
    return x_new, xre.reshape(x0_re.shape), xim.reshape(x0_im.shape)


def _s5_params(a_re, a_im, log_dt, b_re, b_im, c_re, c_im, d_skip, glu_w, glu_b):
    g, n, cg = b_re.shape
    d = g * cg
    ab_re, ab_im, bb_re, bb_im = _s5prep_call(a_re, a_im, log_dt, b_re.transpose(0, 2, 1), b_im.transpose(0, 2, 1))
    sg = S5_SLAB_GROUPS
    eye = jnp.eye(sg, dtype=f32)

    def bdiag(m):
        gq, rr, cc = m.shape
        m = m.reshape(gq // sg, sg, rr, cc)
        return jnp.einsum("sjrc,jk->sjrkc", m, eye).reshape(gq // sg, sg * rr, sg * cc)

    wbu = jnp.concatenate([bdiag(bb_re), bdiag(bb_im)], axis=2).astype(bf16)
    wc = jnp.concatenate([bdiag(c_re.transpose(0, 2, 1)), bdiag(-c_im.transpose(0, 2, 1))], axis=1).astype(bf16)
    return dict(wbu=wbu, wc=wc, ab_re=ab_re.reshape(1, g * n), ab_im=ab_im.reshape(1, g * n),
                d_skip=d_skip.reshape(1, d), glu_w=glu_w.astype(bf16),
                glu_b_val=glu_b[:d].reshape(1, 1, d), glu_b_gate=glu_b[d:].reshape(1, 1, d))


def _trunk(x, ada, states, weights, nb, tt, rw_scan_cfg, ffn_cfg):
    d = x.shape[-1]
    depth, _, nseq, _ = ada.shape
    wkv, shift, gla, s5_re, s5_im = states
    new = dict(shift=[], gla=[], re=[], im=[])
    new_wkv = None
    ia = ib = ic = 0
    vec = (lambda l, j: ada[l, j]) if nb > 1 else (lambda l, j: ada[l, j].reshape(nseq, 1, d))
    for l in range(depth):
        ng = lambda s: weights["norm_g"][l, s].reshape(1, 1, d)
        x = _ffn_call(x, ng(0), vec(l, 0), vec(l, 1), vec(l, 2), weights["ffn_w_in"], weights["ffn_w_out"], l, 0, *ffn_cfg)
        kind = l % 3
        if kind == 0:
            x, sh, new_wkv = _rwkv_mixer(x, ng(1), vec(l, 3), vec(l, 4), vec(l, 5), shift[ia], wkv, new_wkv, ia,
                                         weights["rw"][ia], nb, tt, rw_scan_cfg)
            new["shift"].append(sh)
            ia += 1
        elif kind == 1:
            x, st = _gla_mixer(x, ng(1), vec(l, 3), vec(l, 4), vec(l, 5), gla[ib], weights["gla"][ib], nb, tt)
            new["gla"].append(st)
            ib += 1
        else:
            x, sr, si = _s5_mixer(x, ng(1), vec(l, 3), vec(l, 4), vec(l, 5), s5_re[ic], s5_im[ic], weights["s5"][ic], nb, tt)
            new["re"].append(sr)
            new["im"].append(si)
            ic += 1
        x = _ffn_call(x, ng(2), vec(l, 6), vec(l, 7), vec(l, 8), weights["ffn_w_in"], weights["ffn_w_out"], l, 1, *ffn_cfg)
    y = _final_norm_call(x, weights["final_g"].reshape(1, 1, d), nb, tt)
    return (y, new_wkv, jnp.stack(new["shift"]), jnp.stack(new["gla"]),
            jnp.stack(new["re"]), jnp.stack(new["im"]))


def _pad_to(a, axis, size):
    pad = [(0, 0)] * a.ndim
    pad[axis] = (0, size - a.shape[axis])
    return jnp.pad(a, pad)


def kernel(x_prompt, x_sample, state_rwkv_wkv, state_rwkv_shift, state_gla, state_s5_re, state_s5_im, c_prompt, c_sample, norm_g, ada_w, ada_b, ffn_w_in, ffn_w_out, rw_mu, rw_w_rkv, rw_w0, rw_w1, rw_w2, rw_a0, rw_a1, rw_a2, rw_g1, rw_g2, rw_k_k, rw_k_a, rw_r_k, rw_gn_w, rw_gn_b, rw_w_o, gla_w_in, gla_a_w2, gla_a_b, gla_o_norm, gla_w_o, s5_A_re, s5_A_im, s5_log_dt, s5_B_re, s5_B_im, s5_C_re, s5_C_im, s5_D, s5_glu_w, s5_glu_b, final_g):
    bp, t_p, d = x_prompt.shape
    bs, t_s, _ = x_sample.shape
    depth = ada_w.shape[0]
    n_a, n_b, n_c = rw_mu.shape[0], gla_w_in.shape[0], s5_A_re.shape[0]

    n_c_rows = bp + bs
    c_all = _pad_to(jnp.concatenate([c_sample, c_prompt], axis=0), 0, -(-n_c_rows // SUBLANE) * SUBLANE)
    ada = _ada_call(c_all, ada_w, ada_b, d)
    ada_s = ada[:, :, :bs]
    ada_p = ada[:, :, bs:n_c_rows]

    rw = []
    for i in range(n_a):
        vec = lambda a: a[i].reshape(1, 1, d)
        rw.append(dict(
            mu=rw_mu[i], wr=rw_w_rkv[i, 0].astype(bf16), wk=rw_w_rkv[i, 1].astype(bf16), wv=rw_w_rkv[i, 2].astype(bf16),
            w1=_pad_to(rw_w1[i], 1, RW_LORA_PAD).astype(bf16), w2=_pad_to(rw_w2[i], 0, RW_LORA_PAD).astype(bf16),
            a1=_pad_to(rw_a1[i], 1, RW_LORA_PAD).astype(bf16), a2=_pad_to(rw_a2[i], 0, RW_LORA_PAD).astype(bf16),
            g1=rw_g1[i].astype(bf16), g2=rw_g2[i].astype(bf16), w0=vec(rw_w0), a0=vec(rw_a0),
            k_k=vec(rw_k_k), k_a=vec(rw_k_a), r_k=rw_r_k[i].reshape(1, 1, d), gn_w=vec(rw_gn_w), gn_b=vec(rw_gn_b),
            w_o=rw_w_o[i].astype(bf16)))
    gl = []
    for i in range(n_b):
        n_main = gla_w_in.shape[2] - gla_a_w2.shape[1]
        gl.append(dict(
            w_main=gla_w_in[i, :, :n_main].astype(bf16),
            w_gate=_pad_to(gla_w_in[i, :, n_main:], 1, GLA_LORA_PAD).astype(bf16),
            a_w2=_pad_to(gla_a_w2[i], 0, GLA_LORA_PAD).astype(bf16), a_b=gla_a_b[i].reshape(1, 1, -1),
            o_norm=gla_o_norm[i].reshape(1, -1), w_o=gla_w_o[i].astype(bf16)))
    s5 = [_s5_params(s5_A_re[i], s5_A_im[i], s5_log_dt[i], s5_B_re[i], s5_B_im[i], s5_C_re[i], s5_C_im[i],
                     s5_D[i], s5_glu_w[i], s5_glu_b[i]) for i in range(n_c)]
    weights = dict(norm_g=norm_g, ffn_w_in=ffn_w_in.astype(bf16), ffn_w_out=ffn_w_out.astype(bf16),
                   rw=rw, gla=gl, s5=s5, final_g=final_g)

    zeros = lambda s: jnp.zeros((s.shape[0], bp) + s.shape[2:], f32)
    p_states = tuple(zeros(s) for s in (state_rwkv_wkv, state_rwkv_shift, state_gla, state_s5_re, state_s5_im))
    s_states = (state_rwkv_wkv, state_rwkv_shift, state_gla, state_s5_re, state_s5_im)

    tt_p = min(ROW_TILE, t_p)
    nb_s = min(bs, ROW_TILE // t_s)
    rw_chunk = min(RW_HEAD, t_p)
    n_heads = d // RW_HEAD
    y_p, p_wkv, p_shift, p_gla, p_re, p_im = _trunk(x_prompt, ada_p, p_states, weights, 1, tt_p,
                                                     (1, rw_chunk, min(RW_SCAN_TBLK, t_p), min(RW_SCAN_HEADS_PROMPT, n_heads)),
                                                     (1, min(FFN_ROWS, t_p)))
    nb_scan = max(1, min(bs, RW_HEAD // t_s))
    y_s, s_wkv, s_shift, s_gla, s_re, s_im = _trunk(x_sample.transpose(1, 0, 2), ada_s, s_states, weights, nb_s, t_s,
                                                     (nb_scan, t_s, t_s, min(RW_SCAN_HEADS_SAMPLE, n_heads)),
                                                     (min(bs, FFN_ROWS // t_s), t_s))
    y_s = y_s.transpose(1, 0, 2)
    return (y_p, y_s, p_wkv, p_shift, p_gla, p_re, p_im, s_wkv, s_shift, s_gla, s_re, s_im)
```

```python
import functools
import math

import jax
import jax.numpy as jnp
from jax import lax
from jax.experimental import pallas as pl
from jax.experimental.pallas import tpu as pltpu

f32 = jnp.float32
bf16 = jnp.bfloat16
HI = lax.Precision.HIGHEST

EPS = 1e-6
N_ADA = 9
RW_HEAD = 64
RW_GN_EPS = 64e-5
RW_LORA_PAD = 128
GLA_HEADS = 4
GLA_GATE_NORM = 16.0
GLA_CHUNK = 64
GLA_LORA_PAD = 128
S5_GROUP = 16
S5_STATE = 64
S5_SLAB_GROUPS = 8
LANE = 128
SUBLANE = 8
VMEM_LIMIT_BYTES = 56 * 1024 * 1024

ROW_TILE = 512
FFN_ROWS = 512
FFN_ROWS_SAMPLE = 1024
FFN_TILE_F = 512
FFN_EW_ROWS = 256
MM_TILE_N = 512
ADA_TILE_N = 1024
RW_PROJ_ROWS = 512
RW_PROJ_TILE_N = 256
RW_SCAN_TBLK = 256
RW_SCAN_HEADS_PROMPT = 8
RW_SCAN_HEADS_SAMPLE = 16
GLA_SCAN_TBLK = 256
GLA_IN_ROWS = 1024
GLA_SCAN_HEADS_PROMPT = 2
GLA_SCAN_HEADS_SAMPLE = 4
S5_TBLK = 128
S5_GLU_ROWS = 256
S5_SEQS_SAMPLE = 8
S5_CARRY_ELEMS = 8192


def _cparams(*sem):
    return pltpu.CompilerParams(dimension_semantics=sem, vmem_limit_bytes=VMEM_LIMIT_BYTES)


def _dot(a, b):
    return jnp.dot(a, b, preferred_element_type=f32)


def _dot_hi(a, b):
    return jnp.dot(a, b, precision=HI, preferred_element_type=f32)


def _dot_nt_hi(a, b):
    return lax.dot_general(a, b, (((1,), (1,)), ((), ())), precision=HI, preferred_element_type=f32)


def _dot_tn_hi(a, b):
    return lax.dot_general(a, b, (((0,), (0,)), ((), ())), precision=HI, preferred_element_type=f32)


def _normmod(x, g, shift, scale):
    ms = jnp.mean(x * x, axis=-1, keepdims=True)
    y = x * lax.rsqrt(ms + EPS) * g
    return y * (1.0 + scale) + shift


def _ada_kernel(c_ref, w_ref, b_ref, o_ref):
    c = c_ref[...]
    s = (c * jax.nn.sigmoid(c)).astype(bf16)
    o_ref[0, 0] = _dot(s, w_ref[0].astype(bf16)) + b_ref[0]


def _ada_call(c_all, ada_w, ada_b, d_model):
    depth, d, n = ada_w.shape
    m = c_all.shape[0]
    tn = min(ADA_TILE_N, d_model)
    per_vec = d_model // tn
    return pl.pallas_call(
        _ada_kernel,
        grid=(depth, n // tn),
        in_specs=[pl.BlockSpec((m, d), lambda l, j: (0, 0)),
                  pl.BlockSpec((1, d, tn), lambda l, j: (l, 0, j)),
                  pl.BlockSpec((1, 1, tn), lambda l, j: (l, 0, j))],
        out_specs=pl.BlockSpec((1, 1, m, tn), lambda l, j: (l, j // per_vec, 0, j % per_vec)),
        out_shape=jax.ShapeDtypeStruct((depth, n // d_model, m, d_model), f32),
        compiler_params=_cparams("arbitrary", "arbitrary"),
        name="ada",
    )(c_all, ada_w, ada_b.reshape(depth, 1, n))


def _ffn_kernel(x_ref, g_ref, sh_ref, sc_ref, gt_ref, wg_ref, wu_ref, wo_ref, o_ref, h_scr, acc_scr):
    j = pl.program_id(1)
    nb, tt, d = x_ref.shape
    if nb > 1:
        step = max(1, min(nb, FFN_EW_ROWS // tt))
        chunks = [(slice(a, a + step), slice(None), a * tt, step * tt) for a in range(0, nb, step)]
    else:
        step = min(tt, FFN_EW_ROWS)
        chunks = [(slice(None), slice(a, a + step), a, step) for a in range(0, tt, step)]

    @pl.when(j == 0)
    def _():
        for s0, s1, r0, nr in chunks:
            h = _normmod(x_ref[s0, s1, :], g_ref[...], _seq_vec(sh_ref), _seq_vec(sc_ref))
            h_scr[r0:r0 + nr, :] = h.reshape(nr, d).astype(bf16)

    @pl.when(j == 0)
    def _():
        acc_scr[...] = jnp.zeros_like(acc_scr)

    hb = h_scr[...]
    gate = _dot(hb, wg_ref[...])
    up = _dot(hb, wu_ref[...])
    act = (gate * jax.nn.sigmoid(gate) * up).astype(bf16)
    acc_scr[...] += _dot(act, wo_ref[...])

    @pl.when(j == pl.num_programs(1) - 1)
    def _():
        for s0, s1, r0, nr in chunks:
            shp = x_ref[s0, s1, :].shape
            o_ref[s0, s1, :] = x_ref[s0, s1, :] + 0.5 * _seq_vec(gt_ref) * acc_scr[r0:r0 + nr, :].reshape(shp)


def _ffn_call(x, g, shift, scale, gate, w_in, w_out, l, s, nb, tt):
    d = x.shape[-1]
    f = w_out.shape[2]
    tf = min(FFN_TILE_F, f)
    nf = f // tf
    til = _row_tiling(x.shape, nb, tt)
    row_mode = dict(pipeline_mode=pl.Buffered(1)) if til.grid == 1 else {}
    rows = pl.BlockSpec(til.block(d), til.rmap, **row_mode)
    mod = til.mod_spec(d)
    return pl.pallas_call(
        _ffn_kernel,
        grid=(til.grid, nf),
        in_specs=[rows, pl.BlockSpec((1, 1, d), lambda i, j: (0, 0, 0)), mod, mod, mod,
                  pl.BlockSpec((None, None, d, tf), lambda i, j: (l, s, 0, j)),
                  pl.BlockSpec((None, None, d, tf), lambda i, j: (l, s, 0, nf + j)),
                  pl.BlockSpec((None, None, tf, d), lambda i, j: (l, s, j, 0))],
        out_specs=rows,
        out_shape=jax.ShapeDtypeStruct(x.shape, f32),
        scratch_shapes=[pltpu.VMEM((til.rows, d), bf16), pltpu.VMEM((til.rows, d), f32)],
        compiler_params=_cparams("parallel", "arbitrary"),
        name="ffn",
    )(x, g, _dense_seq_vec(shift, nb), _dense_seq_vec(scale, nb), _dense_seq_vec(gate, nb), w_in, w_in, w_out)


class _RowTiling:
    def __init__(self, shape, nb, tt):
        self.nb = nb
        if nb > 1:
            t, nseq, _ = shape
            assert tt == t and nseq % nb == 0, (shape, nb, tt)
            self.grid, self.rows = nseq // nb, t * nb
            self.block = lambda w: (t, nb, w)
            self.rmap = lambda i, j: (0, i, 0)
            self.cmap = lambda i, j: (0, i, j)
            self.mod_spec = lambda w, **kw: pl.BlockSpec((nb, w), lambda i, j: (i, 0), **kw)
            self.emod_spec = lambda w: pl.BlockSpec((nb, w), lambda i, j: (i, j))
        else:
            nseq, t, _ = shape
            per = t // tt
            assert per * tt == t, (shape, tt)
            self.per = per
            self.grid, self.rows = nseq * per, tt
            self.block = lambda w: (1, tt, w)
            self.rmap = lambda i, j: (i // per, i % per, 0)
            self.cmap = lambda i, j: (i // per, i % per, j)
            self.mod_spec = lambda w, **kw: pl.BlockSpec((1, 1, w), lambda i, j: (i // per, 0, 0), **kw)
            self.emod_spec = lambda w: pl.BlockSpec((1, 1, w), lambda i, j: (i // per, 0, j))


def _row_tiling(shape, nb, tt):
    return _RowTiling(shape, nb, tt)


def _dense_seq_vec(a, nb):
    return a.reshape(a.shape[0], a.shape[-1]) if nb > 1 else a


def _seq_vec(ref):
    v = ref[...]
    return v if v.ndim == 3 else v[None, :, :]


def _mm_kernel(*refs, n_row, n_mod, n_vec, n_w, n_erow, n_emod, n_evec, n_out, prologue, epilogue):
    pos = 0

    def take(n):
        nonlocal pos
        out = refs[pos:pos + n]
        pos += n
        return out

    rows, mods, vecs, ws = take(n_row), take(n_mod), take(n_vec), take(n_w)
    erows, emods, evecs, outs = take(n_erow), take(n_emod), take(n_evec), take(n_out)
    a_scr = refs[pos]
    nb, tt, k = rows[0].shape
    tn = ws[0].shape[-1]

    @pl.when(pl.program_id(1) == 0)
    def _():
        a = prologue([r[...] for r in rows], [_seq_vec(m) for m in mods], [v[...] for v in vecs])
        a_scr[...] = a.reshape(nb * tt, k).astype(bf16)

    ab = a_scr[...]
    accs = [_dot(ab, w[...]).reshape(nb, tt, tn) for w in ws]
    res = epilogue(accs, [r[...] for r in erows], [_seq_vec(m) for m in emods], [v[...] for v in evecs])
    for o_ref, o in zip(outs, res):
        o_ref[...] = o


def _mm_call(name, rows, mods, vecs, ws, erows, emods, evecs, n_total, n_out, prologue, epilogue, nb, tt, tn=MM_TILE_N,
             resident_w=False):
    k = rows[0].shape[-1]
    tn = min(tn, n_total)
    assert n_total % tn == 0, (name, n_total, tn)
    nj = n_total // tn
    w_mode = dict(pipeline_mode=pl.Buffered(1)) if resident_w else {}
    assert not resident_w or nj == 1
    til = _row_tiling(rows[0].shape, nb, tt)
    mods = [_dense_seq_vec(m, nb) for m in mods]
    emods = [_dense_seq_vec(m, nb) for m in emods]
    in_specs = ([pl.BlockSpec(til.block(k), til.rmap)] * len(rows)
                + [til.mod_spec(k)] * len(mods)
                + [pl.BlockSpec((1, 1, k), lambda i, j: (0, 0, 0))] * len(vecs)
                + [pl.BlockSpec((k, tn), functools.partial(lambda i, j, off: (0, off + j), off=off), **w_mode)
                   for _, off in ws]
                + [pl.BlockSpec(til.block(tn), til.cmap)] * len(erows)
                + [til.emod_spec(tn)] * len(emods)
                + [pl.BlockSpec((1, 1, tn), lambda i, j: (0, 0, j))] * len(evecs))
    kern = functools.partial(_mm_kernel, n_row=len(rows), n_mod=len(mods), n_vec=len(vecs), n_w=len(ws),
                             n_erow=len(erows), n_emod=len(emods), n_evec=len(evecs), n_out=n_out,
                             prologue=prologue, epilogue=epilogue)
    out = pl.pallas_call(
        kern,
        grid=(til.grid, nj),
        in_specs=in_specs,
        out_specs=[pl.BlockSpec(til.block(tn), til.cmap)] * n_out,
        out_shape=[jax.ShapeDtypeStruct(rows[0].shape[:2] + (n_total,), f32)] * n_out,
        scratch_shapes=[pltpu.VMEM((til.rows, k), bf16)],
        compiler_params=_cparams("parallel", "arbitrary"),
        name=name,
    )(*rows, *mods, *vecs, *[w for w, _ in ws], *erows, *emods, *evecs)
    return out


def _pro_normmod(rows, mods, vecs):
    return _normmod(rows[0], vecs[0], mods[0], mods[1])


def _pro_mul(rows, mods, vecs):
    return rows[0] * rows[1]


def _pro_id(rows, mods, vecs):
    return rows[0]


def _epi_id(accs, erows, emods, evecs):
    return accs


def _epi_residual(accs, erows, emods, evecs):
    return [erows[0] + emods[0] * accs[0]]


def _rms_kernel(x_ref, g_ref, o_ref):
    x = x_ref[...]
    o_ref[...] = x * lax.rsqrt(jnp.mean(x * x, axis=-1, keepdims=True) + EPS) * g_ref[...]


def _final_norm_call(x, g, nb, tt):
    d = x.shape[-1]
    til = _row_tiling(x.shape, nb, tt)
    rmap = lambda i: til.rmap(i, 0)
    return pl.pallas_call(
        _rms_kernel,
        grid=(til.grid,),
        in_specs=[pl.BlockSpec(til.block(d), rmap), pl.BlockSpec((1, 1, d), lambda i: (0, 0, 0))],
        out_specs=pl.BlockSpec(til.block(d), rmap),
        out_shape=jax.ShapeDtypeStruct(x.shape, f32),
        compiler_params=_cparams("parallel"),
        name="final_norm",
    )(x, g)


def _rwproj_kernel(x_ref, xprev_ref, sp_ref, ng_ref, sh_ref, sc_ref, mu_ref, w1_ref, a1_ref, g1_ref,
                   wr_ref, wk_ref, wv_ref, w2_ref, a2_ref, g2_ref, w0_ref, a0_ref,
                   r_ref, k_ref, v_ref, lw_ref, a_ref, g_ref, hl_ref, xm_scr, tw_scr, ta_scr, tg_scr, *, tiles_per_seq,
                   time_major):
    b0, b1, d = x_ref.shape
    rows = b0 * b1
    tn = wr_ref.shape[-1]

    @pl.when(pl.program_id(1) == 0)
    def _():
        ng, sh, sc = ng_ref[...], _seq_vec(sh_ref), _seq_vec(sc_ref)
        h3 = _normmod(x_ref[...], ng, sh, sc)
        first = _seq_vec(sp_ref)
        if time_major:
            hp3 = jnp.concatenate([first, h3[:-1]], axis=0)
            hl_ref[...] = h3[b0 - 1:b0]
        else:
            if tiles_per_seq > 1:
                h_before = _normmod(xprev_ref[...], ng, sh, sc)[:, SUBLANE - 1:SUBLANE, :]
                first = jnp.where(pl.program_id(0) % tiles_per_seq == 0, first, h_before)
            tok = lax.broadcasted_iota(jnp.int32, h3.shape, 1)
            hp3 = jnp.where(tok == 0, first, pltpu.roll(h3, 1, axis=1))
            hl_ref[...] = h3[:, b1 - SUBLANE:, :]
        h = h3.reshape(rows, d)
        dlt = hp3.reshape(rows, d) - h
        mu = mu_ref[...]
        for p in range(3):
            xm_scr[p] = (h + dlt * mu[p:p + 1]).astype(bf16)
        xw = (h + dlt * mu[3:4]).astype(bf16)
        tw_scr[...] = jnp.tanh(_dot(xw, w1_ref[...])).astype(bf16)
        xa = (h + dlt * mu[4:5]).astype(bf16)
        ta_scr[...] = _dot(xa, a1_ref[...]).astype(bf16)
        xg = (h + dlt * mu[5:6]).astype(bf16)
        tg_scr[...] = jax.nn.sigmoid(_dot(xg, g1_ref[...])).astype(bf16)

    shp = (b0, b1, tn)
    r_ref[...] = _dot(xm_scr[0], wr_ref[...]).reshape(shp)
    k_ref[...] = _dot(xm_scr[1], wk_ref[...]).reshape(shp)
    v_ref[...] = _dot(xm_scr[2], wv_ref[...]).reshape(shp)
    w_log = -jax.nn.softplus(-(w0_ref[0] + _dot(tw_scr[...], w2_ref[...]))) - 0.5
    lw_ref[...] = (-jnp.exp(w_log)).reshape(shp)
    a_ref[...] = jax.nn.sigmoid(a0_ref[0] + _dot(ta_scr[...], a2_ref[...])).reshape(shp)
    g_ref[...] = _dot(tg_scr[...], g2_ref[...]).reshape(shp)


def _rwproj_call(x, shift_prev, ng, shift, scale, mu, w1, a1, g1, wr, wk, wv, w2, a2, g2, w0, a0, nb, tt):
    d = x.shape[-1]
    tn = min(RW_PROJ_TILE_N, d)
    til = _row_tiling(x.shape, nb, tt)
    if nb > 1:
        per = 1
        pmap = til.rmap
        prev_block = til.block(d)
        tail_spec = pl.BlockSpec((1, nb, d), til.rmap)
        tail_shape = (1, x.shape[1], d)
    else:
        per = til.per
        pmap = lambda i, j: (i // per, jnp.maximum((i % per) * (tt // SUBLANE) - 1, 0), 0)
        prev_block = (1, SUBLANE, d)
        tail_spec = pl.BlockSpec((1, SUBLANE, d), til.rmap)
        tail_shape = (x.shape[0], per * SUBLANE, d)
    mod = til.mod_spec(d)
    full = lambda a: pl.BlockSpec(a.shape, lambda i, j: (0,) * a.ndim)
    col = lambda a: pl.BlockSpec((a.shape[0], tn), lambda i, j: (0, j))
    shift_prev, shift, scale = (_dense_seq_vec(a, nb) for a in (shift_prev, shift, scale))
    rows = til.rows
    kern = functools.partial(_rwproj_kernel, tiles_per_seq=per, time_major=nb > 1)
    return pl.pallas_call(
        kern,
        grid=(til.grid, d // tn),
        in_specs=[pl.BlockSpec(til.block(d), til.rmap), pl.BlockSpec(prev_block, pmap), mod,
                  pl.BlockSpec((1, 1, d), lambda i, j: (0, 0, 0)), mod, mod, full(mu), full(w1), full(a1), full(g1),
                  col(wr), col(wk), col(wv), col(w2), col(a2), col(g2),
                  pl.BlockSpec((1, 1, tn), lambda i, j: (0, 0, j)), pl.BlockSpec((1, 1, tn), lambda i, j: (0, 0, j))],
        out_specs=[pl.BlockSpec(til.block(tn), til.cmap)] * 6 + [tail_spec],
        out_shape=[jax.ShapeDtypeStruct(x.shape, f32)] * 6 + [jax.ShapeDtypeStruct(tail_shape, f32)],
        scratch_shapes=[pltpu.VMEM((3, rows, d), bf16), pltpu.VMEM((rows, w1.shape[1]), bf16),
                        pltpu.VMEM((rows, a1.shape[1]), bf16), pltpu.VMEM((rows, g1.shape[1]), bf16)],
        compiler_params=_cparams("parallel", "arbitrary"),
        name="rwkv_proj",
    )(x, x, shift_prev, ng, shift, scale, mu, w1, a1, g1, wr, wk, wv, w2, a2, g2, w0, a0)


def _dot_nt(a, b):
    return lax.dot_general(a, b, (((1,), (1,)), ((), ())), preferred_element_type=f32)


def _dot_tn(a, b):
    return lax.dot_general(a, b, (((0,), (0,)), ((), ())), preferred_element_type=f32)


def _rwkv_chunk_kernel(r_ref, k_ref, v_ref, lw_ref, a_ref, kk_ref, ka_ref, rk_ref, gw_ref, gb_ref, s0_ref,
                       y_ref, sout_ref, s_scr, rp_scr, y0_scr, bon_scr, m_scr, n_scr, *, nb, tt, n_inner):
    c = pl.program_id(2)
    n = RW_HEAD
    big = nb * tt
    n_dbl = max(1, math.ceil(math.log2(tt)))

    @pl.when(c == 0)
    def _():
        s_scr[...] = s0_ref[...]

    ti = lax.broadcasted_iota(jnp.int32, (big, big), 0)
    si = lax.broadcasted_iota(jnp.int32, (big, big), 1)
    same = (ti // tt) == (si // tt)
    incl = jnp.logical_and(same, si <= ti)
    strict = jnp.logical_and(same, si < ti)
    cumsum = _make_masked_sum(lambda t_, s_: jnp.logical_and((t_ // tt) == (s_ // tt), s_ <= t_), big)
    seqsum = _make_masked_sum(lambda t_, s_: (t_ // tt) == (s_ // tt), big)
    eye_f = (ti == si).astype(f32)
    eye_n = (lax.broadcasted_iota(jnp.int32, (n, n), 0) == lax.broadcasted_iota(jnp.int32, (n, n), 1)).astype(f32)

    kkp, kap, rkp, gwp, gbp = kk_ref[0], ka_ref[0], rk_ref[0], gw_ref[0], gb_ref[0]

    hp = r_ref.shape[-1] // n

    def load(ref, ci):
        if nb > 1:
            return ref[...].reshape(big, hp * n)
        return ref[0, ci * big:(ci + 1) * big, :]

    ch = []
    for ci in range(n_inner):
        r, k, v, lw, a = (load(ref, ci) for ref in (r_ref, k_ref, v_ref, lw_ref, a_ref))
        cum = cumsum(lw)
        tot = cum[big - 1:big, :] if nb == 1 else seqsum(lw)
        g_in = jnp.exp(cum)
        g_prev = jnp.exp(cum - lw)
        g_inv = jnp.exp(-cum)
        g_rest = jnp.exp(tot - cum)
        g_tot = jnp.exp(tot)
        kk = k * kkp
        k2 = k * (1.0 + (a - 1.0) * kap)
        for hh in range(hp):
            sl = slice(hh * n, (hh + 1) * n)
            kk_h = kk[:, sl]
            kk_h = kk_h * jnp.minimum(lax.rsqrt(jnp.sum(kk_h * kk_h, axis=1, keepdims=True)), 1e12)
            b_h = kk_h * a[:, sl]
            r_h, k_h, v_h = r[:, sl], k2[:, sl], v[:, sl]
            rt = r_h * g_in[:, sl]
            at = -kk_h * g_prev[:, sl]
            bon_scr[ci, hh] = jnp.sum(r_h * k_h * rkp[:, sl], axis=1, keepdims=True) * v_h
            ch.append(dict(
                ci=ci, hh=hh, rt=rt, at=at, v=v_h, vb=v_h.astype(bf16), gt=g_tot[:, sl],
                bhat=b_h * g_rest[:, sl], khat=k_h * g_rest[:, sl],
                ra=jnp.concatenate([rt, at], axis=0).astype(bf16),
                bk=jnp.concatenate([b_h * g_inv[:, sl], k_h * g_inv[:, sl]], axis=0).astype(bf16)))
    for q in ch:
        q["amat"] = _dot_nt(q["ra"], q["bk"])
    for q in ch:
        amat = q["amat"]
        q["a_rb"] = jnp.where(incl, amat[:big, :big], 0.0).astype(bf16)
        q["a_rk"] = jnp.where(incl, amat[:big, big:], 0.0).astype(bf16)
        q["pw"] = jnp.where(strict, amat[big:, :big], 0.0)
        q["a_ak"] = jnp.where(strict, amat[big:, big:], 0.0).astype(bf16)
        q["tm"] = eye_f + q["pw"]
    for _ in range(n_dbl - 1):
        for q in ch:
            pwb = q["pw"].astype(bf16)
            q["pw"] = _dot(pwb, pwb)
        for q in ch:
            q["tm"] = q["tm"] + _dot(q["tm"].astype(bf16), q["pw"].astype(bf16))
    for q in ch:
        q["akv"] = _dot(q["a_ak"], q["vb"])
    for q in ch:
        q["wa"] = _dot(q["tm"].astype(bf16), jnp.concatenate([q["akv"], q["at"]], axis=1).astype(bf16))
    for q in ch:
        q["aw"] = _dot(q["a_rb"], q["wa"].astype(bf16))
        q["arkv"] = _dot(q["a_rk"], q["vb"])
    for q in ch:
        ci, hh, wa = q["ci"], q["hh"], q["wa"]
        rp_scr[ci, hh] = q["rt"] + q["aw"][:, n:]
        y0_scr[ci, hh] = q["aw"][:, :n] + q["arkv"]
        for i in range(nb):
            rows = slice(i * tt, (i + 1) * tt)
            bh_i = q["bhat"][rows].astype(bf16)
            m_scr[ci, hh, i] = eye_n * q["gt"][i * tt:i * tt + 1] + _dot_tn(wa[rows, n:].astype(bf16), bh_i)
            n_scr[ci, hh, i] = _dot_tn(jnp.concatenate([wa[rows, :n], q["v"][rows]], axis=0).astype(bf16),
                                       jnp.concatenate([q["bhat"][rows], q["khat"][rows]], axis=0).astype(bf16))

    for ci in range(n_inner):
        ys = []
        for hh in range(hp):
            sl = slice(hh * n, (hh + 1) * n)
            rp = rp_scr[ci, hh]
            parts = []
            for i in range(nb):
                sb = s_scr[i, hh].astype(bf16)
                parts.append(_dot_nt(rp[i * tt:(i + 1) * tt].astype(bf16), sb))
                s_scr[i, hh] = _dot(sb, m_scr[ci, hh, i].astype(bf16)) + n_scr[ci, hh, i]
            y = (parts[0] if nb == 1 else jnp.concatenate(parts, axis=0)) + y0_scr[ci, hh]
            mean = jnp.mean(y, axis=1, keepdims=True)
            var = jnp.mean(jnp.square(y - mean), axis=1, keepdims=True)
            ys.append((y - mean) * lax.rsqrt(var + RW_GN_EPS) * gwp[:, sl] + gbp[:, sl] + bon_scr[ci, hh])
        yo = jnp.concatenate(ys, axis=1)
        if nb > 1:
            y_ref[...] = yo.reshape(nb, tt, hp * n)
        else:
            y_ref[0, ci * big:(ci + 1) * big, :] = yo

    @pl.when(c == pl.num_programs(2) - 1)
    def _():
        sout_ref[...] = s_scr[...]


def _rwkv_pair_kernel(r_ref, k_ref, v_ref, lw_ref, a_ref, kk_ref, ka_ref, rk_ref, gw_ref, gb_ref, s0_ref,
                      y_ref, sout_ref, s_scr, rp_scr, y0_scr, bon_scr, m_scr, n_scr, *, nb, tt, n_inner):
    c = pl.program_id(2)
    n = RW_HEAD
    w2 = 2 * n
    big = nb * tt
    n_dbl = max(1, math.ceil(math.log2(tt)))
    pp = r_ref.shape[-1] // w2

    lane_r = lax.broadcasted_iota(jnp.int32, (w2, w2), 0)
    lane_c = lax.broadcasted_iota(jnp.int32, (w2, w2), 1)
    bd = (lane_r // n) == (lane_c // n)
    eye_w = (lane_r == lane_c).astype(f32)

    @pl.when(c == 0)
    def _():
        s_scr[...] = jnp.zeros_like(s_scr)
        for i in range(nb):
            for p in range(pp):
                s_scr[i, p, 0:n, 0:n] = s0_ref[i, 2 * p]
                s_scr[i, p, n:w2, n:w2] = s0_ref[i, 2 * p + 1]

    ti = lax.broadcasted_iota(jnp.int32, (big, w2), 0)
    si = lax.broadcasted_iota(jnp.int32, (big, w2), 1) % big
    same2 = (ti // tt) == (si // tt)
    incl2 = jnp.logical_and(same2, si <= ti)
    strict2 = jnp.logical_and(same2, si < ti)
    upper = lax.broadcasted_iota(jnp.int32, (big, w2), 1) >= big
    cumsum = _make_masked_sum(lambda t_, s_: jnp.logical_and((t_ // tt) == (s_ // tt), s_ <= t_), big)
    seqsum = _make_masked_sum(lambda t_, s_: (t_ // tt) == (s_ // tt), big)
    eye_f = (lax.broadcasted_iota(jnp.int32, (big, big), 0) == lax.broadcasted_iota(jnp.int32, (big, big), 1)).astype(f32)
    head_a = lax.broadcasted_iota(jnp.int32, (big, w2), 1) < n
    zeros_lw = jnp.zeros((big, w2), bf16)

    def head_sum(x):
        s_a = jnp.sum(jnp.where(head_a, x, 0.0), axis=1, keepdims=True)
        s_b = jnp.sum(jnp.where(head_a, 0.0, x), axis=1, keepdims=True)
        return jnp.where(head_a, s_a, s_b)

    def load(ref, ci):
        if nb > 1:
            return ref[...].reshape(big, pp * w2)
        return ref[0, ci * big:(ci + 1) * big, :]

    ch, slabs = [], []
    for ci in range(n_inner):
        r, k, v, lw, a = (load(ref, ci) for ref in (r_ref, k_ref, v_ref, lw_ref, a_ref))
        cum = cumsum(lw)
        tot = cum[big - 1:big, :] if nb == 1 else seqsum(lw)
        for p in range(pp):
            sl = slice(p * w2, (p + 1) * w2)
            r_p, k_p, v_p, lw_p, a_p, cum_p, tot_p = r[:, sl], k[:, sl], v[:, sl], lw[:, sl], a[:, sl], cum[:, sl], tot[:, sl]
            g_inv = jnp.exp(-cum_p)
            g_rest = jnp.exp(tot_p - cum_p)
            kk = k_p * kk_ref[0][:, sl]
            kk = kk * jnp.minimum(lax.rsqrt(head_sum(kk * kk)), 1e12)
            k2 = k_p * (1.0 + (a_p - 1.0) * ka_ref[0][:, sl])
            b = kk * a_p
            rt = r_p * jnp.exp(cum_p)
            at = -kk * jnp.exp(cum_p - lw_p)
            bhat, khat = b * g_rest, k2 * g_rest
            bon_scr[ci, p] = head_sum(r_p * k2 * rk_ref[0][:, sl]) * v_p
            slab = dict(ci=ci, p=p, bhat=bhat, khat=khat, v=v_p, gt=jnp.exp(tot_p),
                        bk=jnp.concatenate([b * g_inv, k2 * g_inv], axis=0).astype(bf16))
            slabs.append(slab)
            for hd in range(2):
                mh = head_a if hd == 0 else jnp.logical_not(head_a)
                at_h = jnp.where(mh, at, 0.0)
                rt_h = jnp.where(mh, rt, 0.0)
                ch.append(dict(slab=slab, rt=rt_h, at=at_h,
                               ra=jnp.concatenate([rt_h, at_h], axis=0).astype(bf16),
                               vv=jnp.concatenate([zeros_lw, jnp.where(mh, v_p, 0.0).astype(bf16)], axis=0)))
    for q in ch:
        q["amat"] = _dot_nt(q["ra"], q["slab"]["bk"])
    for q in ch:
        amat = q["amat"]
        q["top"] = jnp.where(incl2, amat[:big], 0.0).astype(bf16)
        bot = jnp.where(strict2, amat[big:], 0.0)
        q["bot_k"] = jnp.where(upper, bot, 0.0).astype(bf16)
        q["pw"] = bot[:, :big]
        q["tm"] = eye_f + q["pw"]
    for _ in range(n_dbl - 1):
        for q in ch:
            pwb = q["pw"].astype(bf16)
            q["pw"] = _dot(pwb, pwb)
        for q in ch:
            q["tm"] = q["tm"] + _dot(q["tm"].astype(bf16), q["pw"].astype(bf16))
    for q in ch:
        q["akv"] = _dot(q["bot_k"], q["vv"])
    for q in ch:
        q["wa"] = _dot(q["tm"].astype(bf16), jnp.concatenate([q["akv"], q["at"]], axis=1).astype(bf16))
    for q in ch:
        wab = q["wa"].astype(bf16)
        q["arb_a"] = _dot(q["top"][:, :big], wab[:, w2:])
        q["y0"] = _dot(q["top"], jnp.concatenate([wab[:, :w2], q["vv"][big:]], axis=0))
    for idx, slab in enumerate(slabs):
        qa, qb = ch[2 * idx], ch[2 * idx + 1]
        ci, p = slab["ci"], slab["p"]
        rp_scr[ci, p] = qa["rt"] + qb["rt"] + qa["arb_a"] + qb["arb_a"]
        y0_scr[ci, p] = qa["y0"] + qb["y0"]
        wa = qa["wa"] + qb["wa"]
        w_pair, a_pair = wa[:, :w2], wa[:, w2:]
        for i in range(nb):
            rows = slice(i * tt, (i + 1) * tt)
            bh_i = slab["bhat"][rows].astype(bf16)
            m_scr[ci, p, i] = eye_w * slab["gt"][i * tt:i * tt + 1] + jnp.where(bd, _dot_tn(a_pair[rows].astype(bf16), bh_i), 0.0)
            n_scr[ci, p, i] = jnp.where(bd, _dot_tn(
                jnp.concatenate([w_pair[rows], slab["v"][rows]], axis=0).astype(bf16),
                jnp.concatenate([slab["bhat"][rows], slab["khat"][rows]], axis=0).astype(bf16)), 0.0)

    for ci in range(n_inner):
        ys = []
        for p in range(pp):
            sl = slice(p * w2, (p + 1) * w2)
            rp = rp_scr[ci, p]
            parts = []
            for i in range(nb):
                sb = s_scr[i, p].astype(bf16)
                parts.append(_dot_nt(rp[i * tt:(i + 1) * tt].astype(bf16), sb))
                s_scr[i, p] = _dot(sb, m_scr[ci, p, i].astype(bf16)) + n_scr[ci, p, i]
            y = (parts[0] if nb == 1 else jnp.concatenate(parts, axis=0)) + y0_scr[ci, p]
            mean = head_sum(y) * (1.0 / n)
            var = head_sum(jnp.square(y - mean)) * (1.0 / n)
            ys.append((y - mean) * lax.rsqrt(var + RW_GN_EPS) * gw_ref[0][:, sl] + gb_ref[0][:, sl] + bon_scr[ci, p])
        yo = ys[0] if pp == 1 else jnp.concatenate(ys, axis=1)
        if nb > 1:
            y_ref[...] = yo.reshape(nb, tt, pp * w2)
        else:
            y_ref[0, ci * big:(ci + 1) * big, :] = yo

    @pl.when(c == pl.num_programs(2) - 1)
    def _():
        for i in range(nb):
            for p in range(pp):
                sout_ref[i, 2 * p] = s_scr[i, p, 0:n, 0:n]
                sout_ref[i, 2 * p + 1] = s_scr[i, p, n:w2, n:w2]


def _rwscan_call(r, k, v, lw, a, k_k, k_a, r_k, gn_w, gn_b, s0_all, s_new_all, layer, nb, tt, tblk, hp):
    nseq, t, d = r.shape
    nh = d // RW_HEAD
    lanes = hp * RW_HEAD
    nc = t // tblk
    n_inner = tblk // tt if nb == 1 else 1
    xspec = pl.BlockSpec((nb, tblk, lanes), lambda i, p, c: (i, c, p))
    pspec = pl.BlockSpec((1, 1, lanes), lambda i, p, c: (0, 0, p))
    sspec = pl.BlockSpec((nb, hp, RW_HEAD, RW_HEAD), lambda i, p, c: (i, p, 0, 0))
    assert hp % 2 == 0 and nb * tt == RW_HEAD, (hp, nb, tt)
    pp, w2 = hp // 2, 2 * RW_HEAD
    base = functools.partial(_rwkv_pair_kernel, nb=nb, tt=tt, n_inner=n_inner)
    chunk_rows = pltpu.VMEM((n_inner, pp, nb * tt, w2), f32)
    chunk_mats = pltpu.VMEM((n_inner, pp, nb, w2, w2), f32)
    layer_spec = pl.BlockSpec((None, nb, hp, RW_HEAD, RW_HEAD), lambda i, p, c: (layer, i, p, 0, 0))
    n_in = 11
    if s_new_all is None:
        kern, extra_specs, extra_args, aliases = base, [], [], {}
    else:
        kern = lambda *refs: base(*refs[:n_in], *refs[n_in + 1:])
        extra_specs, extra_args, aliases = [pl.BlockSpec(memory_space=pl.ANY)], [s_new_all], {n_in: 1}
    return pl.pallas_call(
        kern,
        grid=(nseq // nb, nh // hp, nc),
        in_specs=[xspec] * 5 + [pspec] * 5 + [layer_spec] + extra_specs,
        out_specs=[xspec, layer_spec],
        out_shape=[jax.ShapeDtypeStruct((nseq, t, d), f32), jax.ShapeDtypeStruct(s0_all.shape, f32)],
        input_output_aliases=aliases,
        scratch_shapes=[pltpu.VMEM((nb, pp, w2, w2), f32), chunk_rows, chunk_rows, chunk_rows,
                        chunk_mats, chunk_mats],
        compiler_params=_cparams("parallel", "parallel", "arbitrary"),
        name="rwkv_scan",
    )(r, k, v, lw, a, k_k, k_a, r_k, gn_w, gn_b, s0_all, *extra_args)


def _split3(x):
    hi = x.astype(bf16)
    r1 = x - hi.astype(f32)
    mid = r1.astype(bf16)
    return hi, mid, (r1 - mid.astype(f32)).astype(bf16)


def _make_masked_sum(mask_fn, rows):
    if (3 * rows) % 16:
        ti = lax.broadcasted_iota(jnp.int32, (rows, rows), 0)
        si = lax.broadcasted_iota(jnp.int32, (rows, rows), 1)
        mask_f = mask_fn(ti, si).astype(f32)
        return lambda x: _dot_hi(mask_f, x)
    ti = lax.broadcasted_iota(jnp.int32, (rows, 3 * rows), 0)
    si = lax.broadcasted_iota(jnp.int32, (rows, 3 * rows), 1) % rows
    mask3 = jnp.where(mask_fn(ti, si), 1.0, 0.0).astype(bf16)
    return lambda x: _dot(mask3, jnp.concatenate(_split3(x), axis=0))


def _glascan_kernel(q_ref, k_ref, v_ref, r_ref, la_ref, on_ref, s0_ref, o_ref, sout_ref, s_scr, *, chunk, n_inner, hb):
    c = pl.program_id(2)
    hk = q_ref.shape[-1] // hb
    hv = v_ref.shape[-1] // hb

    @pl.when(c == 0)
    def _():
        s_scr[...] = s0_ref[0]

    ti = lax.broadcasted_iota(jnp.int32, (chunk, chunk), 0)
    si = lax.broadcasted_iota(jnp.int32, (chunk, chunk), 1)
    causal = si <= ti
    cumsum = _make_masked_sum(lambda t_, s_: s_ <= t_, chunk)

    ch = []
    for ci in range(n_inner):
        rows = slice(ci * chunk, (ci + 1) * chunk)
        for h in range(hb):
            ks, vs = slice(h * hk, (h + 1) * hk), slice(h * hv, (h + 1) * hv)
            ch.append(dict(rows=rows, h=h, vs=vs, q=q_ref[0, rows, ks], k=k_ref[0, rows, ks],
                           vb=v_ref[0, rows, vs].astype(bf16), la=la_ref[0, rows, ks]))
    for z in ch:
        z["b"] = cumsum(z["la"])
    for z in ch:
        b = z["b"]
        b_last = b[chunk - 1:chunk, :]
        z["q_in"] = (z["q"] * (hk ** -0.5) * jnp.exp(b)).astype(bf16)
        z["k_in"] = (z["k"] * jnp.exp(-b)).astype(bf16)
        z["kd"] = (z["k"] * jnp.exp(b_last - b)).astype(bf16)
        z["dec"] = jnp.transpose(jnp.broadcast_to(jnp.exp(b_last), (SUBLANE, hk)))[:, 0:1]
    for z in ch:
        z["att"] = _dot_nt(z["q_in"], z["k_in"])
    for z in ch:
        z["ov"] = _dot(jnp.where(causal, z["att"], 0.0).astype(bf16), z["vb"])
        z["inc"] = _dot_tn(z["kd"], z["vb"])

    for z in ch:
        h, rows, vs = z["h"], z["rows"], z["vs"]
        s = s_scr[h]
        o = z["ov"] + _dot(z["q_in"], s.astype(bf16))
        s_scr[h] = s * z["dec"] + z["inc"]
        o = o * lax.rsqrt(jnp.mean(o * o, axis=-1, keepdims=True) + EPS) * on_ref[...]
        rr = r_ref[0, rows, vs]
        o_ref[0, rows, vs] = o * (rr * jax.nn.sigmoid(rr))

    @pl.when(c == pl.num_programs(2) - 1)
    def _():
        sout_ref[0] = s_scr[...]


def _glascan_call(qkvr, la, o_norm, s0, tblk, hb):
    nseq, t, _ = qkvr.shape
    _, gh, hk, hv = s0.shape
    chunk = math.gcd(t, GLA_CHUNK)
    tblk = min(tblk, t)
    k_off = (gh * hk) // (hb * hk)
    v_off = (2 * gh * hk) // (hb * hv)
    r_off = (2 * gh * hk + gh * hv) // (hb * hv)
    kern = functools.partial(_glascan_kernel, chunk=chunk, n_inner=tblk // chunk, hb=hb)
    return pl.pallas_call(
        kern,
        grid=(nseq, gh // hb, t // tblk),
        in_specs=[pl.BlockSpec((1, tblk, hb * hk), lambda b, h, c: (b, c, h)),
                  pl.BlockSpec((1, tblk, hb * hk), lambda b, h, c: (b, c, k_off + h)),
                  pl.BlockSpec((1, tblk, hb * hv), lambda b, h, c: (b, c, v_off + h)),
                  pl.BlockSpec((1, tblk, hb * hv), lambda b, h, c: (b, c, r_off + h)),
                  pl.BlockSpec((1, tblk, hb * hk), lambda b, h, c: (b, c, h)),
                  pl.BlockSpec((1, hv), lambda b, h, c: (0, 0)),
                  pl.BlockSpec((1, hb, hk, hv), lambda b, h, c: (b, h, 0, 0))],
        out_specs=[pl.BlockSpec((1, tblk, hb * hv), lambda b, h, c: (b, c, h)),
                   pl.BlockSpec((1, hb, hk, hv), lambda b, h, c: (b, h, 0, 0))],
        out_shape=[jax.ShapeDtypeStruct((nseq, t, gh * hv), f32), jax.ShapeDtypeStruct(s0.shape, f32)],
        scratch_shapes=[pltpu.VMEM((hb, hk, hv), f32)],
        compiler_params=_cparams("parallel", "parallel", "arbitrary"),
        name="gla_scan",
    )(qkvr, qkvr, qkvr, qkvr, la, o_norm, s0)


def _s5prep_kernel(are_ref, aim_ref, ldt_ref, bre_ref, bim_ref, abre_ref, abim_ref, bbre_ref, bbim_ref):
    a_re, a_im = are_ref[...], aim_ref[...]
    dt = jnp.exp(ldt_ref[...])
    mag = jnp.exp(a_re * dt)
    ab_re, ab_im = mag * jnp.cos(a_im * dt), mag * jnp.sin(a_im * dt)
    den = a_re * a_re + a_im * a_im
    nr = ab_re - 1.0
    cf_re = (nr * a_re + ab_im * a_im) / den
    cf_im = (ab_im * a_re - nr * a_im) / den
    abre_ref[...] = ab_re
    abim_ref[...] = ab_im
    b_re, b_im = bre_ref[...], bim_ref[...]
    bbre_ref[...] = cf_re * b_re - cf_im * b_im
    bbim_ref[...] = cf_re * b_im + cf_im * b_re


def _s5prep_call(a_re, a_im, log_dt, b_re_t, b_im_t):
    g, n = a_re.shape
    return pl.pallas_call(
        _s5prep_kernel,
        out_shape=[jax.ShapeDtypeStruct((g, 1, n), f32)] * 2 + [jax.ShapeDtypeStruct(b_re_t.shape, f32)] * 2,
        name="s5_prep",
    )(a_re.reshape(g, 1, n), a_im.reshape(g, 1, n), log_dt.reshape(g, 1, 1), b_re_t, b_im_t)


def _s5scan_kernel(x_ref, g_ref, sh_ref, sc_ref, wbu_ref, wc_ref, abre_ref, abim_ref, dsk_ref, x0re_ref, x0im_ref,
                   y_ref, xre_ref, xim_ref, bre_scr, bim_scr, sre_scr, sim_scr, *, lane_chunk, nb):
    c = pl.program_id(1)
    d = x_ref.shape[-1]
    rows = x_ref.shape[0] * x_ref.shape[1]
    tblk = rows // nb
    n_slab = wbu_ref.shape[0]
    half = wbu_ref.shape[2] // 2
    n_state = n_slab * half

    @pl.when(c == 0)
    def _():
        sre_scr[...] = x0re_ref[0]
        sim_scr[...] = x0im_ref[0]

    h = _normmod(x_ref[...], g_ref[...], sh_ref[...], sc_ref[...]).reshape(rows, d)
    hb = h.astype(bf16)
    for s in range(n_slab):
        bu = _dot(hb[:, s * LANE:(s + 1) * LANE], wbu_ref[s])
        bre_scr[:, s * half:(s + 1) * half] = bu[:, :half]
        bim_scr[:, s * half:(s + 1) * half] = bu[:, half:]

    for q in range(n_state // lane_chunk):
        ls = slice(q * lane_chunk, (q + 1) * lane_chunk)
        ar, ai = abre_ref[:, ls], abim_ref[:, ls]

        def step(t, carry):
            xr, xi = carry
            row = pl.ds(pl.multiple_of(t * nb, nb), nb)
            nxr = ar * xr - ai * xi + bre_scr[row, ls]
            nxi = ar * xi + ai * xr + bim_scr[row, ls]
            bre_scr[row, ls] = nxr
            bim_scr[row, ls] = nxi
            return nxr, nxi

        xr, xi = lax.fori_loop(0, tblk, step, (sre_scr[:, ls], sim_scr[:, ls]))
        sre_scr[:, ls] = xr
        sim_scr[:, ls] = xi

    for s in range(n_slab):
        xs = jnp.concatenate([bre_scr[:, s * half:(s + 1) * half], bim_scr[:, s * half:(s + 1) * half]], axis=1)
        ys = _dot(xs.astype(bf16), wc_ref[s])
        sl = slice(s * LANE, (s + 1) * LANE)
        y_ref[:, :, sl] = jax.nn.gelu(ys + dsk_ref[:, sl] * h[:, sl]).reshape(y_ref.shape[0], y_ref.shape[1], LANE)

    @pl.when(c == pl.num_programs(1) - 1)
    def _():
        xre_ref[0] = sre_scr[...]
        xim_ref[0] = sim_scr[...]


def _s5scan_call(x, g, shift, scale, wbu, wc, ab_re, ab_im, d_skip, x0_re, x0_im, tblk, nb):
    if nb > 1:
        t, nseq, d = x.shape
        tblk = t
        shift, scale = shift.reshape(1, nseq, d), scale.reshape(1, nseq, d)
        xspec = pl.BlockSpec((t, nb, d), lambda b, c: (0, b, 0))
        mod = pl.BlockSpec((1, nb, d), lambda b, c: (0, b, 0))
    else:
        nseq, t, d = x.shape
        tblk = min(tblk, t)
        xspec = pl.BlockSpec((1, tblk, d), lambda b, c: (b, c, 0))
        mod = pl.BlockSpec((1, 1, d), lambda b, c: (b, 0, 0))
    n_state = ab_re.shape[1]
    full = lambda a: pl.BlockSpec(a.shape, lambda b, c: (0,) * a.ndim)
    st = pl.BlockSpec((1, nb, n_state), lambda b, c: (b, 0, 0))
    st_shape = jax.ShapeDtypeStruct((nseq // nb, nb, n_state), f32)
    kern = functools.partial(_s5scan_kernel, lane_chunk=min(n_state, S5_CARRY_ELEMS // nb), nb=nb)
    rows = nb * tblk
    return pl.pallas_call(
        kern,
        grid=(nseq // nb, t // tblk),
        in_specs=[xspec, pl.BlockSpec((1, 1, d), lambda b, c: (0, 0, 0)),
                  mod, mod, full(wbu), full(wc), full(ab_re), full(ab_im), full(d_skip), st, st],
        out_specs=[xspec, st, st],
        out_shape=[jax.ShapeDtypeStruct(x.shape, f32), st_shape, st_shape],
        scratch_shapes=[pltpu.VMEM((rows, n_state), f32), pltpu.VMEM((rows, n_state), f32),
                        pltpu.VMEM((nb, n_state), f32), pltpu.VMEM((nb, n_state), f32)],
        compiler_params=_cparams("parallel", "arbitrary"),
        name="s5_scan",
    )(x, g, shift, scale, wbu, wc, ab_re, ab_im, d_skip,
      x0_re.reshape(nseq // nb, nb, n_state), x0_im.reshape(nseq // nb, nb, n_state))


def _epi_glu(accs, erows, emods, evecs):
    val = accs[0] + evecs[0]
    gate = accs[1] + evecs[1]
    return [erows[0] + emods[0] * (val * jax.nn.sigmoid(gate))]


def _rwkv_mixer(x, g, shift, scale, gm, shift_prev, wkv_all, wkv_new_all, layer, p, nb, tt, scan_cfg):
    t, d = (x.shape[0] if nb > 1 else x.shape[1]), x.shape[-1]
    pnb, ptt = (RW_PROJ_ROWS // t, t) if nb > 1 else (1, min(RW_PROJ_ROWS, t))
    r, k, v, lw, a, gg, h_tail = _rwproj_call(x, shift_prev[:, None, :], g, shift, scale, p["mu"], p["w1"], p["a1"], p["g1"],
                                               p["wr"], p["wk"], p["wv"], p["w2"], p["a2"], p["g2"], p["w0"], p["a0"], pnb, ptt)
    seq_major = (lambda z: z.transpose(1, 0, 2)) if nb > 1 else (lambda z: z)
    y, wkv = _rwscan_call(*(seq_major(z) for z in (r, k, v, lw, a)), p["k_k"], p["k_a"], p["r_k"], p["gn_w"], p["gn_b"],
                          wkv_all, wkv_new_all, layer, *scan_cfg)
    (x_new,) = _mm_call("rwkv_out", [seq_major(y), gg], [], [], [(p["w_o"], 0)], [x], [gm], [], d, 1,
                        _pro_mul, _epi_residual, nb, tt, tn=d, resident_w=True)
    return x_new, (h_tail[0] if nb > 1 else h_tail[:, -1]), wkv


def _gla_mixer(x, g, shift, scale, gm, s0, p, nb, tt):
    d = x.shape[-1]
    n_main = p["w_main"].shape[1]
    i_nb, i_tt = (nb, tt) if nb > 1 else (1, min(x.shape[1], GLA_IN_ROWS))
    (qkvr,) = _mm_call("gla_in", [x], [shift, scale], [g], [(p["w_main"], 0)], [], [], [], n_main, 1,
                       _pro_normmod, _epi_id, i_nb, i_tt)
    (a_low,) = _mm_call("gla_gate_in", [x], [shift, scale], [g], [(p["w_gate"], 0)], [], [], [], GLA_LORA_PAD, 1,
                        _pro_normmod, _epi_id, nb, tt)

    def epi_la(accs, erows, emods, evecs):
        return [jax.nn.log_sigmoid(accs[0] + evecs[0]) / GLA_GATE_NORM]

    dk = p["a_w2"].shape[1]
    (la,) = _mm_call("gla_gate", [a_low], [], [], [(p["a_w2"], 0)], [], [], [p["a_b"]], dk, 1, _pro_id, epi_la, nb, tt)
    hb = min(s0.shape[1], GLA_SCAN_HEADS_SAMPLE if nb > 1 else GLA_SCAN_HEADS_PROMPT)
    seq_major = (lambda z: z.transpose(1, 0, 2)) if nb > 1 else (lambda z: z)
    o, s_new = _glascan_call(seq_major(qkvr), seq_major(la), p["o_norm"], s0, GLA_SCAN_TBLK, hb)
    (x_new,) = _mm_call("gla_out", [seq_major(o)], [], [], [(p["w_o"], 0)], [x], [gm], [], d, 1,
                        _pro_id, _epi_residual, nb, tt, tn=d, resident_w=True)
    return x_new, s_new


def _s5_mixer(x, g, shift, scale, gm, x0_re, x0_im, p, nb, tt):
    d = x.shape[-1]
    s5_nb = min(x.shape[1], S5_SEQS_SAMPLE) if nb > 1 else 1
    yg, xre, xim = _s5scan_call(x, g, shift, scale, p["wbu"], p["wc"], p["ab_re"], p["ab_im"], p["d_skip"],
                                x0_re, x0_im, S5_TBLK, s5_nb)
    g_nb, g_tt = (max(1, nb // 2), tt) if nb > 1 else (1, min(tt, S5_GLU_ROWS))
    (x_new,) = _mm_call("s5_glu", [yg], [], [], [(p["glu_w"], 0), (p["glu_w"], 1)], [x], [gm],
                        [p["glu_b_val"], p["glu_b_gate"]], d, 1, _pro_id, _epi_glu, g_nb, g_tt, tn=d, resident_w=True)
    return x_new, xre.reshape(x0_re.shape), xim.reshape(x0_im.shape)


def _s5_params(a_re, a_im, log_dt, b_re, b_im, c_re, c_im, d_skip, glu_w, glu_b):
    g, n, cg = b_re.shape
    d = g * cg
    ab_re, ab_im, bb_re, bb_im = _s5prep_call(a_re, a_im, log_dt, b_re.transpose(0, 2, 1), b_im.transpose(0, 2, 1))
    sg = S5_SLAB_GROUPS
    eye = jnp.eye(sg, dtype=f32)

    def bdiag(m):
        gq, rr, cc = m.shape
        m = m.reshape(gq // sg, sg, rr, cc)
        return jnp.einsum("sjrc,jk->sjrkc", m, eye).reshape(gq // sg, sg * rr, sg * cc)

    wbu = jnp.concatenate([bdiag(bb_re), bdiag(bb_im)], axis=2).astype(bf16)
    wc = jnp.concatenate([bdiag(c_re.transpose(0, 2, 1)), bdiag(-c_im.transpose(0, 2, 1))], axis=1).astype(bf16)
    return dict(wbu=wbu, wc=wc, ab_re=ab_re.reshape(1, g * n), ab_im=ab_im.reshape(1, g * n),
                d_skip=d_skip.reshape(1, d), glu_w=glu_w.astype(bf16),
                glu_b_val=glu_b[:d].reshape(1, 1, d), glu_b_gate=glu_b[d:].reshape(1, 1, d))


def _trunk(x, ada, states, weights, nb, tt, rw_scan_cfg, ffn_cfg):
    d = x.shape[-1]
    depth, _, nseq, _ = ada.shape
    wkv, shift, gla, s5_re, s5_im = states
    new = dict(shift=[], gla=[], re=[], im=[])
    new_wkv = None
    ia = ib = ic = 0
    vec = (lambda l, j: ada[l, j]) if nb > 1 else (lambda l, j: ada[l, j].reshape(nseq, 1, d))
    for l in range(depth):
        ng = lambda s: weights["norm_g"][l, s].reshape(1, 1, d)
        x = _ffn_call(x, ng(0), vec(l, 0), vec(l, 1), vec(l, 2), weights["ffn_w_in"], weights["ffn_w_out"], l, 0, *ffn_cfg)
        kind = l % 3
        if kind == 0:
            x, sh, new_wkv = _rwkv_mixer(x, ng(1), vec(l, 3), vec(l, 4), vec(l, 5), shift[ia], wkv, new_wkv, ia,
                                         weights["rw"][ia], nb, tt, rw_scan_cfg)
            new["shift"].append(sh)
            ia += 1
        elif kind == 1:
            x, st = _gla_mixer(x, ng(1), vec(l, 3), vec(l, 4), vec(l, 5), gla[ib], weights["gla"][ib], nb, tt)
            new["gla"].append(st)
            ib += 1
        else:
            x, sr, si = _s5_mixer(x, ng(1), vec(l, 3), vec(l, 4), vec(l, 5), s5_re[ic], s5_im[ic], weights["s5"][ic], nb, tt)
            new["re"].append(sr)
            new["im"].append(si)
            ic += 1
        x = _ffn_call(x, ng(2), vec(l, 6), vec(l, 7), vec(l, 8), weights["ffn_w_in"], weights["ffn_w_out"], l, 1, *ffn_cfg)
    y = _final_norm_call(x, weights["final_g"].reshape(1, 1, d), nb, tt)
    return (y, new_wkv, jnp.stack(new["shift"]), jnp.stack(new["gla"]),
            jnp.stack(new["re"]), jnp.stack(new["im"]))


def _pad_to(a, axis, size):
    pad = [(0, 0)] * a.ndim
    pad[axis] = (0, size - a.shape[axis])
    return jnp.pad(a, pad)


def kernel(x_prompt, x_sample, state_rwkv_wkv, state_rwkv_shift, state_gla, state_s5_re, state_s5_im, c_prompt, c_sample, norm_g, ada_w, ada_b, ffn_w_in, ffn_w_out, rw_mu, rw_w_rkv, rw_w0, rw_w1, rw_w2, rw_a0, rw_a1, rw_a2, rw_g1, rw_g2, rw_k_k, rw_k_a, rw_r_k, rw_gn_w, rw_gn_b, rw_w_o, gla_w_in, gla_a_w2, gla_a_b, gla_o_norm, gla_w_o, s5_A_re, s5_A_im, s5_log_dt, s5_B_re, s5_B_im, s5_C_re, s5_C_im, s5_D, s5_glu_w, s5_glu_b, final_g):
    bp, t_p, d = x_prompt.shape
    bs, t_s, _ = x_sample.shape
    depth = ada_w.shape[0]
    n_a, n_b, n_c = rw_mu.shape[0], gla_w_in.shape[0], s5_A_re.shape[0]

    n_c_rows = bp + bs
    c_all = _pad_to(jnp.concatenate([c_sample, c_prompt], axis=0), 0, -(-n_c_rows // SUBLANE) * SUBLANE)
    ada = _ada_call(c_all, ada_w, ada_b, d)
    ada_s = ada[:, :, :bs]
    ada_p = ada[:, :, bs:n_c_rows]

    rw = []
    for i in range(n_a):
        vec = lambda a: a[i].reshape(1, 1, d)
        rw.append(dict(
            mu=rw_mu[i], wr=rw_w_rkv[i, 0].astype(bf16), wk=rw_w_rkv[i, 1].astype(bf16), wv=rw_w_rkv[i, 2].astype(bf16),
            w1=_pad_to(rw_w1[i], 1, RW_LORA_PAD).astype(bf16), w2=_pad_to(rw_w2[i], 0, RW_LORA_PAD).astype(bf16),
            a1=_pad_to(rw_a1[i], 1, RW_LORA_PAD).astype(bf16), a2=_pad_to(rw_a2[i], 0, RW_LORA_PAD).astype(bf16),
            g1=rw_g1[i].astype(bf16), g2=rw_g2[i].astype(bf16), w0=vec(rw_w0), a0=vec(rw_a0),
            k_k=vec(rw_k_k), k_a=vec(rw_k_a), r_k=rw_r_k[i].reshape(1, 1, d), gn_w=vec(rw_gn_w), gn_b=vec(rw_gn_b),
            w_o=rw_w_o[i].astype(bf16)))
    gl = []
    for i in range(n_b):
        n_main = gla_w_in.shape[2] - gla_a_w2.shape[1]
        gl.append(dict(
            w_main=gla_w_in[i, :, :n_main].astype(bf16),
            w_gate=_pad_to(gla_w_in[i, :, n_main:], 1, GLA_LORA_PAD).astype(bf16),
            a_w2=_pad_to(gla_a_w2[i], 0, GLA_LORA_PAD).astype(bf16), a_b=gla_a_b[i].reshape(1, 1, -1),
            o_norm=gla_o_norm[i].reshape(1, -1), w_o=gla_w_o[i].astype(bf16)))
    s5 = [_s5_params(s5_A_re[i], s5_A_im[i], s5_log_dt[i], s5_B_re[i], s5_B_im[i], s5_C_re[i], s5_C_im[i],
                     s5_D[i], s5_glu_w[i], s5_glu_b[i]) for i in range(n_c)]
    weights = dict(norm_g=norm_g, ffn_w_in=ffn_w_in.astype(bf16), ffn_w_out=ffn_w_out.astype(bf16),
                   rw=rw, gla=gl, s5=s5, final_g=final_g)

    zeros = lambda s: jnp.zeros((s.shape[0], bp) + s.shape[2:], f32)
    p_states = tuple(zeros(s) for s in (state_rwkv_wkv, state_rwkv_shift, state_gla, state_s5_re, state_s5_im))
    s_states = (state_rwkv_wkv, state_rwkv_shift, state_gla, state_s5_re, state_s5_im)

    tt_p = min(ROW_TILE, t_p)
    nb_s = min(bs, ROW_TILE // t_s)
    rw_chunk = min(RW_HEAD, t_p)
    n_heads = d // RW_HEAD
    y_p, p_wkv, p_shift, p_gla, p_re, p_im = _trunk(x_prompt, ada_p, p_states, weights, 1, tt_p,
                                                     (1, rw_chunk, min(RW_SCAN_TBLK, t_p), min(RW_SCAN_HEADS_PROMPT, n_heads)),
                                                     (1, min(FFN_ROWS, t_p)))
    nb_scan = max(1, min(bs, RW_HEAD // t_s))
    y_s, s_wkv, s_shift, s_gla, s_re, s_im = _trunk(x_sample.transpose(1, 0, 2), ada_s, s_states, weights, nb_s, t_s,
                                                     (nb_scan, t_s, t_s, min(RW_SCAN_HEADS_SAMPLE, n_heads)),
                                                     (min(bs, FFN_ROWS_SAMPLE // t_s), t_s))
    y_s = y_s.transpose(1, 0, 2)
    return (y_p, y_s, p_wkv, p_shift, p_gla, p_re, p_im, s_wkv, s_shift, s_gla, s_re, s_im)
```

```python
import functools
import math

import jax
import jax.numpy as jnp
from jax import lax
from jax.experimental import pallas as pl
from jax.experimental.pallas import tpu as pltpu

f32 = jnp.float32
bf16 = jnp.bfloat16
HI = lax.Precision.HIGHEST

EPS = 1e-6
N_ADA = 9
RW_HEAD = 64
RW_GN_EPS = 64e-5
RW_LORA_PAD = 128
GLA_HEADS = 4
GLA_GATE_NORM = 16.0
GLA_CHUNK = 64
GLA_LORA_PAD = 128
S5_GROUP = 16
S5_STATE = 64
S5_SLAB_GROUPS = 8
LANE = 128
SUBLANE = 8
VMEM_LIMIT_BYTES = 56 * 1024 * 1024

ROW_TILE = 512
FFN_ROWS = 512
FFN_ROWS_SAMPLE = 512
FFN_TILE_F_CAST = 256
FFN_TILE_F = 512
FFN_EW_ROWS = 256
MM_TILE_N = 512
ADA_TILE_N = 1024
RW_PROJ_ROWS = 512
RW_PROJ_TILE_N = 256
RW_SCAN_TBLK = 256
RW_SCAN_HEADS_PROMPT = 8
RW_SCAN_HEADS_SAMPLE = 16
GLA_SCAN_TBLK = 256
GLA_IN_ROWS = 1024
GLA_SCAN_HEADS_PROMPT = 2
GLA_SCAN_HEADS_SAMPLE = 4
S5_TBLK = 128
S5_GLU_ROWS = 256
S5_SEQS_SAMPLE = 8
S5_CARRY_ELEMS = 8192


def _cparams(*sem):
    return pltpu.CompilerParams(dimension_semantics=sem, vmem_limit_bytes=VMEM_LIMIT_BYTES)


def _dot(a, b):
    return jnp.dot(a, b, preferred_element_type=f32)


def _dot_hi(a, b):
    return jnp.dot(a, b, precision=HI, preferred_element_type=f32)


def _dot_nt_hi(a, b):
    return lax.dot_general(a, b, (((1,), (1,)), ((), ())), precision=HI, preferred_element_type=f32)


def _dot_tn_hi(a, b):
    return lax.dot_general(a, b, (((0,), (0,)), ((), ())), precision=HI, preferred_element_type=f32)


def _normmod(x, g, shift, scale):
    ms = jnp.mean(x * x, axis=-1, keepdims=True)
    y = x * lax.rsqrt(ms + EPS) * g
    return y * (1.0 + scale) + shift


def _ada_kernel(c_ref, w_ref, b_ref, o_ref):
    c = c_ref[...]
    s = (c * jax.nn.sigmoid(c)).astype(bf16)
    o_ref[0, 0] = _dot(s, w_ref[0].astype(bf16)) + b_ref[0]


def _ada_call(c_all, ada_w, ada_b, d_model):
    depth, d, n = ada_w.shape
    m = c_all.shape[0]
    tn = min(ADA_TILE_N, d_model)
    per_vec = d_model // tn
    return pl.pallas_call(
        _ada_kernel,
        grid=(depth, n // tn),
        in_specs=[pl.BlockSpec((m, d), lambda l, j: (0, 0)),
                  pl.BlockSpec((1, d, tn), lambda l, j: (l, 0, j)),
                  pl.BlockSpec((1, 1, tn), lambda l, j: (l, 0, j))],
        out_specs=pl.BlockSpec((1, 1, m, tn), lambda l, j: (l, j // per_vec, 0, j % per_vec)),
        out_shape=jax.ShapeDtypeStruct((depth, n // d_model, m, d_model), f32),
        compiler_params=_cparams("arbitrary", "arbitrary"),
        name="ada",
    )(c_all, ada_w, ada_b.reshape(depth, 1, n))


def _ffn_kernel(x_ref, g_ref, sh_ref, sc_ref, gt_ref, wg_ref, wu_ref, wo_ref, *refs, emit_bf16):
    if emit_bf16:
        o_ref, wgo_ref, wuo_ref, woo_ref, h_scr, acc_scr = refs
    else:
        o_ref, h_scr, acc_scr = refs
    j = pl.program_id(1)
    nb, tt, d = x_ref.shape
    if nb > 1:
        step = max(1, min(nb, FFN_EW_ROWS // tt))
        chunks = [(slice(a, a + step), slice(None), a * tt, step * tt) for a in range(0, nb, step)]
    else:
        step = min(tt, FFN_EW_ROWS)
        chunks = [(slice(None), slice(a, a + step), a, step) for a in range(0, tt, step)]

    @pl.when(j == 0)
    def _():
        for s0, s1, r0, nr in chunks:
            h = _normmod(x_ref[s0, s1, :], g_ref[...], _seq_vec(sh_ref), _seq_vec(sc_ref))
            h_scr[r0:r0 + nr, :] = h.reshape(nr, d).astype(bf16)

    @pl.when(j == 0)
    def _():
        acc_scr[...] = jnp.zeros_like(acc_scr)

    hb = h_scr[...]
    wg, wu, wo = wg_ref[...], wu_ref[...], wo_ref[...]
    if emit_bf16:
        wg, wu, wo = wg.astype(bf16), wu.astype(bf16), wo.astype(bf16)
        wgo_ref[...] = wg
        wuo_ref[...] = wu
        woo_ref[...] = wo
    gate = _dot(hb, wg)
    up = _dot(hb, wu)
    act = (gate * jax.nn.sigmoid(gate) * up).astype(bf16)
    acc_scr[...] += _dot(act, wo)

    @pl.when(j == pl.num_programs(1) - 1)
    def _():
        for s0, s1, r0, nr in chunks:
            shp = x_ref[s0, s1, :].shape
            o_ref[s0, s1, :] = x_ref[s0, s1, :] + 0.5 * _seq_vec(gt_ref) * acc_scr[r0:r0 + nr, :].reshape(shp)


def _ffn_call(x, g, shift, scale, gate, weights, nb, tt):
    d = x.shape[-1]
    emit = weights[0] == "f32"
    til = _row_tiling(x.shape, nb, tt)
    row_mode = dict(pipeline_mode=pl.Buffered(1)) if til.grid == 1 else {}
    rows = pl.BlockSpec(til.block(d), til.rmap, **row_mode)
    mod = til.mod_spec(d)
    if emit:
        _, w_in, w_out, l, s = weights
        f = w_out.shape[2]
        tf = min(FFN_TILE_F_CAST, f)
        nf = f // tf
        w_args = (w_in, w_in, w_out)
        w_specs = [pl.BlockSpec((None, None, d, tf), lambda i, j: (l, s, 0, j)),
                   pl.BlockSpec((None, None, d, tf), lambda i, j: (l, s, 0, nf + j)),
                   pl.BlockSpec((None, None, tf, d), lambda i, j: (l, s, j, 0))]
        wo_specs = [pl.BlockSpec((None, d, tf), lambda i, j: (i, 0, j)), pl.BlockSpec((None, d, tf), lambda i, j: (i, 0, j)),
                    pl.BlockSpec((None, tf, d), lambda i, j: (i, j, 0))]
        wo_shapes = [jax.ShapeDtypeStruct((til.grid, d, f), bf16), jax.ShapeDtypeStruct((til.grid, d, f), bf16),
                     jax.ShapeDtypeStruct((til.grid, f, d), bf16)]
    else:
        _, wg, wu, wo = weights
        f = wo.shape[1]
        tf = min(FFN_TILE_F, f)
        nf = f // tf
        w_args = (wg, wu, wo)
        w_specs = [pl.BlockSpec((None, d, tf), lambda i, j: (0, 0, j)), pl.BlockSpec((None, d, tf), lambda i, j: (0, 0, j)),
                   pl.BlockSpec((None, tf, d), lambda i, j: (0, j, 0))]
        wo_specs, wo_shapes = [], []
    out = pl.pallas_call(
        functools.partial(_ffn_kernel, emit_bf16=emit),
        grid=(til.grid, nf),
        in_specs=[rows, pl.BlockSpec((1, 1, d), lambda i, j: (0, 0, 0)), mod, mod, mod] + w_specs,
        out_specs=[rows] + wo_specs,
        out_shape=[jax.ShapeDtypeStruct(x.shape, f32)] + wo_shapes,
        scratch_shapes=[pltpu.VMEM((til.rows, d), bf16), pltpu.VMEM((til.rows, d), f32)],
        compiler_params=_cparams("parallel", "arbitrary"),
        name="ffn",
    )(x, g, _dense_seq_vec(shift, nb), _dense_seq_vec(scale, nb), _dense_seq_vec(gate, nb), *w_args)
    return (out[0], tuple(out[1:])) if emit else out[0]


class _RowTiling:
    def __init__(self, shape, nb, tt):
        self.nb = nb
        if nb > 1:
            t, nseq, _ = shape
            assert tt == t and nseq % nb == 0, (shape, nb, tt)
            self.grid, self.rows = nseq // nb, t * nb
            self.block = lambda w: (t, nb, w)
            self.rmap = lambda i, j: (0, i, 0)
            self.cmap = lambda i, j: (0, i, j)
            self.mod_spec = lambda w, **kw: pl.BlockSpec((nb, w), lambda i, j: (i, 0), **kw)
            self.emod_spec = lambda w: pl.BlockSpec((nb, w), lambda i, j: (i, j))
        else:
            nseq, t, _ = shape
            per = t // tt
            assert per * tt == t, (shape, tt)
            self.per = per
            self.grid, self.rows = nseq * per, tt
            self.block = lambda w: (1, tt, w)
            self.rmap = lambda i, j: (i // per, i % per, 0)
            self.cmap = lambda i, j: (i // per, i % per, j)
            self.mod_spec = lambda w, **kw: pl.BlockSpec((1, 1, w), lambda i, j: (i // per, 0, 0), **kw)
            self.emod_spec = lambda w: pl.BlockSpec((1, 1, w), lambda i, j: (i // per, 0, j))


def _row_tiling(shape, nb, tt):
    return _RowTiling(shape, nb, tt)


def _dense_seq_vec(a, nb):
    return a.reshape(a.shape[0], a.shape[-1]) if nb > 1 else a


def _seq_vec(ref):
    v = ref[...]
    return v if v.ndim == 3 else v[None, :, :]


def _mm_kernel(*refs, n_row, n_mod, n_vec, n_w, n_erow, n_emod, n_evec, n_out, prologue, epilogue):
    pos = 0

    def take(n):
        nonlocal pos
        out = refs[pos:pos + n]
        pos += n
        return out

    rows, mods, vecs, ws = take(n_row), take(n_mod), take(n_vec), take(n_w)
    erows, emods, evecs, outs = take(n_erow), take(n_emod), take(n_evec), take(n_out)
    a_scr = refs[pos]
    nb, tt, k = rows[0].shape
    tn = ws[0].shape[-1]

    @pl.when(pl.program_id(1) == 0)
    def _():
        a = prologue([r[...] for r in rows], [_seq_vec(m) for m in mods], [v[...] for v in vecs])
        a_scr[...] = a.reshape(nb * tt, k).astype(bf16)

    ab = a_scr[...]
    accs = [_dot(ab, w[...]).reshape(nb, tt, tn) for w in ws]
    res = epilogue(accs, [r[...] for r in erows], [_seq_vec(m) for m in emods], [v[...] for v in evecs])
    for o_ref, o in zip(outs, res):
        o_ref[...] = o


def _mm_call(name, rows, mods, vecs, ws, erows, emods, evecs, n_total, n_out, prologue, epilogue, nb, tt, tn=MM_TILE_N,
             resident_w=False):
    k = rows[0].shape[-1]
    tn = min(tn, n_total)
    assert n_total % tn == 0, (name, n_total, tn)
    nj = n_total // tn
    w_mode = dict(pipeline_mode=pl.Buffered(1)) if resident_w else {}
    assert not resident_w or nj == 1
    til = _row_tiling(rows[0].shape, nb, tt)
    mods = [_dense_seq_vec(m, nb) for m in mods]
    emods = [_dense_seq_vec(m, nb) for m in emods]
    in_specs = ([pl.BlockSpec(til.block(k), til.rmap)] * len(rows)
                + [til.mod_spec(k)] * len(mods)
                + [pl.BlockSpec((1, 1, k), lambda i, j: (0, 0, 0))] * len(vecs)
                + [pl.BlockSpec((k, tn), functools.partial(lambda i, j, off: (0, off + j), off=off), **w_mode)
                   for _, off in ws]
                + [pl.BlockSpec(til.block(tn), til.cmap)] * len(erows)
                + [til.emod_spec(tn)] * len(emods)
                + [pl.BlockSpec((1, 1, tn), lambda i, j: (0, 0, j))] * len(evecs))
    kern = functools.partial(_mm_kernel, n_row=len(rows), n_mod=len(mods), n_vec=len(vecs), n_w=len(ws),
                             n_erow=len(erows), n_emod=len(emods), n_evec=len(evecs), n_out=n_out,
                             prologue=prologue, epilogue=epilogue)
    out = pl.pallas_call(
        kern,
        grid=(til.grid, nj),
        in_specs=in_specs,
        out_specs=[pl.BlockSpec(til.block(tn), til.cmap)] * n_out,
        out_shape=[jax.ShapeDtypeStruct(rows[0].shape[:2] + (n_total,), f32)] * n_out,
        scratch_shapes=[pltpu.VMEM((til.rows, k), bf16)],
        compiler_params=_cparams("parallel", "arbitrary"),
        name=name,
    )(*rows, *mods, *vecs, *[w for w, _ in ws], *erows, *emods, *evecs)
    return out


def _pro_normmod(rows, mods, vecs):
    return _normmod(rows[0], vecs[0], mods[0], mods[1])


def _pro_mul(rows, mods, vecs):
    return rows[0] * rows[1]


def _pro_id(rows, mods, vecs):
    return rows[0]


def _epi_id(accs, erows, emods, evecs):
    return accs


def _epi_residual(accs, erows, emods, evecs):
    return [erows[0] + emods[0] * accs[0]]


def _rms_kernel(x_ref, g_ref, o_ref):
    x = x_ref[...]
    o_ref[...] = x * lax.rsqrt(jnp.mean(x * x, axis=-1, keepdims=True) + EPS) * g_ref[...]


def _final_norm_call(x, g, nb, tt):
    d = x.shape[-1]
    til = _row_tiling(x.shape, nb, tt)
    rmap = lambda i: til.rmap(i, 0)
    return pl.pallas_call(
        _rms_kernel,
        grid=(til.grid,),
        in_specs=[pl.BlockSpec(til.block(d), rmap), pl.BlockSpec((1, 1, d), lambda i: (0, 0, 0))],
        out_specs=pl.BlockSpec(til.block(d), rmap),
        out_shape=jax.ShapeDtypeStruct(x.shape, f32),
        compiler_params=_cparams("parallel"),
        name="final_norm",
    )(x, g)


def _rwproj_kernel(x_ref, xprev_ref, sp_ref, ng_ref, sh_ref, sc_ref, mu_ref, w1_ref, a1_ref, g1_ref,
                   wr_ref, wk_ref, wv_ref, w2_ref, a2_ref, g2_ref, w0_ref, a0_ref,
                   r_ref, k_ref, v_ref, lw_ref, a_ref, g_ref, hl_ref, xm_scr, tw_scr, ta_scr, tg_scr, *, tiles_per_seq,
                   time_major):
    b0, b1, d = x_ref.shape
    rows = b0 * b1
    tn = wr_ref.shape[-1]

    @pl.when(pl.program_id(1) == 0)
    def _():
        ng, sh, sc = ng_ref[...], _seq_vec(sh_ref), _seq_vec(sc_ref)
        h3 = _normmod(x_ref[...], ng, sh, sc)
        first = _seq_vec(sp_ref)
        if time_major:
            hp3 = jnp.concatenate([first, h3[:-1]], axis=0)
            hl_ref[...] = h3[b0 - 1:b0]
        else:
            if tiles_per_seq > 1:
                h_before = _normmod(xprev_ref[...], ng, sh, sc)[:, SUBLANE - 1:SUBLANE, :]
                first = jnp.where(pl.program_id(0) % tiles_per_seq == 0, first, h_before)
            tok = lax.broadcasted_iota(jnp.int32, h3.shape, 1)
            hp3 = jnp.where(tok == 0, first, pltpu.roll(h3, 1, axis=1))
            hl_ref[...] = h3[:, b1 - SUBLANE:, :]
        h = h3.reshape(rows, d)
        dlt = hp3.reshape(rows, d) - h
        mu = mu_ref[...]
        for p in range(3):
            xm_scr[p] = (h + dlt * mu[p:p + 1]).astype(bf16)
        xw = (h + dlt * mu[3:4]).astype(bf16)
        tw_scr[...] = jnp.tanh(_dot(xw, w1_ref[...])).astype(bf16)
        xa = (h + dlt * mu[4:5]).astype(bf16)
        ta_scr[...] = _dot(xa, a1_ref[...]).astype(bf16)
        xg = (h + dlt * mu[5:6]).astype(bf16)
        tg_scr[...] = jax.nn.sigmoid(_dot(xg, g1_ref[...])).astype(bf16)

    shp = (b0, b1, tn)
    r_ref[...] = _dot(xm_scr[0], wr_ref[...]).reshape(shp)
    k_ref[...] = _dot(xm_scr[1], wk_ref[...]).reshape(shp)
    v_ref[...] = _dot(xm_scr[2], wv_ref[...]).reshape(shp)
    w_log = -jax.nn.softplus(-(w0_ref[0] + _dot(tw_scr[...], w2_ref[...]))) - 0.5
    lw_ref[...] = (-jnp.exp(w_log)).reshape(shp)
    a_ref[...] = jax.nn.sigmoid(a0_ref[0] + _dot(ta_scr[...], a2_ref[...])).reshape(shp)
    g_ref[...] = _dot(tg_scr[...], g2_ref[...]).reshape(shp)


def _rwproj_call(x, shift_prev, ng, shift, scale, mu, w1, a1, g1, wr, wk, wv, w2, a2, g2, w0, a0, nb, tt):
    d = x.shape[-1]
    tn = min(RW_PROJ_TILE_N, d)
    til = _row_tiling(x.shape, nb, tt)
    if nb > 1:
        per = 1
        pmap = til.rmap
        prev_block = til.block(d)
        tail_spec = pl.BlockSpec((1, nb, d), til.rmap)
        tail_shape = (1, x.shape[1], d)
    else:
        per = til.per
        pmap = lambda i, j: (i // per, jnp.maximum((i % per) * (tt // SUBLANE) - 1, 0), 0)
        prev_block = (1, SUBLANE, d)
        tail_spec = pl.BlockSpec((1, SUBLANE, d), til.rmap)
        tail_shape = (x.shape[0], per * SUBLANE, d)
    mod = til.mod_spec(d)
    full = lambda a: pl.BlockSpec(a.shape, lambda i, j: (0,) * a.ndim)
    col = lambda a: pl.BlockSpec((a.shape[0], tn), lambda i, j: (0, j))
    shift_prev, shift, scale = (_dense_seq_vec(a, nb) for a in (shift_prev, shift, scale))
    rows = til.rows
    kern = functools.partial(_rwproj_kernel, tiles_per_seq=per, time_major=nb > 1)
    return pl.pallas_call(
        kern,
        grid=(til.grid, d // tn),
        in_specs=[pl.BlockSpec(til.block(d), til.rmap), pl.BlockSpec(prev_block, pmap), mod,
                  pl.BlockSpec((1, 1, d), lambda i, j: (0, 0, 0)), mod, mod, full(mu), full(w1), full(a1), full(g1),
                  col(wr), col(wk), col(wv), col(w2), col(a2), col(g2),
                  pl.BlockSpec((1, 1, tn), lambda i, j: (0, 0, j)), pl.BlockSpec((1, 1, tn), lambda i, j: (0, 0, j))],
        out_specs=[pl.BlockSpec(til.block(tn), til.cmap)] * 6 + [tail_spec],
        out_shape=[jax.ShapeDtypeStruct(x.shape, f32)] * 6 + [jax.ShapeDtypeStruct(tail_shape, f32)],
        scratch_shapes=[pltpu.VMEM((3, rows, d), bf16), pltpu.VMEM((rows, w1.shape[1]), bf16),
                        pltpu.VMEM((rows, a1.shape[1]), bf16), pltpu.VMEM((rows, g1.shape[1]), bf16)],
        compiler_params=_cparams("parallel", "arbitrary"),
        name="rwkv_proj",
    )(x, x, shift_prev, ng, shift, scale, mu, w1, a1, g1, wr, wk, wv, w2, a2, g2, w0, a0)


def _dot_nt(a, b):
    return lax.dot_general(a, b, (((1,), (1,)), ((), ())), preferred_element_type=f32)


def _dot_tn(a, b):
    return lax.dot_general(a, b, (((0,), (0,)), ((), ())), preferred_element_type=f32)


def _rwkv_chunk_kernel(r_ref, k_ref, v_ref, lw_ref, a_ref, kk_ref, ka_ref, rk_ref, gw_ref, gb_ref, s0_ref,
                       y_ref, sout_ref, s_scr, rp_scr, y0_scr, bon_scr, m_scr, n_scr, *, nb, tt, n_inner):
    c = pl.program_id(2)
    n = RW_HEAD
    big = nb * tt
    n_dbl = max(1, math.ceil(math.log2(tt)))

    @pl.when(c == 0)
    def _():
        s_scr[...] = s0_ref[...]

    ti = lax.broadcasted_iota(jnp.int32, (big, big), 0)
    si = lax.broadcasted_iota(jnp.int32, (big, big), 1)
    same = (ti // tt) == (si // tt)
    incl = jnp.logical_and(same, si <= ti)
    strict = jnp.logical_and(same, si < ti)
    cumsum = _make_masked_sum(lambda t_, s_: jnp.logical_and((t_ // tt) == (s_ // tt), s_ <= t_), big)
    seqsum = _make_masked_sum(lambda t_, s_: (t_ // tt) == (s_ // tt), big)
    eye_f = (ti == si).astype(f32)
    eye_n = (lax.broadcasted_iota(jnp.int32, (n, n), 0) == lax.broadcasted_iota(jnp.int32, (n, n), 1)).astype(f32)

    kkp, kap, rkp, gwp, gbp = kk_ref[0], ka_ref[0], rk_ref[0], gw_ref[0], gb_ref[0]

    hp = r_ref.shape[-1] // n

    def load(ref, ci):
        if nb > 1:
            return ref[...].reshape(big, hp * n)
        return ref[0, ci * big:(ci + 1) * big, :]

    ch = []
    for ci in range(n_inner):
        r, k, v, lw, a = (load(ref, ci) for ref in (r_ref, k_ref, v_ref, lw_ref, a_ref))
        cum = cumsum(lw)
        tot = cum[big - 1:big, :] if nb == 1 else seqsum(lw)
        g_in = jnp.exp(cum)
        g_prev = jnp.exp(cum - lw)
        g_inv = jnp.exp(-cum)
        g_rest = jnp.exp(tot - cum)
        g_tot = jnp.exp(tot)
        kk = k * kkp
        k2 = k * (1.0 + (a - 1.0) * kap)
        for hh in range(hp):
            sl = slice(hh * n, (hh + 1) * n)
            kk_h = kk[:, sl]
            kk_h = kk_h * jnp.minimum(lax.rsqrt(jnp.sum(kk_h * kk_h, axis=1, keepdims=True)), 1e12)
            b_h = kk_h * a[:, sl]
            r_h, k_h, v_h = r[:, sl], k2[:, sl], v[:, sl]
            rt = r_h * g_in[:, sl]
            at = -kk_h * g_prev[:, sl]
            bon_scr[ci, hh] = jnp.sum(r_h * k_h * rkp[:, sl], axis=1, keepdims=True) * v_h
            ch.append(dict(
                ci=ci, hh=hh, rt=rt, at=at, v=v_h, vb=v_h.astype(bf16), gt=g_tot[:, sl],
                bhat=b_h * g_rest[:, sl], khat=k_h * g_rest[:, sl],
                ra=jnp.concatenate([rt, at], axis=0).astype(bf16),
                bk=jnp.concatenate([b_h * g_inv[:, sl], k_h * g_inv[:, sl]], axis=0).astype(bf16)))
    for q in ch:
        q["amat"] = _dot_nt(q["ra"], q["bk"])
    for q in ch:
        amat = q["amat"]
        q["a_rb"] = jnp.where(incl, amat[:big, :big], 0.0).astype(bf16)
        q["a_rk"] = jnp.where(incl, amat[:big, big:], 0.0).astype(bf16)
        q["pw"] = jnp.where(strict, amat[big:, :big], 0.0)
        q["a_ak"] = jnp.where(strict, amat[big:, big:], 0.0).astype(bf16)
        q["tm"] = eye_f + q["pw"]
    for _ in range(n_dbl - 1):
        for q in ch:
            pwb = q["pw"].astype(bf16)
            q["pw"] = _dot(pwb, pwb)
        for q in ch:
            q["tm"] = q["tm"] + _dot(q["tm"].astype(bf16), q["pw"].astype(bf16))
    for q in ch:
        q["akv"] = _dot(q["a_ak"], q["vb"])
    for q in ch:
        q["wa"] = _dot(q["tm"].astype(bf16), jnp.concatenate([q["akv"], q["at"]], axis=1).astype(bf16))
    for q in ch:
        q["aw"] = _dot(q["a_rb"], q["wa"].astype(bf16))
        q["arkv"] = _dot(q["a_rk"], q["vb"])
    for q in ch:
        ci, hh, wa = q["ci"], q["hh"], q["wa"]
        rp_scr[ci, hh] = q["rt"] + q["aw"][:, n:]
        y0_scr[ci, hh] = q["aw"][:, :n] + q["arkv"]
        for i in range(nb):
            rows = slice(i * tt, (i + 1) * tt)
            bh_i = q["bhat"][rows].astype(bf16)
            m_scr[ci, hh, i] = eye_n * q["gt"][i * tt:i * tt + 1] + _dot_tn(wa[rows, n:].astype(bf16), bh_i)
            n_scr[ci, hh, i] = _dot_tn(jnp.concatenate([wa[rows, :n], q["v"][rows]], axis=0).astype(bf16),
                                       jnp.concatenate([q["bhat"][rows], q["khat"][rows]], axis=0).astype(bf16))

    for ci in range(n_inner):
        ys = []
        for hh in range(hp):
            sl = slice(hh * n, (hh + 1) * n)
            rp = rp_scr[ci, hh]
            parts = []
            for i in range(nb):
                sb = s_scr[i, hh].astype(bf16)
                parts.append(_dot_nt(rp[i * tt:(i + 1) * tt].astype(bf16), sb))
                s_scr[i, hh] = _dot(sb, m_scr[ci, hh, i].astype(bf16)) + n_scr[ci, hh, i]
            y = (parts[0] if nb == 1 else jnp.concatenate(parts, axis=0)) + y0_scr[ci, hh]
            mean = jnp.mean(y, axis=1, keepdims=True)
            var = jnp.mean(jnp.square(y - mean), axis=1, keepdims=True)
            ys.append((y - mean) * lax.rsqrt(var + RW_GN_EPS) * gwp[:, sl] + gbp[:, sl] + bon_scr[ci, hh])
        yo = jnp.concatenate(ys, axis=1)
        if nb > 1:
            y_ref[...] = yo.reshape(nb, tt, hp * n)
        else:
            y_ref[0, ci * big:(ci + 1) * big, :] = yo

    @pl.when(c == pl.num_programs(2) - 1)
    def _():
        sout_ref[...] = s_scr[...]


def _rwkv_pair_kernel(r_ref, k_ref, v_ref, lw_ref, a_ref, kk_ref, ka_ref, rk_ref, gw_ref, gb_ref, s0_ref,
                      y_ref, sout_ref, s_scr, rp_scr, y0_scr, bon_scr, m_scr, n_scr, *, nb, tt, n_inner):
    c = pl.program_id(2)
    n = RW_HEAD
    w2 = 2 * n
    big = nb * tt
    n_dbl = max(1, math.ceil(math.log2(tt)))
    pp = r_ref.shape[-1] // w2

    lane_r = lax.broadcasted_iota(jnp.int32, (w2, w2), 0)
    lane_c = lax.broadcasted_iota(jnp.int32, (w2, w2), 1)
    bd = (lane_r // n) == (lane_c // n)
    eye_w = (lane_r == lane_c).astype(f32)

    @pl.when(c == 0)
    def _():
        s_scr[...] = jnp.zeros_like(s_scr)
        for i in range(nb):
            for p in range(pp):
                s_scr[i, p, 0:n, 0:n] = s0_ref[i, 2 * p]
                s_scr[i, p, n:w2, n:w2] = s0_ref[i, 2 * p + 1]

    ti = lax.broadcasted_iota(jnp.int32, (big, w2), 0)
    si = lax.broadcasted_iota(jnp.int32, (big, w2), 1) % big
    same2 = (ti // tt) == (si // tt)
    incl2 = jnp.logical_and(same2, si <= ti)
    strict2 = jnp.logical_and(same2, si < ti)
    upper = lax.broadcasted_iota(jnp.int32, (big, w2), 1) >= big
    cumsum = _make_masked_sum(lambda t_, s_: jnp.logical_and((t_ // tt) == (s_ // tt), s_ <= t_), big)
    seqsum = _make_masked_sum(lambda t_, s_: (t_ // tt) == (s_ // tt), big)
    eye_f = (lax.broadcasted_iota(jnp.int32, (big, big), 0) == lax.broadcasted_iota(jnp.int32, (big, big), 1)).astype(f32)
    head_a = lax.broadcasted_iota(jnp.int32, (big, w2), 1) < n
    zeros_lw = jnp.zeros((big, w2), bf16)

    def head_sum(x):
        s_a = jnp.sum(jnp.where(head_a, x, 0.0), axis=1, keepdims=True)
        s_b = jnp.sum(jnp.where(head_a, 0.0, x), axis=1, keepdims=True)
        return jnp.where(head_a, s_a, s_b)

    def load(ref, ci):
        if nb > 1:
            return ref[...].reshape(big, pp * w2)
        return ref[0, ci * big:(ci + 1) * big, :]

    ch, slabs = [], []
    for ci in range(n_inner):
        r, k, v, lw, a = (load(ref, ci) for ref in (r_ref, k_ref, v_ref, lw_ref, a_ref))
        cum = cumsum(lw)
        tot = cum[big - 1:big, :] if nb == 1 else seqsum(lw)
        for p in range(pp):
            sl = slice(p * w2, (p + 1) * w2)
            r_p, k_p, v_p, lw_p, a_p, cum_p, tot_p = r[:, sl], k[:, sl], v[:, sl], lw[:, sl], a[:, sl], cum[:, sl], tot[:, sl]
            g_inv = jnp.exp(-cum_p)
            g_rest = jnp.exp(tot_p - cum_p)
            kk = k_p * kk_ref[0][:, sl]
            kk = kk * jnp.minimum(lax.rsqrt(head_sum(kk * kk)), 1e12)
            k2 = k_p * (1.0 + (a_p - 1.0) * ka_ref[0][:, sl])
            b = kk * a_p
            rt = r_p * jnp.exp(cum_p)
            at = -kk * jnp.exp(cum_p - lw_p)
            bhat, khat = b * g_rest, k2 * g_rest
            bon_scr[ci, p] = head_sum(r_p * k2 * rk_ref[0][:, sl]) * v_p
            slab = dict(ci=ci, p=p, bhat=bhat, khat=khat, v=v_p, gt=jnp.exp(tot_p),
                        bk=jnp.concatenate([b * g_inv, k2 * g_inv], axis=0).astype(bf16))
            slabs.append(slab)
            for hd in range(2):
                mh = head_a if hd == 0 else jnp.logical_not(head_a)
                at_h = jnp.where(mh, at, 0.0)
                rt_h = jnp.where(mh, rt, 0.0)
                ch.append(dict(slab=slab, rt=rt_h, at=at_h,
                               ra=jnp.concatenate([rt_h, at_h], axis=0).astype(bf16),
                               vv=jnp.concatenate([zeros_lw, jnp.where(mh, v_p, 0.0).astype(bf16)], axis=0)))
    for q in ch:
        q["amat"] = _dot_nt(q["ra"], q["slab"]["bk"])
    for q in ch:
        amat = q["amat"]
        q["top"] = jnp.where(incl2, amat[:big], 0.0).astype(bf16)
        bot = jnp.where(strict2, amat[big:], 0.0)
        q["bot_k"] = jnp.where(upper, bot, 0.0).astype(bf16)
        q["pw"] = bot[:, :big]
        q["tm"] = eye_f + q["pw"]
    for _ in range(n_dbl - 1):
        for q in ch:
            pwb = q["pw"].astype(bf16)
            q["pw"] = _dot(pwb, pwb)
        for q in ch:
            q["tm"] = q["tm"] + _dot(q["tm"].astype(bf16), q["pw"].astype(bf16))
    for q in ch:
        q["akv"] = _dot(q["bot_k"], q["vv"])
    for q in ch:
        q["wa"] = _dot(q["tm"].astype(bf16), jnp.concatenate([q["akv"], q["at"]], axis=1).astype(bf16))
    for q in ch:
        wab = q["wa"].astype(bf16)
        q["arb_a"] = _dot(q["top"][:, :big], wab[:, w2:])
        q["y0"] = _dot(q["top"], jnp.concatenate([wab[:, :w2], q["vv"][big:]], axis=0))
    for idx, slab in enumerate(slabs):
        qa, qb = ch[2 * idx], ch[2 * idx + 1]
        ci, p = slab["ci"], slab["p"]
        rp_scr[ci, p] = qa["rt"] + qb["rt"] + qa["arb_a"] + qb["arb_a"]
        y0_scr[ci, p] = qa["y0"] + qb["y0"]
        wa = qa["wa"] + qb["wa"]
        w_pair, a_pair = wa[:, :w2], wa[:, w2:]
        for i in range(nb):
            rows = slice(i * tt, (i + 1) * tt)
            bh_i = slab["bhat"][rows].astype(bf16)
            m_scr[ci, p, i] = eye_w * slab["gt"][i * tt:i * tt + 1] + jnp.where(bd, _dot_tn(a_pair[rows].astype(bf16), bh_i), 0.0)
            n_scr[ci, p, i] = jnp.where(bd, _dot_tn(
                jnp.concatenate([w_pair[rows], slab["v"][rows]], axis=0).astype(bf16),
                jnp.concatenate([slab["bhat"][rows], slab["khat"][rows]], axis=0).astype(bf16)), 0.0)

    for ci in range(n_inner):
        ys = []
        for p in range(pp):
            sl = slice(p * w2, (p + 1) * w2)
            rp = rp_scr[ci, p]
            parts = []
            for i in range(nb):
                sb = s_scr[i, p].astype(bf16)
                parts.append(_dot_nt(rp[i * tt:(i + 1) * tt].astype(bf16), sb))
                s_scr[i, p] = _dot(sb, m_scr[ci, p, i].astype(bf16)) + n_scr[ci, p, i]
            y = (parts[0] if nb == 1 else jnp.concatenate(parts, axis=0)) + y0_scr[ci, p]
            mean = head_sum(y) * (1.0 / n)
            var = head_sum(jnp.square(y - mean)) * (1.0 / n)
            ys.append((y - mean) * lax.rsqrt(var + RW_GN_EPS) * gw_ref[0][:, sl] + gb_ref[0][:, sl] + bon_scr[ci, p])
        yo = ys[0] if pp == 1 else jnp.concatenate(ys, axis=1)
        if nb > 1:
            y_ref[...] = yo.reshape(nb, tt, pp * w2)
        else:
            y_ref[0, ci * big:(ci + 1) * big, :] = yo

    @pl.when(c == pl.num_programs(2) - 1)
    def _():
        for i in range(nb):
            for p in range(pp):
                sout_ref[i, 2 * p] = s_scr[i, p, 0:n, 0:n]
                sout_ref[i, 2 * p + 1] = s_scr[i, p, n:w2, n:w2]


def _rwscan_call(r, k, v, lw, a, k_k, k_a, r_k, gn_w, gn_b, s0_all, s_new_all, layer, nb, tt, tblk, hp):
    nseq, t, d = r.shape
    nh = d // RW_HEAD
    lanes = hp * RW_HEAD
    nc = t // tblk
    n_inner = tblk // tt if nb == 1 else 1
    xspec = pl.BlockSpec((nb, tblk, lanes), lambda i, p, c: (i, c, p))
    pspec = pl.BlockSpec((1, 1, lanes), lambda i, p, c: (0, 0, p))
    sspec = pl.BlockSpec((nb, hp, RW_HEAD, RW_HEAD), lambda i, p, c: (i, p, 0, 0))
    assert hp % 2 == 0 and nb * tt == RW_HEAD, (hp, nb, tt)
    pp, w2 = hp // 2, 2 * RW_HEAD
    base = functools.partial(_rwkv_pair_kernel, nb=nb, tt=tt, n_inner=n_inner)
    chunk_rows = pltpu.VMEM((n_inner, pp, nb * tt, w2), f32)
    chunk_mats = pltpu.VMEM((n_inner, pp, nb, w2, w2), f32)
    layer_spec = pl.BlockSpec((None, nb, hp, RW_HEAD, RW_HEAD), lambda i, p, c: (layer, i, p, 0, 0))
    n_in = 11
    if s_new_all is None:
        kern, extra_specs, extra_args, aliases = base, [], [], {}
    else:
        kern = lambda *refs: base(*refs[:n_in], *refs[n_in + 1:])
        extra_specs, extra_args, aliases = [pl.BlockSpec(memory_space=pl.ANY)], [s_new_all], {n_in: 1}
    return pl.pallas_call(
        kern,
        grid=(nseq // nb, nh // hp, nc),
        in_specs=[xspec] * 5 + [pspec] * 5 + [layer_spec] + extra_specs,
        out_specs=[xspec, layer_spec],
        out_shape=[jax.ShapeDtypeStruct((nseq, t, d), f32), jax.ShapeDtypeStruct(s0_all.shape, f32)],
        input_output_aliases=aliases,
        scratch_shapes=[pltpu.VMEM((nb, pp, w2, w2), f32), chunk_rows, chunk_rows, chunk_rows,
                        chunk_mats, chunk_mats],
        compiler_params=_cparams("parallel", "parallel", "arbitrary"),
        name="rwkv_scan",
    )(r, k, v, lw, a, k_k, k_a, r_k, gn_w, gn_b, s0_all, *extra_args)


def _split3(x):
    hi = x.astype(bf16)
    r1 = x - hi.astype(f32)
    mid = r1.astype(bf16)
    return hi, mid, (r1 - mid.astype(f32)).astype(bf16)


def _make_masked_sum(mask_fn, rows):
    if (3 * rows) % 16:
        ti = lax.broadcasted_iota(jnp.int32, (rows, rows), 0)
        si = lax.broadcasted_iota(jnp.int32, (rows, rows), 1)
        mask_f = mask_fn(ti, si).astype(f32)
        return lambda x: _dot_hi(mask_f, x)
    ti = lax.broadcasted_iota(jnp.int32, (rows, 3 * rows), 0)
    si = lax.broadcasted_iota(jnp.int32, (rows, 3 * rows), 1) % rows
    mask3 = jnp.where(mask_fn(ti, si), 1.0, 0.0).astype(bf16)
    return lambda x: _dot(mask3, jnp.concatenate(_split3(x), axis=0))


def _glascan_kernel(q_ref, k_ref, v_ref, r_ref, la_ref, on_ref, s0_ref, o_ref, sout_ref, s_scr, *, chunk, n_inner, hb):
    c = pl.program_id(2)
    hk = q_ref.shape[-1] // hb
    hv = v_ref.shape[-1] // hb

    @pl.when(c == 0)
    def _():
        s_scr[...] = s0_ref[0]

    ti = lax.broadcasted_iota(jnp.int32, (chunk, chunk), 0)
    si = lax.broadcasted_iota(jnp.int32, (chunk, chunk), 1)
    causal = si <= ti
    cumsum = _make_masked_sum(lambda t_, s_: s_ <= t_, chunk)

    ch = []
    for ci in range(n_inner):
        rows = slice(ci * chunk, (ci + 1) * chunk)
        for h in range(hb):
            ks, vs = slice(h * hk, (h + 1) * hk), slice(h * hv, (h + 1) * hv)
            ch.append(dict(rows=rows, h=h, vs=vs, q=q_ref[0, rows, ks], k=k_ref[0, rows, ks],
                           vb=v_ref[0, rows, vs].astype(bf16), la=la_ref[0, rows, ks]))
    for z in ch:
        z["b"] = cumsum(z["la"])
    for z in ch:
        b = z["b"]
        b_last = b[chunk - 1:chunk, :]
        z["q_in"] = (z["q"] * (hk ** -0.5) * jnp.exp(b)).astype(bf16)
        z["k_in"] = (z["k"] * jnp.exp(-b)).astype(bf16)
        z["kd"] = (z["k"] * jnp.exp(b_last - b)).astype(bf16)
        z["dec"] = jnp.transpose(jnp.broadcast_to(jnp.exp(b_last), (SUBLANE, hk)))[:, 0:1]
    for z in ch:
        z["att"] = _dot_nt(z["q_in"], z["k_in"])
    for z in ch:
        z["ov"] = _dot(jnp.where(causal, z["att"], 0.0).astype(bf16), z["vb"])
        z["inc"] = _dot_tn(z["kd"], z["vb"])

    for z in ch:
        h, rows, vs = z["h"], z["rows"], z["vs"]
        s = s_scr[h]
        o = z["ov"] + _dot(z["q_in"], s.astype(bf16))
        s_scr[h] = s * z["dec"] + z["inc"]
        o = o * lax.rsqrt(jnp.mean(o * o, axis=-1, keepdims=True) + EPS) * on_ref[...]
        rr = r_ref[0, rows, vs]
        o_ref[0, rows, vs] = o * (rr * jax.nn.sigmoid(rr))

    @pl.when(c == pl.num_programs(2) - 1)
    def _():
        sout_ref[0] = s_scr[...]


def _glascan_call(qkvr, la, o_norm, s0, tblk, hb):
    nseq, t, _ = qkvr.shape
    _, gh, hk, hv = s0.shape
    chunk = math.gcd(t, GLA_CHUNK)
    tblk = min(tblk, t)
    k_off = (gh * hk) // (hb * hk)
    v_off = (2 * gh * hk) // (hb * hv)
    r_off = (2 * gh * hk + gh * hv) // (hb * hv)
    kern = functools.partial(_glascan_kernel, chunk=chunk, n_inner=tblk // chunk, hb=hb)
    return pl.pallas_call(
        kern,
        grid=(nseq, gh // hb, t // tblk),
        in_specs=[pl.BlockSpec((1, tblk, hb * hk), lambda b, h, c: (b, c, h)),
                  pl.BlockSpec((1, tblk, hb * hk), lambda b, h, c: (b, c, k_off + h)),
                  pl.BlockSpec((1, tblk, hb * hv), lambda b, h, c: (b, c, v_off + h)),
                  pl.BlockSpec((1, tblk, hb * hv), lambda b, h, c: (b, c, r_off + h)),
                  pl.BlockSpec((1, tblk, hb * hk), lambda b, h, c: (b, c, h)),
                  pl.BlockSpec((1, hv), lambda b, h, c: (0, 0)),
                  pl.BlockSpec((1, hb, hk, hv), lambda b, h, c: (b, h, 0, 0))],
        out_specs=[pl.BlockSpec((1, tblk, hb * hv), lambda b, h, c: (b, c, h)),
                   pl.BlockSpec((1, hb, hk, hv), lambda b, h, c: (b, h, 0, 0))],
        out_shape=[jax.ShapeDtypeStruct((nseq, t, gh * hv), f32), jax.ShapeDtypeStruct(s0.shape, f32)],
        scratch_shapes=[pltpu.VMEM((hb, hk, hv), f32)],
        compiler_params=_cparams("parallel", "parallel", "arbitrary"),
        name="gla_scan",
    )(qkvr, qkvr, qkvr, qkvr, la, o_norm, s0)


def _s5prep_kernel(are_ref, aim_ref, ldt_ref, bre_ref, bim_ref, abre_ref, abim_ref, bbre_ref, bbim_ref):
    a_re, a_im = are_ref[...], aim_ref[...]
    dt = jnp.exp(ldt_ref[...])
    mag = jnp.exp(a_re * dt)
    ab_re, ab_im = mag * jnp.cos(a_im * dt), mag * jnp.sin(a_im * dt)
    den = a_re * a_re + a_im * a_im
    nr = ab_re - 1.0
    cf_re = (nr * a_re + ab_im * a_im) / den
    cf_im = (ab_im * a_re - nr * a_im) / den
    abre_ref[...] = ab_re
    abim_ref[...] = ab_im
    b_re, b_im = bre_ref[...], bim_ref[...]
    bbre_ref[...] = cf_re * b_re - cf_im * b_im
    bbim_ref[...] = cf_re * b_im + cf_im * b_re


def _s5prep_call(a_re, a_im, log_dt, b_re_t, b_im_t):
    g, n = a_re.shape
    return pl.pallas_call(
        _s5prep_kernel,
        out_shape=[jax.ShapeDtypeStruct((g, 1, n), f32)] * 2 + [jax.ShapeDtypeStruct(b_re_t.shape, f32)] * 2,
        name="s5_prep",
    )(a_re.reshape(g, 1, n), a_im.reshape(g, 1, n), log_dt.reshape(g, 1, 1), b_re_t, b_im_t)


def _s5scan_kernel(x_ref, g_ref, sh_ref, sc_ref, wbu_ref, wc_ref, abre_ref, abim_ref, dsk_ref, x0re_ref, x0im_ref,
                   y_ref, xre_ref, xim_ref, bre_scr, bim_scr, sre_scr, sim_scr, *, lane_chunk, nb):
    c = pl.program_id(1)
    d = x_ref.shape[-1]
    rows = x_ref.shape[0] * x_ref.shape[1]
    tblk = rows // nb
    n_slab = wbu_ref.shape[0]
    half = wbu_ref.shape[2] // 2
    n_state = n_slab * half

    @pl.when(c == 0)
    def _():
        sre_scr[...] = x0re_ref[0]
        sim_scr[...] = x0im_ref[0]

    h = _normmod(x_ref[...], g_ref[...], sh_ref[...], sc_ref[...]).reshape(rows, d)
    hb = h.astype(bf16)
    for s in range(n_slab):
        bu = _dot(hb[:, s * LANE:(s + 1) * LANE], wbu_ref[s])
        bre_scr[:, s * half:(s + 1) * half] = bu[:, :half]
        bim_scr[:, s * half:(s + 1) * half] = bu[:, half:]

    for q in range(n_state // lane_chunk):
        ls = slice(q * lane_chunk, (q + 1) * lane_chunk)
        ar, ai = abre_ref[:, ls], abim_ref[:, ls]

        def step(t, carry):
            xr, xi = carry
            row = pl.ds(pl.multiple_of(t * nb, nb), nb)
            nxr = ar * xr - ai * xi + bre_scr[row, ls]
            nxi = ar * xi + ai * xr + bim_scr[row, ls]
            bre_scr[row, ls] = nxr
            bim_scr[row, ls] = nxi
            return nxr, nxi

        xr, xi = lax.fori_loop(0, tblk, step, (sre_scr[:, ls], sim_scr[:, ls]))
        sre_scr[:, ls] = xr
        sim_scr[:, ls] = xi

    for s in range(n_slab):
        xs = jnp.concatenate([bre_scr[:, s * half:(s + 1) * half], bim_scr[:, s * half:(s + 1) * half]], axis=1)
        ys = _dot(xs.astype(bf16), wc_ref[s])
        sl = slice(s * LANE, (s + 1) * LANE)
        y_ref[:, :, sl] = jax.nn.gelu(ys + dsk_ref[:, sl] * h[:, sl]).reshape(y_ref.shape[0], y_ref.shape[1], LANE)

    @pl.when(c == pl.num_programs(1) - 1)
    def _():
        xre_ref[0] = sre_scr[...]
        xim_ref[0] = sim_scr[...]


def _s5scan_call(x, g, shift, scale, wbu, wc, ab_re, ab_im, d_skip, x0_re, x0_im, tblk, nb):
    if nb > 1:
        t, nseq, d = x.shape
        tblk = t
        shift, scale = shift.reshape(1, nseq, d), scale.reshape(1, nseq, d)
        xspec = pl.BlockSpec((t, nb, d), lambda b, c: (0, b, 0))
        mod = pl.BlockSpec((1, nb, d), lambda b, c: (0, b, 0))
    else:
        nseq, t, d = x.shape
        tblk = min(tblk, t)
        xspec = pl.BlockSpec((1, tblk, d), lambda b, c: (b, c, 0))
        mod = pl.BlockSpec((1, 1, d), lambda b, c: (b, 0, 0))
    n_state = ab_re.shape[1]
    full = lambda a: pl.BlockSpec(a.shape, lambda b, c: (0,) * a.ndim)
    st = pl.BlockSpec((1, nb, n_state), lambda b, c: (b, 0, 0))
    st_shape = jax.ShapeDtypeStruct((nseq // nb, nb, n_state), f32)
    kern = functools.partial(_s5scan_kernel, lane_chunk=min(n_state, S5_CARRY_ELEMS // nb), nb=nb)
    rows = nb * tblk
    return pl.pallas_call(
        kern,
        grid=(nseq // nb, t // tblk),
        in_specs=[xspec, pl.BlockSpec((1, 1, d), lambda b, c: (0, 0, 0)),
                  mod, mod, full(wbu), full(wc), full(ab_re), full(ab_im), full(d_skip), st, st],
        out_specs=[xspec, st, st],
        out_shape=[jax.ShapeDtypeStruct(x.shape, f32), st_shape, st_shape],
        scratch_shapes=[pltpu.VMEM((rows, n_state), f32), pltpu.VMEM((rows, n_state), f32),
                        pltpu.VMEM((nb, n_state), f32), pltpu.VMEM((nb, n_state), f32)],
        compiler_params=_cparams("parallel", "arbitrary"),
        name="s5_scan",
    )(x, g, shift, scale, wbu, wc, ab_re, ab_im, d_skip,
      x0_re.reshape(nseq // nb, nb, n_state), x0_im.reshape(nseq // nb, nb, n_state))


def _epi_glu(accs, erows, emods, evecs):
    val = accs[0] + evecs[0]
    gate = accs[1] + evecs[1]
    return [erows[0] + emods[0] * (val * jax.nn.sigmoid(gate))]


def _rwkv_mixer(x, g, shift, scale, gm, shift_prev, wkv_all, wkv_new_all, layer, p, nb, tt, scan_cfg):
    t, d = (x.shape[0] if nb > 1 else x.shape[1]), x.shape[-1]
    pnb, ptt = (RW_PROJ_ROWS // t, t) if nb > 1 else (1, min(RW_PROJ_ROWS, t))
    r, k, v, lw, a, gg, h_tail = _rwproj_call(x, shift_prev[:, None, :], g, shift, scale, p["mu"], p["w1"], p["a1"], p["g1"],
                                               p["wr"], p["wk"], p["wv"], p["w2"], p["a2"], p["g2"], p["w0"], p["a0"], pnb, ptt)
    seq_major = (lambda z: z.transpose(1, 0, 2)) if nb > 1 else (lambda z: z)
    y, wkv = _rwscan_call(*(seq_major(z) for z in (r, k, v, lw, a)), p["k_k"], p["k_a"], p["r_k"], p["gn_w"], p["gn_b"],
                          wkv_all, wkv_new_all, layer, *scan_cfg)
    (x_new,) = _mm_call("rwkv_out", [seq_major(y), gg], [], [], [(p["w_o"], 0)], [x], [gm], [], d, 1,
                        _pro_mul, _epi_residual, nb, tt, tn=d, resident_w=True)
    return x_new, (h_tail[0] if nb > 1 else h_tail[:, -1]), wkv


def _gla_mixer(x, g, shift, scale, gm, s0, p, nb, tt):
    d = x.shape[-1]
    n_main = p["w_main"].shape[1]
    i_nb, i_tt = (nb, tt) if nb > 1 else (1, min(x.shape[1], GLA_IN_ROWS))
    (qkvr,) = _mm_call("gla_in", [x], [shift, scale], [g], [(p["w_main"], 0)], [], [], [], n_main, 1,
                       _pro_normmod, _epi_id, i_nb, i_tt)
    (a_low,) = _mm_call("gla_gate_in", [x], [shift, scale], [g], [(p["w_gate"], 0)], [], [], [], GLA_LORA_PAD, 1,
                        _pro_normmod, _epi_id, nb, tt)

    def epi_la(accs, erows, emods, evecs):
        return [jax.nn.log_sigmoid(accs[0] + evecs[0]) / GLA_GATE_NORM]

    dk = p["a_w2"].shape[1]
    (la,) = _mm_call("gla_gate", [a_low], [], [], [(p["a_w2"], 0)], [], [], [p["a_b"]], dk, 1, _pro_id, epi_la, nb, tt)
    hb = min(s0.shape[1], GLA_SCAN_HEADS_SAMPLE if nb > 1 else GLA_SCAN_HEADS_PROMPT)
    seq_major = (lambda z: z.transpose(1, 0, 2)) if nb > 1 else (lambda z: z)
    o, s_new = _glascan_call(seq_major(qkvr), seq_major(la), p["o_norm"], s0, GLA_SCAN_TBLK, hb)
    (x_new,) = _mm_call("gla_out", [seq_major(o)], [], [], [(p["w_o"], 0)], [x], [gm], [], d, 1,
                        _pro_id, _epi_residual, nb, tt, tn=d, resident_w=True)
    return x_new, s_new


def _s5_mixer(x, g, shift, scale, gm, x0_re, x0_im, p, nb, tt):
    d = x.shape[-1]
    s5_nb = min(x.shape[1], S5_SEQS_SAMPLE) if nb > 1 else 1
    yg, xre, xim = _s5scan_call(x, g, shift, scale, p["wbu"], p["wc"], p["ab_re"], p["ab_im"], p["d_skip"],
                                x0_re, x0_im, S5_TBLK, s5_nb)
    g_nb, g_tt = (max(1, nb // 2), tt) if nb > 1 else (1, min(tt, S5_GLU_ROWS))
    (x_new,) = _mm_call("s5_glu", [yg], [], [], [(p["glu_w"], 0), (p["glu_w"], 1)], [x], [gm],
                        [p["glu_b_val"], p["glu_b_gate"]], d, 1, _pro_id, _epi_glu, g_nb, g_tt, tn=d, resident_w=True)
    return x_new, xre.reshape(x0_re.shape), xim.reshape(x0_im.shape)


def _s5_params(a_re, a_im, log_dt, b_re, b_im, c_re, c_im, d_skip, glu_w, glu_b):
    g, n, cg = b_re.shape
    d = g * cg
    ab_re, ab_im, bb_re, bb_im = _s5prep_call(a_re, a_im, log_dt, b_re.transpose(0, 2, 1), b_im.transpose(0, 2, 1))
    sg = S5_SLAB_GROUPS
    eye = jnp.eye(sg, dtype=f32)

    def bdiag(m):
        gq, rr, cc = m.shape
        m = m.reshape(gq // sg, sg, rr, cc)
        return jnp.einsum("sjrc,jk->sjrkc", m, eye).reshape(gq // sg, sg * rr, sg * cc)

    wbu = jnp.concatenate([bdiag(bb_re), bdiag(bb_im)], axis=2).astype(bf16)
    wc = jnp.concatenate([bdiag(c_re.transpose(0, 2, 1)), bdiag(-c_im.transpose(0, 2, 1))], axis=1).astype(bf16)
    return dict(wbu=wbu, wc=wc, ab_re=ab_re.reshape(1, g * n), ab_im=ab_im.reshape(1, g * n),
                d_skip=d_skip.reshape(1, d), glu_w=glu_w.astype(bf16),
                glu_b_val=glu_b[:d].reshape(1, 1, d), glu_b_gate=glu_b[d:].reshape(1, 1, d))


def _trunk(x, ada, states, weights, nb, tt, rw_scan_cfg, ffn_cfg, ffn_bf16):
    d = x.shape[-1]
    depth, _, nseq, _ = ada.shape
    wkv, shift, gla, s5_re, s5_im = states
    new = dict(shift=[], gla=[], re=[], im=[])
    new_wkv = None
    ia = ib = ic = 0
    vec = (lambda l, j: ada[l, j]) if nb > 1 else (lambda l, j: ada[l, j].reshape(nseq, 1, d))

    def ffn(x_, g_, shift_, scale_, gate_, l_, s_):
        if (l_, s_) in ffn_bf16:
            return _ffn_call(x_, g_, shift_, scale_, gate_, ("bf16",) + ffn_bf16[(l_, s_)], *ffn_cfg)
        y_, ffn_bf16[(l_, s_)] = _ffn_call(x_, g_, shift_, scale_, gate_,
                                          ("f32", weights["ffn_w_in"], weights["ffn_w_out"], l_, s_), *ffn_cfg)
        return y_

    for l in range(depth):
        ng = lambda s: weights["norm_g"][l, s].reshape(1, 1, d)
        x = ffn(x, ng(0), vec(l, 0), vec(l, 1), vec(l, 2), l, 0)
        kind = l % 3
        if kind == 0:
            x, sh, new_wkv = _rwkv_mixer(x, ng(1), vec(l, 3), vec(l, 4), vec(l, 5), shift[ia], wkv, new_wkv, ia,
                                         weights["rw"][ia], nb, tt, rw_scan_cfg)
            new["shift"].append(sh)
            ia += 1
        elif kind == 1:
            x, st = _gla_mixer(x, ng(1), vec(l, 3), vec(l, 4), vec(l, 5), gla[ib], weights["gla"][ib], nb, tt)
            new["gla"].append(st)
            ib += 1
        else:
            x, sr, si = _s5_mixer(x, ng(1), vec(l, 3), vec(l, 4), vec(l, 5), s5_re[ic], s5_im[ic], weights["s5"][ic], nb, tt)
            new["re"].append(sr)
            new["im"].append(si)
            ic += 1
        x = ffn(x, ng(2), vec(l, 6), vec(l, 7), vec(l, 8), l, 1)
    y = _final_norm_call(x, weights["final_g"].reshape(1, 1, d), nb, tt)
    return (y, new_wkv, jnp.stack(new["shift"]), jnp.stack(new["gla"]),
            jnp.stack(new["re"]), jnp.stack(new["im"]))


def _pad_to(a, axis, size):
    pad = [(0, 0)] * a.ndim
    pad[axis] = (0, size - a.shape[axis])
    return jnp.pad(a, pad)


def kernel(x_prompt, x_sample, state_rwkv_wkv, state_rwkv_shift, state_gla, state_s5_re, state_s5_im, c_prompt, c_sample, norm_g, ada_w, ada_b, ffn_w_in, ffn_w_out, rw_mu, rw_w_rkv, rw_w0, rw_w1, rw_w2, rw_a0, rw_a1, rw_a2, rw_g1, rw_g2, rw_k_k, rw_k_a, rw_r_k, rw_gn_w, rw_gn_b, rw_w_o, gla_w_in, gla_a_w2, gla_a_b, gla_o_norm, gla_w_o, s5_A_re, s5_A_im, s5_log_dt, s5_B_re, s5_B_im, s5_C_re, s5_C_im, s5_D, s5_glu_w, s5_glu_b, final_g):
    bp, t_p, d = x_prompt.shape
    bs, t_s, _ = x_sample.shape
    depth = ada_w.shape[0]
    n_a, n_b, n_c = rw_mu.shape[0], gla_w_in.shape[0], s5_A_re.shape[0]

    n_c_rows = bp + bs
    c_all = _pad_to(jnp.concatenate([c_sample, c_prompt], axis=0), 0, -(-n_c_rows // SUBLANE) * SUBLANE)
    ada = _ada_call(c_all, ada_w, ada_b, d)
    ada_s = ada[:, :, :bs]
    ada_p = ada[:, :, bs:n_c_rows]

    rw = []
    for i in range(n_a):
        vec = lambda a: a[i].reshape(1, 1, d)
        rw.append(dict(
            mu=rw_mu[i], wr=rw_w_rkv[i, 0].astype(bf16), wk=rw_w_rkv[i, 1].astype(bf16), wv=rw_w_rkv[i, 2].astype(bf16),
            w1=_pad_to(rw_w1[i], 1, RW_LORA_PAD).astype(bf16), w2=_pad_to(rw_w2[i], 0, RW_LORA_PAD).astype(bf16),
            a1=_pad_to(rw_a1[i], 1, RW_LORA_PAD).astype(bf16), a2=_pad_to(rw_a2[i], 0, RW_LORA_PAD).astype(bf16),
            g1=rw_g1[i].astype(bf16), g2=rw_g2[i].astype(bf16), w0=vec(rw_w0), a0=vec(rw_a0),
            k_k=vec(rw_k_k), k_a=vec(rw_k_a), r_k=rw_r_k[i].reshape(1, 1, d), gn_w=vec(rw_gn_w), gn_b=vec(rw_gn_b),
            w_o=rw_w_o[i].astype(bf16)))
    gl = []
    for i in range(n_b):
        n_main = gla_w_in.shape[2] - gla_a_w2.shape[1]
        gl.append(dict(
            w_main=gla_w_in[i, :, :n_main].astype(bf16),
            w_gate=_pad_to(gla_w_in[i, :, n_main:], 1, GLA_LORA_PAD).astype(bf16),
            a_w2=_pad_to(gla_a_w2[i], 0, GLA_LORA_PAD).astype(bf16), a_b=gla_a_b[i].reshape(1, 1, -1),
            o_norm=gla_o_norm[i].reshape(1, -1), w_o=gla_w_o[i].astype(bf16)))
    s5 = [_s5_params(s5_A_re[i], s5_A_im[i], s5_log_dt[i], s5_B_re[i], s5_B_im[i], s5_C_re[i], s5_C_im[i],
                     s5_D[i], s5_glu_w[i], s5_glu_b[i]) for i in range(n_c)]
    weights = dict(norm_g=norm_g, ffn_w_in=ffn_w_in, ffn_w_out=ffn_w_out,
                   rw=rw, gla=gl, s5=s5, final_g=final_g)

    zeros = lambda s: jnp.zeros((s.shape[0], bp) + s.shape[2:], f32)
    p_states = tuple(zeros(s) for s in (state_rwkv_wkv, state_rwkv_shift, state_gla, state_s5_re, state_s5_im))
    s_states = (state_rwkv_wkv, state_rwkv_shift, state_gla, state_s5_re, state_s5_im)

    tt_p = min(ROW_TILE, t_p)
    nb_s = min(bs, ROW_TILE // t_s)
    rw_chunk = min(RW_HEAD, t_p)
    n_heads = d // RW_HEAD
    nb_scan = max(1, min(bs, RW_HEAD // t_s))
    ffn_bf16 = {}
    y_s, s_wkv, s_shift, s_gla, s_re, s_im = _trunk(x_sample.transpose(1, 0, 2), ada_s, s_states, weights, nb_s, t_s,
                                                     (nb_scan, t_s, t_s, min(RW_SCAN_HEADS_SAMPLE, n_heads)),
                                                     (min(bs, FFN_ROWS_SAMPLE // t_s), t_s), ffn_bf16)
    y_s = y_s.transpose(1, 0, 2)
    y_p, p_wkv, p_shift, p_gla, p_re, p_im = _trunk(x_prompt, ada_p, p_states, weights, 1, tt_p,
                                                     (1, rw_chunk, min(RW_SCAN_TBLK, t_p), min(RW_SCAN_HEADS_PROMPT, n_heads)),
                                                     (1, min(FFN_ROWS, t_p)), ffn_bf16)
    return (y_p, y_s, p_wkv, p_shift, p_gla, p_re, p_im, s_wkv, s_shift, s_gla, s_re, s_im)
```

```python
import functools
import math

import jax
import jax.numpy as jnp
from jax import lax
from jax.experimental import pallas as pl
from jax.experimental.pallas import tpu as pltpu

f32 = jnp.float32
bf16 = jnp.bfloat16
HI = lax.Precision.HIGHEST

EPS = 1e-6
RW_HEAD = 64
RW_GN_EPS = 64e-5
RW_LORA_PAD = 128
GLA_GATE_NORM = 16.0
GLA_CHUNK = 64
GLA_LORA_PAD = 128
S5_SLAB_GROUPS = 8
LANE = 128
SUBLANE = 8
VMEM_LIMIT_BYTES = 56 * 1024 * 1024

ROW_TILE = 512
FFN_ROWS = 512
FFN_ROWS_SAMPLE = 1024
FFN_TILE_F_CAST = 256
FFN_TILE_F = 512
FFN_EW_ROWS = 256
MM_TILE_N = 512
ADA_TILE_N = 2048
RW_PROJ_ROWS = 512
RW_PROJ_TILE_N = 256
RW_SCAN_TBLK = 256
RW_SCAN_HEADS_PROMPT = 8
RW_SCAN_HEADS_SAMPLE = 16
GLA_SCAN_TBLK = 512
GLA_IN_ROWS = 1024
GLA_SCAN_HEADS_PROMPT = 2
GLA_SCAN_HEADS_SAMPLE = 4
S5_TBLK = 128
S5_GLU_ROWS = 256
S5_SEQS_SAMPLE = 8
S5_CARRY_ELEMS = 8192


def _cparams(*sem):
    return pltpu.CompilerParams(dimension_semantics=sem, vmem_limit_bytes=VMEM_LIMIT_BYTES)


def _dot(a, b):
    return jnp.dot(a, b, preferred_element_type=f32)


def _dot_hi(a, b):
    return jnp.dot(a, b, precision=HI, preferred_element_type=f32)


def _normmod(x, g, shift, scale):
    ms = jnp.mean(x * x, axis=-1, keepdims=True)
    y = x * lax.rsqrt(ms + EPS) * g
    return y * (1.0 + scale) + shift


def _ada_kernel(c_ref, w_ref, b_ref, o_ref):
    c = c_ref[...]
    s = (c * jax.nn.sigmoid(c)).astype(bf16)
    o_ref[0, 0] = _dot(s, w_ref[0].astype(bf16)) + b_ref[0]


def _ada_call(c_all, ada_w, ada_b, d_model):
    depth, d, n = ada_w.shape
    m = c_all.shape[0]
    tn = min(ADA_TILE_N, d_model)
    per_vec = d_model // tn
    return pl.pallas_call(
        _ada_kernel,
        grid=(depth, n // tn),
        in_specs=[pl.BlockSpec((m, d), lambda l, j: (0, 0)),
                  pl.BlockSpec((1, d, tn), lambda l, j: (l, 0, j)),
                  pl.BlockSpec((1, 1, tn), lambda l, j: (l, 0, j))],
        out_specs=pl.BlockSpec((1, 1, m, tn), lambda l, j: (l, j // per_vec, 0, j % per_vec)),
        out_shape=jax.ShapeDtypeStruct((depth, n // d_model, m, d_model), f32),
        compiler_params=_cparams("arbitrary", "arbitrary"),
        name="ada",
    )(c_all, ada_w, ada_b.reshape(depth, 1, n))


def _ffn_kernel(x_ref, g_ref, sh_ref, sc_ref, gt_ref, wg_ref, wu_ref, wo_ref, *refs, emit_bf16):
    if emit_bf16:
        o_ref, wgo_ref, wuo_ref, woo_ref, h_scr, acc_scr = refs
    else:
        o_ref, h_scr, acc_scr = refs
    j = pl.program_id(1)
    nb, tt, d = x_ref.shape
    if nb > 1:
        step = max(1, min(nb, FFN_EW_ROWS // tt))
        chunks = [(slice(a, a + step), slice(None), a * tt, step * tt) for a in range(0, nb, step)]
    else:
        step = min(tt, FFN_EW_ROWS)
        chunks = [(slice(None), slice(a, a + step), a, step) for a in range(0, tt, step)]

    @pl.when(j == 0)
    def _():
        for s0, s1, r0, nr in chunks:
            h = _normmod(x_ref[s0, s1, :], g_ref[...], _seq_vec(sh_ref), _seq_vec(sc_ref))
            h_scr[r0:r0 + nr, :] = h.reshape(nr, d).astype(bf16)

    @pl.when(j == 0)
    def _():
        acc_scr[...] = jnp.zeros_like(acc_scr)

    hb = h_scr[...]
    wg, wu, wo = wg_ref[...], wu_ref[...], wo_ref[...]
    if emit_bf16:
        wg, wu, wo = wg.astype(bf16), wu.astype(bf16), wo.astype(bf16)
        wgo_ref[...] = wg
        wuo_ref[...] = wu
        woo_ref[...] = wo
    gate = _dot(hb, wg)
    up = _dot(hb, wu)
    act = (gate * jax.nn.sigmoid(gate) * up).astype(bf16)
    acc_scr[...] += _dot(act, wo)

    @pl.when(j == pl.num_programs(1) - 1)
    def _():
        for s0, s1, r0, nr in chunks:
            shp = x_ref[s0, s1, :].shape
            o_ref[s0, s1, :] = x_ref[s0, s1, :] + 0.5 * _seq_vec(gt_ref) * acc_scr[r0:r0 + nr, :].reshape(shp)


def _ffn_call(x, g, shift, scale, gate, weights, nb, tt):
    d = x.shape[-1]
    emit = weights[0] == "f32"
    til = _row_tiling(x.shape, nb, tt)
    row_mode = dict(pipeline_mode=pl.Buffered(1)) if til.grid == 1 else {}
    rows = pl.BlockSpec(til.block(d), til.rmap, **row_mode)
    mod = til.mod_spec(d, **row_mode)
    if emit:
        _, w_in, w_out, l, s = weights
        f = w_out.shape[2]
        tf = min(FFN_TILE_F_CAST, f)
        nf = f // tf
        w_args = (w_in, w_in, w_out)
        w_specs = [pl.BlockSpec((None, None, d, tf), lambda i, j: (l, s, 0, j)),
                   pl.BlockSpec((None, None, d, tf), lambda i, j: (l, s, 0, nf + j)),
                   pl.BlockSpec((None, None, tf, d), lambda i, j: (l, s, j, 0))]
        wo_specs = [pl.BlockSpec((None, d, tf), lambda i, j: (i, 0, j)), pl.BlockSpec((None, d, tf), lambda i, j: (i, 0, j)),
                    pl.BlockSpec((None, tf, d), lambda i, j: (i, j, 0))]
        wo_shapes = [jax.ShapeDtypeStruct((til.grid, d, f), bf16), jax.ShapeDtypeStruct((til.grid, d, f), bf16),
                     jax.ShapeDtypeStruct((til.grid, f, d), bf16)]
    else:
        _, wg, wu, wo = weights
        f = wo.shape[1]
        tf = min(FFN_TILE_F, f)
        nf = f // tf
        w_args = (wg, wu, wo)
        w_specs = [pl.BlockSpec((None, d, tf), lambda i, j: (0, 0, j)), pl.BlockSpec((None, d, tf), lambda i, j: (0, 0, j)),
                   pl.BlockSpec((None, tf, d), lambda i, j: (0, j, 0))]
        wo_specs, wo_shapes = [], []
    out = pl.pallas_call(
        functools.partial(_ffn_kernel, emit_bf16=emit),
        grid=(til.grid, nf),
        in_specs=[rows, pl.BlockSpec((1, 1, d), lambda i, j: (0, 0, 0)), mod, mod, mod] + w_specs,
        out_specs=[rows] + wo_specs,
        out_shape=[jax.ShapeDtypeStruct(x.shape, f32)] + wo_shapes,
        scratch_shapes=[pltpu.VMEM((til.rows, d), bf16), pltpu.VMEM((til.rows, d), f32)],
        compiler_params=_cparams("parallel", "arbitrary"),
        name="ffn",
    )(x, g, _dense_seq_vec(shift, nb), _dense_seq_vec(scale, nb), _dense_seq_vec(gate, nb), *w_args)
    return (out[0], tuple(out[1:])) if emit else out[0]


class _RowTiling:
    def __init__(self, shape, nb, tt):
        self.nb = nb
        if nb > 1:
            t, nseq, _ = shape
            assert tt == t and nseq % nb == 0, (shape, nb, tt)
            self.grid, self.rows = nseq // nb, t * nb
            self.block = lambda w: (t, nb, w)
            self.rmap = lambda i, j: (0, i, 0)
            self.cmap = lambda i, j: (0, i, j)
            self.mod_spec = lambda w, **kw: pl.BlockSpec((nb, w), lambda i, j: (i, 0), **kw)
            self.emod_spec = lambda w: pl.BlockSpec((nb, w), lambda i, j: (i, j))
        else:
            nseq, t, _ = shape
            per = t // tt
            assert per * tt == t, (shape, tt)
            self.per = per
            self.grid, self.rows = nseq * per, tt
            self.block = lambda w: (1, tt, w)
            self.rmap = lambda i, j: (i // per, i % per, 0)
            self.cmap = lambda i, j: (i // per, i % per, j)
            self.mod_spec = lambda w, **kw: pl.BlockSpec((1, 1, w), lambda i, j: (i // per, 0, 0), **kw)
            self.emod_spec = lambda w: pl.BlockSpec((1, 1, w), lambda i, j: (i // per, 0, j))


def _row_tiling(shape, nb, tt):
    return _RowTiling(shape, nb, tt)


def _dense_seq_vec(a, nb):
    return a.reshape(a.shape[0], a.shape[-1]) if nb > 1 else a


def _seq_vec(ref):
    v = ref[...]
    return v if v.ndim == 3 else v[None, :, :]


def _mm_kernel(*refs, n_row, n_mod, n_vec, n_w, n_erow, n_emod, n_evec, n_out, prologue, epilogue):
    pos = 0

    def take(n):
        nonlocal pos
        out = refs[pos:pos + n]
        pos += n
        return out

    rows, mods, vecs, ws = take(n_row), take(n_mod), take(n_vec), take(n_w)
    erows, emods, evecs, outs = take(n_erow), take(n_emod), take(n_evec), take(n_out)
    a_scr = refs[pos]
    nb, tt, k = rows[0].shape
    tn = ws[0].shape[-1]

    @pl.when(pl.program_id(1) == 0)
    def _():
        a = prologue([r[...] for r in rows], [_seq_vec(m) for m in mods], [v[...] for v in vecs])
        a_scr[...] = a.reshape(nb * tt, k).astype(bf16)

    ab = a_scr[...]
    accs = [_dot(ab, w[...]).reshape(nb, tt, tn) for w in ws]
    res = epilogue(accs, [r[...] for r in erows], [_seq_vec(m) for m in emods], [v[...] for v in evecs])
    for o_ref, o in zip(outs, res):
        o_ref[...] = o


def _mm_call(name, rows, mods, vecs, ws, erows, emods, evecs, n_total, n_out, prologue, epilogue, nb, tt, tn=MM_TILE_N,
             resident_w=False):
    k = rows[0].shape[-1]
    tn = min(tn, n_total)
    assert n_total % tn == 0, (name, n_total, tn)
    nj = n_total // tn
    w_mode = dict(pipeline_mode=pl.Buffered(1)) if resident_w else {}
    assert not resident_w or nj == 1
    til = _row_tiling(rows[0].shape, nb, tt)
    mods = [_dense_seq_vec(m, nb) for m in mods]
    emods = [_dense_seq_vec(m, nb) for m in emods]
    in_specs = ([pl.BlockSpec(til.block(k), til.rmap)] * len(rows)
                + [til.mod_spec(k)] * len(mods)
                + [pl.BlockSpec((1, 1, k), lambda i, j: (0, 0, 0))] * len(vecs)
                + [pl.BlockSpec((k, tn), functools.partial(lambda i, j, off: (0, off + j), off=off), **w_mode)
                   for _, off in ws]
                + [pl.BlockSpec(til.block(tn), til.cmap)] * len(erows)
                + [til.emod_spec(tn)] * len(emods)
                + [pl.BlockSpec((1, 1, tn), lambda i, j: (0, 0, j))] * len(evecs))
    kern = functools.partial(_mm_kernel, n_row=len(rows), n_mod=len(mods), n_vec=len(vecs), n_w=len(ws),
                             n_erow=len(erows), n_emod=len(emods), n_evec=len(evecs), n_out=n_out,
                             prologue=prologue, epilogue=epilogue)
    out = pl.pallas_call(
        kern,
        grid=(til.grid, nj),
        in_specs=in_specs,
        out_specs=[pl.BlockSpec(til.block(tn), til.cmap)] * n_out,
        out_shape=[jax.ShapeDtypeStruct(rows[0].shape[:2] + (n_total,), f32)] * n_out,
        scratch_shapes=[pltpu.VMEM((til.rows, k), bf16)],
        compiler_params=_cparams("parallel", "arbitrary"),
        name=name,
    )(*rows, *mods, *vecs, *[w for w, _ in ws], *erows, *emods, *evecs)
    return out


def _pro_normmod(rows, mods, vecs):
    return _normmod(rows[0], vecs[0], mods[0], mods[1])


def _pro_mul(rows, mods, vecs):
    return rows[0] * rows[1]


def _pro_id(rows, mods, vecs):
    return rows[0]


def _epi_id(accs, erows, emods, evecs):
    return accs


def _epi_residual(accs, erows, emods, evecs):
    return [erows[0] + emods[0] * accs[0]]


def _rms_kernel(x_ref, g_ref, o_ref):
    x = x_ref[...]
    o_ref[...] = x * lax.rsqrt(jnp.mean(x * x, axis=-1, keepdims=True) + EPS) * g_ref[...]


def _final_norm_call(x, g, nb, tt):
    d = x.shape[-1]
    til = _row_tiling(x.shape, nb, tt)
    rmap = lambda i: til.rmap(i, 0)
    return pl.pallas_call(
        _rms_kernel,
        grid=(til.grid,),
        in_specs=[pl.BlockSpec(til.block(d), rmap), pl.BlockSpec((1, 1, d), lambda i: (0, 0, 0))],
        out_specs=pl.BlockSpec(til.block(d), rmap),
        out_shape=jax.ShapeDtypeStruct(x.shape, f32),
        compiler_params=_cparams("parallel"),
        name="final_norm",
    )(x, g)


def _rwproj_kernel(x_ref, xprev_ref, sp_ref, ng_ref, sh_ref, sc_ref, mu_ref, w1_ref, a1_ref, g1_ref,
                   wr_ref, wk_ref, wv_ref, w2_ref, a2_ref, g2_ref, w0_ref, a0_ref,
                   r_ref, k_ref, v_ref, lw_ref, a_ref, g_ref, hl_ref, xm_scr, tw_scr, ta_scr, tg_scr, *, tiles_per_seq,
                   time_major):
    b0, b1, d = x_ref.shape
    rows = b0 * b1
    tn = wr_ref.shape[-1]

    @pl.when(pl.program_id(1) == 0)
    def _():
        ng, sh, sc = ng_ref[...], _seq_vec(sh_ref), _seq_vec(sc_ref)
        h3 = _normmod(x_ref[...], ng, sh, sc)
        first = _seq_vec(sp_ref)
        if time_major:
            hp3 = jnp.concatenate([first, h3[:-1]], axis=0)
            hl_ref[...] = h3[b0 - 1:b0]
        else:
            if tiles_per_seq > 1:
                h_before = _normmod(xprev_ref[...], ng, sh, sc)[:, SUBLANE - 1:SUBLANE, :]
                first = jnp.where(pl.program_id(0) % tiles_per_seq == 0, first, h_before)
            tok = lax.broadcasted_iota(jnp.int32, h3.shape, 1)
            hp3 = jnp.where(tok == 0, first, pltpu.roll(h3, 1, axis=1))
            hl_ref[...] = h3[:, b1 - SUBLANE:, :]
        h = h3.reshape(rows, d)
        dlt = hp3.reshape(rows, d) - h
        mu = mu_ref[...]
        for p in range(3):
            xm_scr[p] = (h + dlt * mu[p:p + 1]).astype(bf16)
        xw = (h + dlt * mu[3:4]).astype(bf16)
        tw_scr[...] = jnp.tanh(_dot(xw, w1_ref[...])).astype(bf16)
        xa = (h + dlt * mu[4:5]).astype(bf16)
        ta_scr[...] = _dot(xa, a1_ref[...]).astype(bf16)
        xg = (h + dlt * mu[5:6]).astype(bf16)
        tg_scr[...] = jax.nn.sigmoid(_dot(xg, g1_ref[...])).astype(bf16)

    shp = (b0, b1, tn)
    r_ref[...] = _dot(xm_scr[0], wr_ref[...]).reshape(shp)
    k_ref[...] = _dot(xm_scr[1], wk_ref[...]).reshape(shp)
    v_ref[...] = _dot(xm_scr[2], wv_ref[...]).reshape(shp)
    w_log = -jax.nn.softplus(-(w0_ref[0] + _dot(tw_scr[...], w2_ref[...]))) - 0.5
    lw_ref[...] = (-jnp.exp(w_log)).reshape(shp)
    a_ref[...] = jax.nn.sigmoid(a0_ref[0] + _dot(ta_scr[...], a2_ref[...])).reshape(shp)
    g_ref[...] = _dot(tg_scr[...], g2_ref[...]).reshape(shp)


def _rwproj_call(x, shift_prev, ng, shift, scale, mu, w1, a1, g1, wr, wk, wv, w2, a2, g2, w0, a0, nb, tt):
    d = x.shape[-1]
    tn = min(RW_PROJ_TILE_N, d)
    til = _row_tiling(x.shape, nb, tt)
    if nb > 1:
        per = 1
        pmap = til.rmap
        prev_block = til.block(d)
        tail_spec = pl.BlockSpec((1, nb, d), til.rmap)
        tail_shape = (1, x.shape[1], d)
    else:
        per = til.per
        pmap = lambda i, j: (i // per, jnp.maximum((i % per) * (tt // SUBLANE) - 1, 0), 0)
        prev_block = (1, SUBLANE, d)
        tail_spec = pl.BlockSpec((1, SUBLANE, d), til.rmap)
        tail_shape = (x.shape[0], per * SUBLANE, d)
    mod = til.mod_spec(d)
    full = lambda a: pl.BlockSpec(a.shape, lambda i, j: (0,) * a.ndim)
    col = lambda a: pl.BlockSpec((a.shape[0], tn), lambda i, j: (0, j))
    shift_prev, shift, scale = (_dense_seq_vec(a, nb) for a in (shift_prev, shift, scale))
    rows = til.rows
    kern = functools.partial(_rwproj_kernel, tiles_per_seq=per, time_major=nb > 1)
    return pl.pallas_call(
        kern,
        grid=(til.grid, d // tn),
        in_specs=[pl.BlockSpec(til.block(d), til.rmap), pl.BlockSpec(prev_block, pmap), mod,
                  pl.BlockSpec((1, 1, d), lambda i, j: (0, 0, 0)), mod, mod, full(mu), full(w1), full(a1), full(g1),
                  col(wr), col(wk), col(wv), col(w2), col(a2), col(g2),
                  pl.BlockSpec((1, 1, tn), lambda i, j: (0, 0, j)), pl.BlockSpec((1, 1, tn), lambda i, j: (0, 0, j))],
        out_specs=[pl.BlockSpec(til.block(tn), til.cmap)] * 6 + [tail_spec],
        out_shape=[jax.ShapeDtypeStruct(x.shape, f32)] * 6 + [jax.ShapeDtypeStruct(tail_shape, f32)],
        scratch_shapes=[pltpu.VMEM((3, rows, d), bf16), pltpu.VMEM((rows, w1.shape[1]), bf16),
                        pltpu.VMEM((rows, a1.shape[1]), bf16), pltpu.VMEM((rows, g1.shape[1]), bf16)],
        compiler_params=_cparams("parallel", "arbitrary"),
        name="rwkv_proj",
    )(x, x, shift_prev, ng, shift, scale, mu, w1, a1, g1, wr, wk, wv, w2, a2, g2, w0, a0)


def _dot_nt(a, b):
    return lax.dot_general(a, b, (((1,), (1,)), ((), ())), preferred_element_type=f32)


def _dot_tn(a, b):
    return lax.dot_general(a, b, (((0,), (0,)), ((), ())), preferred_element_type=f32)


def _rwkv_pair_kernel(r_ref, k_ref, v_ref, lw_ref, a_ref, kk_ref, ka_ref, rk_ref, gw_ref, gb_ref, s0_ref,
                      y_ref, sout_ref, s_scr, rp_scr, y0_scr, bon_scr, m_scr, n_scr, *, nb, tt, n_inner):
    c = pl.program_id(2)
    n = RW_HEAD
    w2 = 2 * n
    big = nb * tt
    n_dbl = max(1, math.ceil(math.log2(tt)))
    pp = r_ref.shape[-1] // w2

    lane_r = lax.broadcasted_iota(jnp.int32, (w2, w2), 0)
    lane_c = lax.broadcasted_iota(jnp.int32, (w2, w2), 1)
    bd = (lane_r // n) == (lane_c // n)
    eye_w = (lane_r == lane_c).astype(f32)

    @pl.when(c == 0)
    def _():
        s_scr[...] = jnp.zeros_like(s_scr)
        for i in range(nb):
            for p in range(pp):
                s_scr[i, p, 0:n, 0:n] = s0_ref[i, 2 * p]
                s_scr[i, p, n:w2, n:w2] = s0_ref[i, 2 * p + 1]

    ti = lax.broadcasted_iota(jnp.int32, (big, w2), 0)
    si = lax.broadcasted_iota(jnp.int32, (big, w2), 1) % big
    same2 = (ti // tt) == (si // tt)
    incl2 = jnp.logical_and(same2, si <= ti)
    strict2 = jnp.logical_and(same2, si < ti)
    upper = lax.broadcasted_iota(jnp.int32, (big, w2), 1) >= big
    cumsum = _make_masked_sum(lambda t_, s_: jnp.logical_and((t_ // tt) == (s_ // tt), s_ <= t_), big)
    seqsum = _make_masked_sum(lambda t_, s_: (t_ // tt) == (s_ // tt), big)
    eye_f = (lax.broadcasted_iota(jnp.int32, (big, big), 0) == lax.broadcasted_iota(jnp.int32, (big, big), 1)).astype(f32)
    head_a = lax.broadcasted_iota(jnp.int32, (big, w2), 1) < n
    zeros_lw = jnp.zeros((big, w2), bf16)

    def head_sum(x):
        s_a = jnp.sum(jnp.where(head_a, x, 0.0), axis=1, keepdims=True)
        s_b = jnp.sum(jnp.where(head_a, 0.0, x), axis=1, keepdims=True)
        return jnp.where(head_a, s_a, s_b)

    def load(ref, ci):
        if nb > 1:
            return ref[...].reshape(big, pp * w2)
        return ref[0, ci * big:(ci + 1) * big, :]

    ch, slabs = [], []
    for ci in range(n_inner):
        r, k, v, lw, a = (load(ref, ci) for ref in (r_ref, k_ref, v_ref, lw_ref, a_ref))
        cum = cumsum(lw)
        tot = cum[big - 1:big, :] if nb == 1 else seqsum(lw)
        for p in range(pp):
            sl = slice(p * w2, (p + 1) * w2)
            r_p, k_p, v_p, lw_p, a_p, cum_p, tot_p = r[:, sl], k[:, sl], v[:, sl], lw[:, sl], a[:, sl], cum[:, sl], tot[:, sl]
            g_inv = jnp.exp(-cum_p)
            g_rest = jnp.exp(tot_p - cum_p)
            kk = k_p * kk_ref[0][:, sl]
            kk = kk * jnp.minimum(lax.rsqrt(head_sum(kk * kk)), 1e12)
            k2 = k_p * (1.0 + (a_p - 1.0) * ka_ref[0][:, sl])
            b = kk * a_p
            rt = r_p * jnp.exp(cum_p)
            at = -kk * jnp.exp(cum_p - lw_p)
            bhat, khat = b * g_rest, k2 * g_rest
            bon_scr[ci, p] = head_sum(r_p * k2 * rk_ref[0][:, sl]) * v_p
            slab = dict(ci=ci, p=p, bhat=bhat, khat=khat, v=v_p, gt=jnp.exp(tot_p),
                        bk=jnp.concatenate([b * g_inv, k2 * g_inv], axis=0).astype(bf16))
            slabs.append(slab)
            for hd in range(2):
                mh = head_a if hd == 0 else jnp.logical_not(head_a)
                at_h = jnp.where(mh, at, 0.0)
                rt_h = jnp.where(mh, rt, 0.0)
                ch.append(dict(slab=slab, rt=rt_h, at=at_h,
                               ra=jnp.concatenate([rt_h, at_h], axis=0).astype(bf16),
                               vv=jnp.concatenate([zeros_lw, jnp.where(mh, v_p, 0.0).astype(bf16)], axis=0)))
    for q in ch:
        q["amat"] = _dot_nt(q["ra"], q["slab"]["bk"])
    for q in ch:
        amat = q["amat"]
        q["top"] = jnp.where(incl2, amat[:big], 0.0).astype(bf16)
        bot = jnp.where(strict2, amat[big:], 0.0)
        q["bot_k"] = jnp.where(upper, bot, 0.0).astype(bf16)
        q["pw"] = bot[:, :big]
        q["tm"] = eye_f + q["pw"]
    for _ in range(n_dbl - 1):
        for q in ch:
            pwb = q["pw"].astype(bf16)
            q["pw"] = _dot(pwb, pwb)
        for q in ch:
            q["tm"] = q["tm"] + _dot(q["tm"].astype(bf16), q["pw"].astype(bf16))
    for q in ch:
        q["akv"] = _dot(q["bot_k"], q["vv"])
    for q in ch:
        q["wa"] = _dot(q["tm"].astype(bf16), jnp.concatenate([q["akv"], q["at"]], axis=1).astype(bf16))
    for q in ch:
        wab = q["wa"].astype(bf16)
        q["arb_a"] = _dot(q["top"][:, :big], wab[:, w2:])
        q["y0"] = _dot(q["top"], jnp.concatenate([wab[:, :w2], q["vv"][big:]], axis=0))
    for idx, slab in enumerate(slabs):
        qa, qb = ch[2 * idx], ch[2 * idx + 1]
        ci, p = slab["ci"], slab["p"]
        rp_scr[ci, p] = qa["rt"] + qb["rt"] + qa["arb_a"] + qb["arb_a"]
        y0_scr[ci, p] = qa["y0"] + qb["y0"]
        wa = qa["wa"] + qb["wa"]
        w_pair, a_pair = wa[:, :w2], wa[:, w2:]
        for i in range(nb):
            rows = slice(i * tt, (i + 1) * tt)
            bh_i = slab["bhat"][rows].astype(bf16)
            m_scr[ci, p, i] = eye_w * slab["gt"][i * tt:i * tt + 1] + jnp.where(bd, _dot_tn(a_pair[rows].astype(bf16), bh_i), 0.0)
            n_scr[ci, p, i] = jnp.where(bd, _dot_tn(
                jnp.concatenate([w_pair[rows], slab["v"][rows]], axis=0).astype(bf16),
                jnp.concatenate([slab["bhat"][rows], slab["khat"][rows]], axis=0).astype(bf16)), 0.0)

    for ci in range(n_inner):
        ys = []
        for p in range(pp):
            sl = slice(p * w2, (p + 1) * w2)
            rp = rp_scr[ci, p]
            parts = []
            for i in range(nb):
                sb = s_scr[i, p].astype(bf16)
                parts.append(_dot_nt(rp[i * tt:(i + 1) * tt].astype(bf16), sb))
                s_scr[i, p] = _dot(sb, m_scr[ci, p, i].astype(bf16)) + n_scr[ci, p, i]
            y = (parts[0] if nb == 1 else jnp.concatenate(parts, axis=0)) + y0_scr[ci, p]
            mean = head_sum(y) * (1.0 / n)
            var = head_sum(jnp.square(y - mean)) * (1.0 / n)
            ys.append((y - mean) * lax.rsqrt(var + RW_GN_EPS) * gw_ref[0][:, sl] + gb_ref[0][:, sl] + bon_scr[ci, p])
        yo = ys[0] if pp == 1 else jnp.concatenate(ys, axis=1)
        if nb > 1:
            y_ref[...] = yo.reshape(nb, tt, pp * w2)
        else:
            y_ref[0, ci * big:(ci + 1) * big, :] = yo

    @pl.when(c == pl.num_programs(2) - 1)
    def _():
        for i in range(nb):
            for p in range(pp):
                sout_ref[i, 2 * p] = s_scr[i, p, 0:n, 0:n]
                sout_ref[i, 2 * p + 1] = s_scr[i, p, n:w2, n:w2]


def _rwscan_call(r, k, v, lw, a, k_k, k_a, r_k, gn_w, gn_b, s0_all, s_new_all, layer, nb, tt, tblk, hp):
    nseq, t, d = r.shape
    nh = d // RW_HEAD
    lanes = hp * RW_HEAD
    nc = t // tblk
    n_inner = tblk // tt if nb == 1 else 1
    xspec = pl.BlockSpec((nb, tblk, lanes), lambda i, p, c: (i, c, p))
    pspec = pl.BlockSpec((1, 1, lanes), lambda i, p, c: (0, 0, p))
    assert hp % 2 == 0 and nb * tt == RW_HEAD, (hp, nb, tt)
    pp, w2 = hp // 2, 2 * RW_HEAD
    base = functools.partial(_rwkv_pair_kernel, nb=nb, tt=tt, n_inner=n_inner)
    chunk_rows = pltpu.VMEM((n_inner, pp, nb * tt, w2), f32)
    chunk_mats = pltpu.VMEM((n_inner, pp, nb, w2, w2), f32)
    layer_spec = pl.BlockSpec((None, nb, hp, RW_HEAD, RW_HEAD), lambda i, p, c: (layer, i, p, 0, 0))
    n_in = 11
    if s_new_all is None:
        kern, extra_specs, extra_args, aliases = base, [], [], {}
    else:
        kern = lambda *refs: base(*refs[:n_in], *refs[n_in + 1:])
        extra_specs, extra_args, aliases = [pl.BlockSpec(memory_space=pl.ANY)], [s_new_all], {n_in: 1}
    return pl.pallas_call(
        kern,
        grid=(nseq // nb, nh // hp, nc),
        in_specs=[xspec] * 5 + [pspec] * 5 + [layer_spec] + extra_specs,
        out_specs=[xspec, layer_spec],
        out_shape=[jax.ShapeDtypeStruct((nseq, t, d), f32), jax.ShapeDtypeStruct(s0_all.shape, f32)],
        input_output_aliases=aliases,
        scratch_shapes=[pltpu.VMEM((nb, pp, w2, w2), f32), chunk_rows, chunk_rows, chunk_rows,
                        chunk_mats, chunk_mats],
        compiler_params=_cparams("parallel", "parallel", "arbitrary"),
        name="rwkv_scan",
    )(r, k, v, lw, a, k_k, k_a, r_k, gn_w, gn_b, s0_all, *extra_args)


def _split3(x):
    hi = x.astype(bf16)
    r1 = x - hi.astype(f32)
    mid = r1.astype(bf16)
    return hi, mid, (r1 - mid.astype(f32)).astype(bf16)


def _make_masked_sum(mask_fn, rows):
    if (3 * rows) % 16:
        ti = lax.broadcasted_iota(jnp.int32, (rows, rows), 0)
        si = lax.broadcasted_iota(jnp.int32, (rows, rows), 1)
        mask_f = mask_fn(ti, si).astype(f32)
        return lambda x: _dot_hi(mask_f, x)
    ti = lax.broadcasted_iota(jnp.int32, (rows, 3 * rows), 0)
    si = lax.broadcasted_iota(jnp.int32, (rows, 3 * rows), 1) % rows
    mask3 = jnp.where(mask_fn(ti, si), 1.0, 0.0).astype(bf16)
    return lambda x: _dot(mask3, jnp.concatenate(_split3(x), axis=0))


def _glascan_kernel(q_ref, k_ref, v_ref, r_ref, la_ref, on_ref, s0_ref, o_ref, sout_ref, s_scr, *, chunk, n_inner, hb):
    c = pl.program_id(2)
    hk = q_ref.shape[-1] // hb
    hv = v_ref.shape[-1] // hb

    @pl.when(c == 0)
    def _():
        s_scr[...] = s0_ref[0]

    ti = lax.broadcasted_iota(jnp.int32, (chunk, chunk), 0)
    si = lax.broadcasted_iota(jnp.int32, (chunk, chunk), 1)
    causal = si <= ti
    cumsum = _make_masked_sum(lambda t_, s_: s_ <= t_, chunk)

    ch = []
    for ci in range(n_inner):
        rows = slice(ci * chunk, (ci + 1) * chunk)
        for h in range(hb):
            ks, vs = slice(h * hk, (h + 1) * hk), slice(h * hv, (h + 1) * hv)
            ch.append(dict(rows=rows, h=h, vs=vs, q=q_ref[0, rows, ks], k=k_ref[0, rows, ks],
                           vb=v_ref[0, rows, vs].astype(bf16), la=la_ref[0, rows, ks]))
    for z in ch:
        z["b"] = cumsum(z["la"])
    for z in ch:
        b = z["b"]
        b_last = b[chunk - 1:chunk, :]
        z["q_in"] = (z["q"] * (hk ** -0.5) * jnp.exp(b)).astype(bf16)
        z["k_in"] = (z["k"] * jnp.exp(-b)).astype(bf16)
        z["kd"] = (z["k"] * jnp.exp(b_last - b)).astype(bf16)
        z["dec"] = jnp.transpose(jnp.broadcast_to(jnp.exp(b_last), (SUBLANE, hk)))[:, 0:1]
    for z in ch:
        z["att"] = _dot_nt(z["q_in"], z["k_in"])
    for z in ch:
        z["ov"] = _dot(jnp.where(causal, z["att"], 0.0).astype(bf16), z["vb"])
        z["inc"] = _dot_tn(z["kd"], z["vb"])

    for z in ch:
        h, rows, vs = z["h"], z["rows"], z["vs"]
        s = s_scr[h]
        o = z["ov"] + _dot(z["q_in"], s.astype(bf16))
        s_scr[h] = s * z["dec"] + z["inc"]
        o = o * lax.rsqrt(jnp.mean(o * o, axis=-1, keepdims=True) + EPS) * on_ref[...]
        rr = r_ref[0, rows, vs]
        o_ref[0, rows, vs] = o * (rr * jax.nn.sigmoid(rr))

    @pl.when(c == pl.num_programs(2) - 1)
    def _():
        sout_ref[0] = s_scr[...]


def _glascan_call(qkvr, la, o_norm, s0, tblk, hb):
    nseq, t, _ = qkvr.shape
    _, gh, hk, hv = s0.shape
    chunk = math.gcd(t, GLA_CHUNK)
    tblk = min(tblk, t)
    k_off = (gh * hk) // (hb * hk)
    v_off = (2 * gh * hk) // (hb * hv)
    r_off = (2 * gh * hk + gh * hv) // (hb * hv)
    kern = functools.partial(_glascan_kernel, chunk=chunk, n_inner=tblk // chunk, hb=hb)
    return pl.pallas_call(
        kern,
        grid=(nseq, gh // hb, t // tblk),
        in_specs=[pl.BlockSpec((1, tblk, hb * hk), lambda b, h, c: (b, c, h)),
                  pl.BlockSpec((1, tblk, hb * hk), lambda b, h, c: (b, c, k_off + h)),
                  pl.BlockSpec((1, tblk, hb * hv), lambda b, h, c: (b, c, v_off + h)),
                  pl.BlockSpec((1, tblk, hb * hv), lambda b, h, c: (b, c, r_off + h)),
                  pl.BlockSpec((1, tblk, hb * hk), lambda b, h, c: (b, c, h)),
                  pl.BlockSpec((1, hv), lambda b, h, c: (0, 0)),
                  pl.BlockSpec((1, hb, hk, hv), lambda b, h, c: (b, h, 0, 0))],
        out_specs=[pl.BlockSpec((1, tblk, hb * hv), lambda b, h, c: (b, c, h)),
                   pl.BlockSpec((1, hb, hk, hv), lambda b, h, c: (b, h, 0, 0))],
        out_shape=[jax.ShapeDtypeStruct((nseq, t, gh * hv), f32), jax.ShapeDtypeStruct(s0.shape, f32)],
        scratch_shapes=[pltpu.VMEM((hb, hk, hv), f32)],
        compiler_params=_cparams("parallel", "parallel", "arbitrary"),
        name="gla_scan",
    )(qkvr, qkvr, qkvr, qkvr, la, o_norm, s0)


def _s5prep_kernel(are_ref, aim_ref, ldt_ref, bre_ref, bim_ref, abre_ref, abim_ref, bbre_ref, bbim_ref):
    a_re, a_im = are_ref[...], aim_ref[...]
    dt = jnp.exp(ldt_ref[...])
    mag = jnp.exp(a_re * dt)
    ab_re, ab_im = mag * jnp.cos(a_im * dt), mag * jnp.sin(a_im * dt)
    den = a_re * a_re + a_im * a_im
    nr = ab_re - 1.0
    cf_re = (nr * a_re + ab_im * a_im) / den
    cf_im = (ab_im * a_re - nr * a_im) / den
    abre_ref[...] = ab_re
    abim_ref[...] = ab_im
    b_re, b_im = bre_ref[...], bim_ref[...]
    bbre_ref[...] = cf_re * b_re - cf_im * b_im
    bbim_ref[...] = cf_re * b_im + cf_im * b_re


def _s5prep_call(a_re, a_im, log_dt, b_re_t, b_im_t):
    g, n = a_re.shape
    return pl.pallas_call(
        _s5prep_kernel,
        out_shape=[jax.ShapeDtypeStruct((g, 1, n), f32)] * 2 + [jax.ShapeDtypeStruct(b_re_t.shape, f32)] * 2,
        name="s5_prep",
    )(a_re.reshape(g, 1, n), a_im.reshape(g, 1, n), log_dt.reshape(g, 1, 1), b_re_t, b_im_t)


def _s5scan_kernel(x_ref, g_ref, sh_ref, sc_ref, wbu_ref, wc_ref, abre_ref, abim_ref, dsk_ref, x0re_ref, x0im_ref,
                   y_ref, xre_ref, xim_ref, bre_scr, bim_scr, sre_scr, sim_scr, *, lane_chunk, nb):
    c = pl.program_id(1)
    d = x_ref.shape[-1]
    rows = x_ref.shape[0] * x_ref.shape[1]
    tblk = rows // nb
    n_slab = wbu_ref.shape[0]
    half = wbu_ref.shape[2] // 2
    n_state = n_slab * half

    @pl.when(c == 0)
    def _():
        sre_scr[...] = x0re_ref[0]
        sim_scr[...] = x0im_ref[0]

    h = _normmod(x_ref[...], g_ref[...], sh_ref[...], sc_ref[...]).reshape(rows, d)
    hb = h.astype(bf16)
    for s in range(n_slab):
        bu = _dot(hb[:, s * LANE:(s + 1) * LANE], wbu_ref[s])
        bre_scr[:, s * half:(s + 1) * half] = bu[:, :half]
        bim_scr[:, s * half:(s + 1) * half] = bu[:, half:]

    for q in range(n_state // lane_chunk):
        ls = slice(q * lane_chunk, (q + 1) * lane_chunk)
        ar, ai = abre_ref[:, ls], abim_ref[:, ls]

        def step(t, carry):
            xr, xi = carry
            row = pl.ds(pl.multiple_of(t * nb, nb), nb)
            nxr = ar * xr - ai * xi + bre_scr[row, ls]
            nxi = ar * xi + ai * xr + bim_scr[row, ls]
            bre_scr[row, ls] = nxr
            bim_scr[row, ls] = nxi
            return nxr, nxi

        xr, xi = lax.fori_loop(0, tblk, step, (sre_scr[:, ls], sim_scr[:, ls]))
        sre_scr[:, ls] = xr
        sim_scr[:, ls] = xi

    for s in range(n_slab):
        xs = jnp.concatenate([bre_scr[:, s * half:(s + 1) * half], bim_scr[:, s * half:(s + 1) * half]], axis=1)
        ys = _dot(xs.astype(bf16), wc_ref[s])
        sl = slice(s * LANE, (s + 1) * LANE)
        y_ref[:, :, sl] = jax.nn.gelu(ys + dsk_ref[:, sl] * h[:, sl]).reshape(y_ref.shape[0], y_ref.shape[1], LANE)

    @pl.when(c == pl.num_programs(1) - 1)
    def _():
        xre_ref[0] = sre_scr[...]
        xim_ref[0] = sim_scr[...]


def _s5scan_call(x, g, shift, scale, wbu, wc, ab_re, ab_im, d_skip, x0_re, x0_im, tblk, nb):
    if nb > 1:
        t, nseq, d = x.shape
        tblk = t
        shift, scale = shift.reshape(1, nseq, d), scale.reshape(1, nseq, d)
        xspec = pl.BlockSpec((t, nb, d), lambda b, c: (0, b, 0))
        mod = pl.BlockSpec((1, nb, d), lambda b, c: (0, b, 0))
    else:
        nseq, t, d = x.shape
        tblk = min(tblk, t)
        xspec = pl.BlockSpec((1, tblk, d), lambda b, c: (b, c, 0))
        mod = pl.BlockSpec((1, 1, d), lambda b, c: (b, 0, 0))
    n_state = ab_re.shape[1]
    full = lambda a: pl.BlockSpec(a.shape, lambda b, c: (0,) * a.ndim)
    st = pl.BlockSpec((1, nb, n_state), lambda b, c: (b, 0, 0))
    st_shape = jax.ShapeDtypeStruct((nseq // nb, nb, n_state), f32)
    kern = functools.partial(_s5scan_kernel, lane_chunk=min(n_state, S5_CARRY_ELEMS // nb), nb=nb)
    rows = nb * tblk
    return pl.pallas_call(
        kern,
        grid=(nseq // nb, t // tblk),
        in_specs=[xspec, pl.BlockSpec((1, 1, d), lambda b, c: (0, 0, 0)),
                  mod, mod, full(wbu), full(wc), full(ab_re), full(ab_im), full(d_skip), st, st],
        out_specs=[xspec, st, st],
        out_shape=[jax.ShapeDtypeStruct(x.shape, f32), st_shape, st_shape],
        scratch_shapes=[pltpu.VMEM((rows, n_state), f32), pltpu.VMEM((rows, n_state), f32),
                        pltpu.VMEM((nb, n_state), f32), pltpu.VMEM((nb, n_state), f32)],
        compiler_params=_cparams("parallel", "arbitrary"),
        name="s5_scan",
    )(x, g, shift, scale, wbu, wc, ab_re, ab_im, d_skip,
      x0_re.reshape(nseq // nb, nb, n_state), x0_im.reshape(nseq // nb, nb, n_state))


def _epi_glu(accs, erows, emods, evecs):
    val = accs[0] + evecs[0]
    gate = accs[1] + evecs[1]
    return [erows[0] + emods[0] * (val * jax.nn.sigmoid(gate))]


def _rwkv_mixer(x, g, shift, scale, gm, shift_prev, wkv_all, wkv_new_all, layer, p, nb, tt, scan_cfg):
    t, d = (x.shape[0] if nb > 1 else x.shape[1]), x.shape[-1]
    pnb, ptt = (RW_PROJ_ROWS // t, t) if nb > 1 else (1, min(RW_PROJ_ROWS, t))
    r, k, v, lw, a, gg, h_tail = _rwproj_call(x, shift_prev[:, None, :], g, shift, scale, p["mu"], p["w1"], p["a1"], p["g1"],
                                               p["wr"], p["wk"], p["wv"], p["w2"], p["a2"], p["g2"], p["w0"], p["a0"], pnb, ptt)
    seq_major = (lambda z: z.transpose(1, 0, 2)) if nb > 1 else (lambda z: z)
    y, wkv = _rwscan_call(*(seq_major(z) for z in (r, k, v, lw, a)), p["k_k"], p["k_a"], p["r_k"], p["gn_w"], p["gn_b"],
                          wkv_all, wkv_new_all, layer, *scan_cfg)
    (x_new,) = _mm_call("rwkv_out", [seq_major(y), gg], [], [], [(p["w_o"], 0)], [x], [gm], [], d, 1,
                        _pro_mul, _epi_residual, nb, tt, tn=d, resident_w=True)
    return x_new, (h_tail[0] if nb > 1 else h_tail[:, -1]), wkv


def _gla_mixer(x, g, shift, scale, gm, s0, p, nb, tt):
    d = x.shape[-1]
    n_main = p["w_main"].shape[1]
    i_nb, i_tt = (nb, tt) if nb > 1 else (1, min(x.shape[1], GLA_IN_ROWS))
    (qkvr,) = _mm_call("gla_in", [x], [shift, scale], [g], [(p["w_main"], 0)], [], [], [], n_main, 1,
                       _pro_normmod, _epi_id, i_nb, i_tt)
    (a_low,) = _mm_call("gla_gate_in", [x], [shift, scale], [g], [(p["w_gate"], 0)], [], [], [], GLA_LORA_PAD, 1,
                        _pro_normmod, _epi_id, nb, tt)

    def epi_la(accs, erows, emods, evecs):
        return [jax.nn.log_sigmoid(accs[0] + evecs[0]) / GLA_GATE_NORM]

    dk = p["a_w2"].shape[1]
    (la,) = _mm_call("gla_gate", [a_low], [], [], [(p["a_w2"], 0)], [], [], [p["a_b"]], dk, 1, _pro_id, epi_la, nb, tt)
    hb = min(s0.shape[1], GLA_SCAN_HEADS_SAMPLE if nb > 1 else GLA_SCAN_HEADS_PROMPT)
    seq_major = (lambda z: z.transpose(1, 0, 2)) if nb > 1 else (lambda z: z)
    o, s_new = _glascan_call(seq_major(qkvr), seq_major(la), p["o_norm"], s0, GLA_SCAN_TBLK, hb)
    (x_new,) = _mm_call("gla_out", [seq_major(o)], [], [], [(p["w_o"], 0)], [x], [gm], [], d, 1,
                        _pro_id, _epi_residual, nb, tt, tn=d, resident_w=True)
    return x_new, s_new


def _s5_mixer(x, g, shift, scale, gm, x0_re, x0_im, p, nb, tt):
    d = x.shape[-1]
    s5_nb = min(x.shape[1], S5_SEQS_SAMPLE) if nb > 1 else 1
    yg, xre, xim = _s5scan_call(x, g, shift, scale, p["wbu"], p["wc"], p["ab_re"], p["ab_im"], p["d_skip"],
                                x0_re, x0_im, S5_TBLK, s5_nb)
    g_nb, g_tt = (max(1, nb // 2), tt) if nb > 1 else (1, min(tt, S5_GLU_ROWS))
    (x_new,) = _mm_call("s5_glu", [yg], [], [], [(p["glu_w"], 0), (p["glu_w"], 1)], [x], [gm],
                        [p["glu_b_val"], p["glu_b_gate"]], d, 1, _pro_id, _epi_glu, g_nb, g_tt, tn=d, resident_w=True)
    return x_new, xre.reshape(x0_re.shape), xim.reshape(x0_im.shape)


def _s5_params(a_re, a_im, log_dt, b_re, b_im, c_re, c_im, d_skip, glu_w, glu_b):
    g, n, cg = b_re.shape
    d = g * cg
    ab_re, ab_im, bb_re, bb_im = _s5prep_call(a_re, a_im, log_dt, b_re.transpose(0, 2, 1), b_im.transpose(0, 2, 1))
    sg = S5_SLAB_GROUPS
    eye = jnp.eye(sg, dtype=f32)

    def bdiag(m):
        gq, rr, cc = m.shape
        m = m.reshape(gq // sg, sg, rr, cc)
        return jnp.einsum("sjrc,jk->sjrkc", m, eye).reshape(gq // sg, sg * rr, sg * cc)

    wbu = jnp.concatenate([bdiag(bb_re), bdiag(bb_im)], axis=2).astype(bf16)
    wc = jnp.concatenate([bdiag(c_re.transpose(0, 2, 1)), bdiag(-c_im.transpose(0, 2, 1))], axis=1).astype(bf16)
    return dict(wbu=wbu, wc=wc, ab_re=ab_re.reshape(1, g * n), ab_im=ab_im.reshape(1, g * n),
                d_skip=d_skip.reshape(1, d), glu_w=glu_w.astype(bf16),
                glu_b_val=glu_b[:d].reshape(1, 1, d), glu_b_gate=glu_b[d:].reshape(1, 1, d))


def _trunk(x, ada, states, weights, nb, tt, rw_scan_cfg, ffn_cfg, ffn_bf16):
    d = x.shape[-1]
    depth, _, nseq, _ = ada.shape
    wkv, shift, gla, s5_re, s5_im = states
    new = dict(shift=[], gla=[], re=[], im=[])
    new_wkv = None
    ia = ib = ic = 0
    vec = (lambda l, j: ada[l, j]) if nb > 1 else (lambda l, j: ada[l, j].reshape(nseq, 1, d))

    def ffn(x_, g_, shift_, scale_, gate_, l_, s_):
        if (l_, s_) in ffn_bf16:
            return _ffn_call(x_, g_, shift_, scale_, gate_, ("bf16",) + ffn_bf16[(l_, s_)], *ffn_cfg)
        y_, ffn_bf16[(l_, s_)] = _ffn_call(x_, g_, shift_, scale_, gate_,
                                          ("f32", weights["ffn_w_in"], weights["ffn_w_out"], l_, s_), *ffn_cfg)
        return y_

    for l in range(depth):
        ng = lambda s: weights["norm_g"][l, s].reshape(1, 1, d)
        x = ffn(x, ng(0), vec(l, 0), vec(l, 1), vec(l, 2), l, 0)
        kind = l % 3
        if kind == 0:
            x, sh, new_wkv = _rwkv_mixer(x, ng(1), vec(l, 3), vec(l, 4), vec(l, 5), shift[ia], wkv, new_wkv, ia,
                                         weights["rw"][ia], nb, tt, rw_scan_cfg)
            new["shift"].append(sh)
            ia += 1
        elif kind == 1:
            x, st = _gla_mixer(x, ng(1), vec(l, 3), vec(l, 4), vec(l, 5), gla[ib], weights["gla"][ib], nb, tt)
            new["gla"].append(st)
            ib += 1
        else:
            x, sr, si = _s5_mixer(x, ng(1), vec(l, 3), vec(l, 4), vec(l, 5), s5_re[ic], s5_im[ic], weights["s5"][ic], nb, tt)
            new["re"].append(sr)
            new["im"].append(si)
            ic += 1
        x = ffn(x, ng(2), vec(l, 6), vec(l, 7), vec(l, 8), l, 1)
    y = _final_norm_call(x, weights["final_g"].reshape(1, 1, d), nb, tt)
    return (y, new_wkv, jnp.stack(new["shift"]), jnp.stack(new["gla"]),
            jnp.stack(new["re"]), jnp.stack(new["im"]))


def _pad_to(a, axis, size):
    pad = [(0, 0)] * a.ndim
    pad[axis] = (0, size - a.shape[axis])
    return jnp.pad(a, pad)


def kernel(x_prompt, x_sample, state_rwkv_wkv, state_rwkv_shift, state_gla, state_s5_re, state_s5_im, c_prompt, c_sample, norm_g, ada_w, ada_b, ffn_w_in, ffn_w_out, rw_mu, rw_w_rkv, rw_w0, rw_w1, rw_w2, rw_a0, rw_a1, rw_a2, rw_g1, rw_g2, rw_k_k, rw_k_a, rw_r_k, rw_gn_w, rw_gn_b, rw_w_o, gla_w_in, gla_a_w2, gla_a_b, gla_o_norm, gla_w_o, s5_A_re, s5_A_im, s5_log_dt, s5_B_re, s5_B_im, s5_C_re, s5_C_im, s5_D, s5_glu_w, s5_glu_b, final_g):
    bp, t_p, d = x_prompt.shape
    bs, t_s, _ = x_sample.shape
    depth = ada_w.shape[0]
    n_a, n_b, n_c = rw_mu.shape[0], gla_w_in.shape[0], s5_A_re.shape[0]

    n_c_rows = bp + bs
    c_all = _pad_to(jnp.concatenate([c_sample, c_prompt], axis=0), 0, -(-n_c_rows // SUBLANE) * SUBLANE)
    ada = _ada_call(c_all, ada_w, ada_b, d)
    ada_s = ada[:, :, :bs]
    ada_p = ada[:, :, bs:n_c_rows]

    rw = []
    for i in range(n_a):
        vec = lambda a: a[i].reshape(1, 1, d)
        rw.append(dict(
            mu=rw_mu[i], wr=rw_w_rkv[i, 0].astype(bf16), wk=rw_w_rkv[i, 1].astype(bf16), wv=rw_w_rkv[i, 2].astype(bf16),
            w1=_pad_to(rw_w1[i], 1, RW_LORA_PAD).astype(bf16), w2=_pad_to(rw_w2[i], 0, RW_LORA_PAD).astype(bf16),
            a1=_pad_to(rw_a1[i], 1, RW_LORA_PAD).astype(bf16), a2=_pad_to(rw_a2[i], 0, RW_LORA_PAD).astype(bf16),
            g1=rw_g1[i].astype(bf16), g2=rw_g2[i].astype(bf16), w0=vec(rw_w0), a0=vec(rw_a0),
            k_k=vec(rw_k_k), k_a=vec(rw_k_a), r_k=rw_r_k[i].reshape(1, 1, d), gn_w=vec(rw_gn_w), gn_b=vec(rw_gn_b),
            w_o=rw_w_o[i].astype(bf16)))
    gl = []
    for i in range(n_b):
        n_main = gla_w_in.shape[2] - gla_a_w2.shape[1]
        gl.append(dict(
            w_main=gla_w_in[i, :, :n_main].astype(bf16),
            w_gate=_pad_to(gla_w_in[i, :, n_main:], 1, GLA_LORA_PAD).astype(bf16),
            a_w2=_pad_to(gla_a_w2[i], 0, GLA_LORA_PAD).astype(bf16), a_b=gla_a_b[i].reshape(1, 1, -1),
            o_norm=gla_o_norm[i].reshape(1, -1), w_o=gla_w_o[i].astype(bf16)))
    s5 = [_s5_params(s5_A_re[i], s5_A_im[i], s5_log_dt[i], s5_B_re[i], s5_B_im[i], s5_C_re[i], s5_C_im[i],
                     s5_D[i], s5_glu_w[i], s5_glu_b[i]) for i in range(n_c)]
    weights = dict(norm_g=norm_g, ffn_w_in=ffn_w_in, ffn_w_out=ffn_w_out,
                   rw=rw, gla=gl, s5=s5, final_g=final_g)

    zeros = lambda s: jnp.zeros((s.shape[0], bp) + s.shape[2:], f32)
    p_states = tuple(zeros(s) for s in (state_rwkv_wkv, state_rwkv_shift, state_gla, state_s5_re, state_s5_im))
    s_states = (state_rwkv_wkv, state_rwkv_shift, state_gla, state_s5_re, state_s5_im)

    tt_p = min(ROW_TILE, t_p)
    nb_s = min(bs, ROW_TILE // t_s)
    rw_chunk = min(RW_HEAD, t_p)
    n_heads = d // RW_HEAD
    nb_scan = max(1, min(bs, RW_HEAD // t_s))
    ffn_bf16 = {}
    y_s, s_wkv, s_shift, s_gla, s_re, s_im = _trunk(x_sample.transpose(1, 0, 2), ada_s, s_states, weights, nb_s, t_s,
                                                     (nb_scan, t_s, t_s, min(RW_SCAN_HEADS_SAMPLE, n_heads)),
                                                     (min(bs, FFN_ROWS_SAMPLE // t_s), t_s), ffn_bf16)
    y_s = y_s.transpose(1, 0, 2)
    y_p, p_wkv, p_shift, p_gla, p_re, p_im = _trunk(x_prompt, ada_p, p_states, weights, 1, tt_p,
                                                     (1, rw_chunk, min(RW_SCAN_TBLK, t_p), min(RW_SCAN_HEADS_PROMPT, n_heads)),
                                                     (1, min(FFN_ROWS, t_p)), ffn_bf16)
    return (y_p, y_s, p_wkv, p_shift, p_gla, p_re, p_im, s_wkv, s_shift, s_gla, s_re, s_im)
```
